```python
import math
import jax, jax.numpy as jnp
from jax import lax
import numpy as np

D_MODEL = 1024
BATCH = 4
SEQ = 4096
DEPTH = 1
DEC_BATCH = 32
DEC_SEQ = 8
PAST_LEN = 16384
PAGE_SIZE = 128

N_HEADS_A = 4
DK = 64
DV = 2 * DK
ROT = DK // 4
ROPE_THETA = 500000.0
A_WIDTH = N_HEADS_A * DV
QK_WIDTH = N_HEADS_A * 2 * DK
B_WIDTH = D_MODEL - A_WIDTH
N_GROUPS_B = 4
C_B = B_WIDTH // N_GROUPS_B
CHUNK = 128
IN_WIDTH = 2 * QK_WIDTH + A_WIDTH + 2 * B_WIDTH
N_EXPERT_GROUPS = 4
EXPERTS_PER_GROUP = 8
N_EXPERTS = N_EXPERT_GROUPS * EXPERTS_PER_GROUP
TOP_K = 2
D_EXPERT = 512
EXPERT_BLOCK = 128
Q_BLOCK = 128
EPS = 1e-6
NEG = -1e30

kernel_name = "hymba_diffattn_chunkgmlp_hmoe_step"


def rms_norm(x, g):
    xf = x.astype(jnp.float32)
    y = xf * lax.rsqrt(jnp.mean(xf * xf, axis=-1, keepdims=True) + EPS)
    return (y * g.astype(jnp.float32)).astype(x.dtype)


def modulate(x, g, shift, scale):
    return rms_norm(x, g) * (1 + scale[:, None, :]) + shift[:, None, :]


def partial_rope(x, pos):
    inv = ROPE_THETA ** (-jnp.arange(0, ROT, 2, dtype=jnp.float32) / ROT)
    ang = pos.astype(jnp.float32)[:, None] * inv[None, :]
    cos = jnp.cos(ang)[:, None, None, :]
    sin = jnp.sin(ang)[:, None, None, :]
    xr = x[..., :ROT].astype(jnp.float32)
    x1, x2 = xr[..., :ROT // 2], xr[..., ROT // 2:]
    rot = jnp.concatenate([x1 * cos - x2 * sin, x2 * cos + x1 * sin], axis=-1).astype(x.dtype)
    return jnp.concatenate([rot, x[..., ROT:]], axis=-1)


def project_in(h, w_in, pos):
    z = h @ w_in
    lead = z.shape[:-1]
    q = z[..., :QK_WIDTH].reshape(*lead, N_HEADS_A, 2, DK)
    k = z[..., QK_WIDTH:2 * QK_WIDTH].reshape(*lead, N_HEADS_A, 2, DK)
    v = z[..., 2 * QK_WIDTH:2 * QK_WIDTH + A_WIDTH].reshape(*lead, N_HEADS_A, DV)
    uv = jax.nn.gelu(z[..., 2 * QK_WIDTH + A_WIDTH:])
    u = uv[..., :B_WIDTH].reshape(*lead, N_GROUPS_B, C_B)
    gv = uv[..., B_WIDTH:].reshape(*lead, N_GROUPS_B, C_B)
    return partial_rope(q, pos), partial_rope(k, pos), v, u, gv


def diff_lambda(lam_p, lam_init):
    lp = lam_p.astype(jnp.float32)
    return jnp.exp(jnp.sum(lp[0] * lp[1])) - jnp.exp(jnp.sum(lp[2] * lp[3])) + lam_init


def diff_attn_prompt(q, k, v, lam):
    b, s = q.shape[:2]
    n_qb = s // Q_BLOCK
    scale = DK ** -0.5
    qb = q.reshape(b, n_qb, Q_BLOCK, N_HEADS_A, 2, DK).transpose(1, 0, 2, 3, 4, 5)
    kpos = jnp.arange(s)

    def block(args):
        qi, i = args
        sc = jnp.einsum('bqhmd,bkhmd->bhmqk', qi, k, preferred_element_type=jnp.float32) * scale
        qpos = i * Q_BLOCK + jnp.arange(Q_BLOCK)
        sc = jnp.where(kpos[None, :] <= qpos[:, None], sc, NEG)
        p = jax.nn.softmax(sc, axis=-1)
        a = (p[:, :, 0] - lam * p[:, :, 1]).astype(v.dtype)
        return jnp.einsum('bhqk,bkhe->bqhe', a, v)

    o = lax.map(block, (qb, jnp.arange(n_qb)))
    return o.transpose(1, 0, 2, 3, 4).reshape(b, s, N_HEADS_A, DV)


def diff_attn_sample(q, k_new, v_new, k_past, v_past, lam):
    ds = q.shape[1]
    n_past = k_past.shape[1]
    scale = DK ** -0.5
    s_past = jnp.einsum('bqhmd,bkhmd->bhmqk', q, k_past, preferred_element_type=jnp.float32) * scale
    s_new = jnp.einsum('bqhmd,bkhmd->bhmqk', q, k_new, preferred_element_type=jnp.float32) * scale
    causal = jnp.tril(jnp.ones((ds, ds), dtype=bool))
    s_new = jnp.where(causal, s_new, NEG)
    p = jax.nn.softmax(jnp.concatenate([s_past, s_new], axis=-1), axis=-1)
    a = (p[:, :, 0] - lam * p[:, :, 1]).astype(v_new.dtype)
    return (jnp.einsum('bhqk,bkhe->bqhe', a[..., :n_past], v_past)
            + jnp.einsum('bhqk,bkhe->bqhe', a[..., n_past:], v_new))


def diff_head_out(o, g_attn, lam_init):
    o = rms_norm(o, g_attn) * (1 - lam_init)
    return o.reshape(*o.shape[:-2], A_WIDTH)


def chunk_spatial_gate(u, gv, g_v, g_mlp, w_s, b_s, chunk_len):
    n, s = u.shape[:2]
    nc = s // chunk_len
    u = u.reshape(n, nc, chunk_len, N_GROUPS_B, C_B)
    v = rms_norm(gv.reshape(n, nc, chunk_len, N_GROUPS_B, C_B), g_v.reshape(N_GROUPS_B, C_B))
    ws = (w_s * jnp.tril(jnp.ones((CHUNK, CHUNK), w_s.dtype)))[:, :chunk_len, :chunk_len]
    mixed = jnp.einsum('gts,bnsgc->bntgc', ws, v) + b_s[:, :chunk_len].T[None, None, :, :, None]
    out = rms_norm(u * mixed, g_mlp.reshape(N_GROUPS_B, C_B))
    return out.reshape(n, s, B_WIDTH), v.reshape(n, s, B_WIDTH)


def routed_experts(t, expert, gate, w_eg, w_eu, w_ed):
    n_tok = t.shape[0]
    n_asg = n_tok * TOP_K
    flat_e = expert.reshape(-1)
    flat_tok = jnp.arange(n_asg, dtype=jnp.int32) // TOP_K
    flat_w = gate.reshape(-1)
    order = jnp.argsort(flat_e)
    se = flat_e[order]
    counts = jnp.bincount(flat_e, length=N_EXPERTS)
    padded = ((counts + EXPERT_BLOCK - 1) // EXPERT_BLOCK) * EXPERT_BLOCK
    pend = jnp.cumsum(padded)
    pstart = pend - padded
    start = jnp.cumsum(counts) - counts
    dest = pstart[se] + jnp.arange(n_asg) - start[se]
    n_blocks = (n_asg + N_EXPERTS * (EXPERT_BLOCK - 1) + EXPERT_BLOCK - 1) // EXPERT_BLOCK
    n_rows = n_blocks * EXPERT_BLOCK
    row_tok = jnp.full((n_rows,), n_tok, jnp.int32).at[dest].set(flat_tok[order])
    row_w = jnp.zeros((n_rows,), jnp.float32).at[dest].set(flat_w[order])
    block_e = jnp.minimum(jnp.searchsorted(pend, jnp.arange(n_blocks) * EXPERT_BLOCK, side='right'),
                          N_EXPERTS - 1)
    x_pad = jnp.concatenate([t, jnp.zeros((1, D_MODEL), t.dtype)], axis=0)
    xb = x_pad[row_tok].reshape(n_blocks, EXPERT_BLOCK, D_MODEL)

    def expert_block(args):
        xblk, e = args
        return (jax.nn.silu(xblk @ w_eg[e]) * (xblk @ w_eu[e])) @ w_ed[e]

    yb = lax.map(expert_block, (xb, block_e)).reshape(n_rows, D_MODEL)
    out = jnp.zeros((n_tok + 1, D_MODEL), t.dtype).at[row_tok].add(yb * row_w[:, None].astype(yb.dtype))
    return out[:n_tok]


def hier_moe(h, w_rg, b_rg, w_re, b_re, w_eg, w_eu, w_ed):
    lead = h.shape[:-1]
    t = h.reshape(-1, D_MODEL)
    n_tok = t.shape[0]
    g_prob = jax.nn.softmax((t @ w_rg + b_rg).astype(jnp.float32), axis=-1)
    g_p, g_idx = lax.top_k(g_prob, 1)
    e_logits = (t @ w_re + b_re).astype(jnp.float32).reshape(n_tok, N_EXPERT_GROUPS, EXPERTS_PER_GROUP)
    e_logits = e_logits[jnp.arange(n_tok), g_idx[:, 0]]
    e_top, e_idx = lax.top_k(e_logits, TOP_K)
    gate = jax.nn.softmax(e_top, axis=-1) * g_p
    expert = g_idx * EXPERTS_PER_GROUP + e_idx
    y = routed_experts(t, expert, gate, w_eg, w_eu, w_ed)
    return y.reshape(*lead, D_MODEL)


def decoder_layer(x, c, pos, chunk_len, attend, lam, lam_init, p):
    mod = jnp.split(jax.nn.silu(c) @ p['w_ada'] + p['b_ada'], 6, axis=-1)
    h = modulate(x, p['g_norm1'], mod[0], mod[1])
    q, k, v, u, gv = project_in(h, p['w_in'], pos)
    o_a = diff_head_out(attend(q, k, v, lam), p['g_attn'], lam_init)
    o_b, v_rows = chunk_spatial_gate(u, gv, p['g_v'], p['g_mlp'], p['w_s'], p['b_s'], chunk_len)
    mix = jnp.concatenate([o_a, o_b], axis=-1) @ p['w_out']
    x = x + mod[2][:, None, :] * mix
    h2 = modulate(x, p['g_norm2'], mod[3], mod[4])
    ff = hier_moe(h2, p['w_router_g'], p['b_router_g'], p['w_router_e'], p['b_router_e'],
                  p['w_exp_gate'], p['w_exp_up'], p['w_exp_down'])
    x = x + mod[5][:, None, :] * ff
    return x, k, v, v_rows


def setup_inputs(seed: int = 0) -> dict:
    key = jax.random.key(seed)
    ks = jax.random.split(key, 32)
    n_pages = PAST_LEN // PAGE_SIZE
    n_used = DEC_BATCH * n_pages
    n_pool = n_used + n_used // 4
    f32 = jnp.float32

    def nrm(k, shape, s):
        return jax.random.normal(k, shape, f32) * s

    def gain(k, shape):
        return 1.0 + 0.1 * jax.random.normal(k, shape, f32)

    page_table = jax.random.permutation(ks[4], n_pool)[:n_used].reshape(DEC_BATCH, n_pages).astype(jnp.int32)
    return {
        "x_prompt": nrm(ks[0], (BATCH, SEQ, D_MODEL), 1.0),
        "x_sample": nrm(ks[1], (DEC_BATCH, DEC_SEQ, D_MODEL), 1.0),
        "cache_k": nrm(ks[2], (DEPTH, n_pool, PAGE_SIZE, N_HEADS_A, 2, DK), 1.0),
        "cache_v": nrm(ks[3], (DEPTH, n_pool, PAGE_SIZE, N_HEADS_A, DV), 1.0),
        "page_table": page_table,
        "c_prompt": nrm(ks[5], (BATCH, D_MODEL), 1.0),
        "c_sample": nrm(ks[6], (DEC_BATCH, D_MODEL), 1.0),
        "w_ada": nrm(ks[7], (DEPTH, D_MODEL, 6 * D_MODEL), D_MODEL ** -0.5),
        "b_ada": nrm(ks[8], (DEPTH, 6 * D_MODEL), 0.02),
        "g_norm1": gain(ks[9], (DEPTH, D_MODEL)),
        "w_in": nrm(ks[10], (DEPTH, D_MODEL, IN_WIDTH), D_MODEL ** -0.5),
        "lam_p": nrm(ks[11], (DEPTH, 4, DK), 0.1),
        "g_attn": gain(ks[12], (DEPTH, N_HEADS_A, DV)),
        "g_v": gain(ks[13], (DEPTH, B_WIDTH)),
        "w_s": nrm(ks[14], (DEPTH, N_GROUPS_B, CHUNK, CHUNK), CHUNK ** -0.5),
        "b_s": gain(ks[15], (DEPTH, N_GROUPS_B, CHUNK)),
        "g_mlp": gain(ks[16], (DEPTH, B_WIDTH)),
        "w_out": nrm(ks[17], (DEPTH, D_MODEL, D_MODEL), D_MODEL ** -0.5),
        "g_norm2": gain(ks[18], (DEPTH, D_MODEL)),
        "w_router_g": nrm(ks[19], (DEPTH, D_MODEL, N_EXPERT_GROUPS), D_MODEL ** -0.5),
        "b_router_g": nrm(ks[20], (DEPTH, N_EXPERT_GROUPS), 0.01),
        "w_router_e": nrm(ks[21], (DEPTH, D_MODEL, N_EXPERTS), D_MODEL ** -0.5),
        "b_router_e": nrm(ks[22], (DEPTH, N_EXPERTS), 0.01),
        "w_exp_gate": nrm(ks[23], (DEPTH, N_EXPERTS, D_MODEL, D_EXPERT), D_MODEL ** -0.5),
        "w_exp_up": nrm(ks[24], (DEPTH, N_EXPERTS, D_MODEL, D_EXPERT), D_MODEL ** -0.5),
        "w_exp_down": nrm(ks[25], (DEPTH, N_EXPERTS, D_EXPERT, D_MODEL), D_EXPERT ** -0.5),
        "g_final": gain(ks[26], (D_MODEL,)),
    }


def reference(x_prompt, x_sample, cache_k, cache_v, page_table, c_prompt, c_sample,
              w_ada, b_ada, g_norm1, w_in, lam_p, g_attn, g_v, w_s, b_s, g_mlp, w_out,
              g_norm2, w_router_g, b_router_g, w_router_e, b_router_e,
              w_exp_gate, w_exp_up, w_exp_down, g_final):
    n_seq_p = x_prompt.shape[1]
    n_seq_s = x_sample.shape[1]
    n_dec = x_sample.shape[0]
    pos_p = jnp.arange(n_seq_p)
    pos_s = PAST_LEN + jnp.arange(n_seq_s)
    xp, xs = x_prompt, x_sample
    kp_l, vp_l, ks_l, vs_l, cv_l = [], [], [], [], []
    for l in range(DEPTH):
        lam_init = 0.8 - 0.6 * math.exp(-0.3 * l)
        lam = diff_lambda(lam_p[l], lam_init)
        p = {
            'w_ada': w_ada[l], 'b_ada': b_ada[l], 'g_norm1': g_norm1[l], 'w_in': w_in[l],
            'g_attn': g_attn[l], 'g_v': g_v[l], 'w_s': w_s[l], 'b_s': b_s[l], 'g_mlp': g_mlp[l],
            'w_out': w_out[l], 'g_norm2': g_norm2[l],
            'w_router_g': w_router_g[l], 'b_router_g': b_router_g[l],
            'w_router_e': w_router_e[l], 'b_router_e': b_router_e[l],
            'w_exp_gate': w_exp_gate[l], 'w_exp_up': w_exp_up[l], 'w_exp_down': w_exp_down[l],
        }
        xp, k_p, v_p, _ = decoder_layer(xp, c_prompt, pos_p, CHUNK, diff_attn_prompt, lam, lam_init, p)
        k_past = cache_k[l][page_table].reshape(n_dec, -1, N_HEADS_A, 2, DK)
        v_past = cache_v[l][page_table].reshape(n_dec, -1, N_HEADS_A, DV)

        def attend_sample(q, k, v, lam_, k_past=k_past, v_past=v_past):
            return diff_attn_sample(q, k, v, k_past, v_past, lam_)

        xs, k_s, v_s, cv_s = decoder_layer(xs, c_sample, pos_s, n_seq_s, attend_sample, lam, lam_init, p)
        kp_l.append(k_p)
        vp_l.append(v_p)
        ks_l.append(k_s)
        vs_l.append(v_s)
        cv_l.append(cv_s)
    y_prompt = rms_norm(xp, g_final)
    y_sample = rms_norm(xs, g_final)
    k_prompt = jnp.stack(kp_l)
    v_prompt = jnp.stack(vp_l)
    k_sample = jnp.stack(ks_l)
    v_sample = jnp.stack(vs_l)
    chunk_v_sample = jnp.stack(cv_l)
    return (y_prompt, y_sample, k_prompt, v_prompt, k_sample, v_sample, chunk_v_sample)
```

```python
import functools
import math

import jax
import jax.numpy as jnp
from jax import lax
from jax.experimental import pallas as pl
from jax.experimental.pallas import tpu as pltpu

f32 = jnp.float32
bf16 = jnp.bfloat16
i32 = jnp.int32
SDS = jax.ShapeDtypeStruct

N_HEADS = 4
DK = 64
DV = 2 * DK
ROT = DK // 4
ROPE_THETA = 500000.0
N_GROUPS_B = 4
C_B = 128
CHUNK = 128
N_EXPERT_GROUPS = 4
EXPERTS_PER_GROUP = 8
N_EXPERTS = N_EXPERT_GROUPS * EXPERTS_PER_GROUP
PAGE = 128
EPS = 1e-6
NEG = -1e30

LANES = 128
SUBLANES = 8
VMEM_LIMIT = 56 * 1024 * 1024

TOKEN_TILE = 512
ATTN_TQ = 256
ATTN_TK = 256
PAGES_PER_STEP = 16
EXPERT_BLOCK = 128
RUN_ALIGN = 8
ROUTER_ROWS = 40


def _cparams(sem=None):
    return pltpu.CompilerParams(dimension_semantics=sem, vmem_limit_bytes=VMEM_LIMIT)


def _round_up(x, m):
    return (x + m - 1) // m * m


def _adaln_kernel(c_ref, w_ref, b_ref, o_ref):
    a = jax.nn.silu(c_ref[...])
    o_ref[...] = jnp.dot(a, w_ref[...], preferred_element_type=f32,
                         precision=lax.Precision.HIGHEST) + b_ref[...]


def _adaln(c_all, w_ada, b_ada):
    n, d = c_all.shape
    m = w_ada.shape[1]
    tn = 1536
    return pl.pallas_call(
        _adaln_kernel,
        grid=(m // tn,),
        in_specs=[pl.BlockSpec((n, d), lambda j: (0, 0)),
                  pl.BlockSpec((d, tn), lambda j: (0, j)),
                  pl.BlockSpec((1, tn), lambda j: (0, j))],
        out_specs=pl.BlockSpec((n, tn), lambda j: (0, j)),
        out_shape=SDS((n, m), f32),
        compiler_params=_cparams(("arbitrary",)),
        name="adaln",
    )(c_all, w_ada, b_ada.reshape(1, m))


def _rope_kernel(inv_ref, cos_ref, sin_ref, *, rows_prompt, past_len):
    shape = cos_ref.shape
    r = lax.broadcasted_iota(i32, shape, 0)
    l = lax.broadcasted_iota(i32, shape, 1)
    base = jnp.where(r < rows_prompt, r * 16, past_len + (r - rows_prompt) * 16)
    pos = base + (l >> 3)
    ang = pos.astype(f32) * inv_ref[...]
    cos_ref[...] = jnp.cos(ang)
    sin_ref[...] = jnp.sin(ang)


def _rope_tables(seq, dec_seq, past_len):
    inv = ROPE_THETA ** (-jnp.arange(0, ROT, 2, dtype=f32) / ROT)
    inv_lane = jnp.tile(inv, LANES // (ROT // 2)).reshape(1, LANES)
    rp = seq // 16
    rt = rp + SUBLANES
    cos_c, sin_c = pl.pallas_call(
        functools.partial(_rope_kernel, rows_prompt=rp, past_len=past_len),
        out_shape=(SDS((rt, LANES), f32), SDS((rt, LANES), f32)),
        name="rope_table",
    )(inv_lane)

    def expand(c8, s8):
        n = c8.shape[0]
        z8 = jnp.zeros((n, ROT // 2), f32)
        rest = DK - ROT
        cos_t = jnp.concatenate([c8, c8, jnp.ones((n, rest), f32)], axis=1)
        sin_a = jnp.concatenate([-s8, z8, jnp.zeros((n, rest), f32)], axis=1)
        sin_b = jnp.concatenate([z8, s8, jnp.zeros((n, rest), f32)], axis=1)
        return tuple(jnp.tile(t, (1, LANES // DK)) for t in (cos_t, sin_a, sin_b))

    half = ROT // 2
    prompt = expand(cos_c[:rp].reshape(seq, half), sin_c[:rp].reshape(seq, half))
    sample = expand(cos_c[rp].reshape(16, half)[:dec_seq], sin_c[rp].reshape(16, half)[:dec_seq])
    return prompt, sample


def _rms(x, g):
    return x * lax.rsqrt(jnp.mean(x * x, axis=-1, keepdims=True) + EPS) * g


def _in_proj_kernel(x_ref, sh_ref, sc_ref, g1_ref, w_ref, cos_ref, sa_ref, sb_ref,
                    ws_ref, msk_ref, bs_ref, gv_ref, gm_ref,
                    q_ref, k_ref, v_ref, ob_ref, *vn_refs, mix_rows):
    x = x_ref[0]
    tm = x.shape[0]
    h = _rms(x, g1_ref[...]) * (1.0 + sc_ref[0]) + sh_ref[0]
    z = jnp.dot(h.astype(bf16), w_ref[...], preferred_element_type=f32)
    qk_w = N_HEADS * 2 * DK
    a_w = N_HEADS * DV
    b_w = N_GROUPS_B * C_B
    cos_t, sin_a, sin_b = cos_ref[...], sa_ref[...], sb_ref[...]
    for s in range(2 * qk_w // LANES):
        zs = z[:, s * LANES:(s + 1) * LANES]
        rot = zs * cos_t + pltpu.roll(zs, LANES - ROT // 2, 1) * sin_a + pltpu.roll(zs, ROT // 2, 1) * sin_b
        if s < qk_w // LANES:
            q_ref[0, :, s * LANES:(s + 1) * LANES] = (rot * (DK ** -0.5)).astype(bf16)
        else:
            k_ref[0, :, s * LANES - qk_w:(s + 1) * LANES - qk_w] = rot
    v_ref[0] = z[:, 2 * qk_w:2 * qk_w + a_w]
    uv = jax.nn.gelu(z[:, 2 * qk_w + a_w:])
    nblk = tm // mix_rows
    for g in range(N_GROUPS_B):
        u = uv[:, g * C_B:(g + 1) * C_B]
        vn = _rms(uv[:, b_w + g * C_B:b_w + (g + 1) * C_B], gv_ref[:, g * C_B:(g + 1) * C_B])
        if vn_refs:
            vn_refs[0][0, :, g * C_B:(g + 1) * C_B] = vn
        wm = (ws_ref[g] * msk_ref[...]).astype(bf16)
        vcat = jnp.concatenate([vn[j * mix_rows:(j + 1) * mix_rows] for j in range(nblk)], axis=1)
        mixed = jnp.dot(wm, vcat.astype(bf16), preferred_element_type=f32)
        for j in range(nblk):
            t = u[j * mix_rows:(j + 1) * mix_rows] * (mixed[:, j * C_B:(j + 1) * C_B] + bs_ref[g])
            ob_ref[0, j * mix_rows:(j + 1) * mix_rows, g * C_B:(g + 1) * C_B] = _rms(
                t, gm_ref[:, g * C_B:(g + 1) * C_B]).astype(bf16)


def _in_proj(x, shift, scale, g1, w_in_bf, tables, ws_t, mask, bs_b, g_v, g_mlp, *, tm, emit_vn):
    n, t, d = x.shape
    in_w = w_in_bf.shape[1]
    r = ws_t.shape[1]
    mod_rows = shift.shape[1]
    tab_rows = tables[0].shape[0]
    w_half = N_HEADS * DV

    def mod_spec():
        if mod_rows == 1:
            return pl.BlockSpec((1, 1, d), lambda b, i: (b, 0, 0))
        return pl.BlockSpec((1, tm, d), lambda b, i: (b, i, 0))

    def tab_spec():
        if tab_rows == tm:
            return pl.BlockSpec((tm, LANES), lambda b, i: (0, 0))
        return pl.BlockSpec((tm, LANES), lambda b, i: (i, 0))

    full = lambda shp: pl.BlockSpec(shp, lambda b, i: (0,) * len(shp))
    tok = lambda w: pl.BlockSpec((1, tm, w), lambda b, i: (b, i, 0))
    out_shape = [SDS((n, t, w_half), bf16), SDS((n, t, w_half), f32), SDS((n, t, w_half), f32),
                 SDS((n, t, w_half), bf16)]
    out_specs = [tok(w_half)] * 4
    if emit_vn:
        out_shape.append(SDS((n, t, w_half), f32))
        out_specs.append(tok(w_half))
    return pl.pallas_call(
        functools.partial(_in_proj_kernel, mix_rows=r),
        grid=(n, t // tm),
        in_specs=[tok(d), mod_spec(), mod_spec(), full((1, d)), full((d, in_w)),
                  tab_spec(), tab_spec(), tab_spec(),
                  full((N_GROUPS_B, r, r)), full((r, r)), full((N_GROUPS_B, r, C_B)),
                  full((1, w_half)), full((1, w_half))],
        out_specs=out_specs,
        out_shape=out_shape,
        compiler_params=_cparams(("arbitrary", "arbitrary")),
        name="in_proj",
    )(x, shift, scale, g1, w_in_bf, *tables, ws_t, mask, bs_b, g_v, g_mlp)


def _diff_lambda(lam_ref, lam_init):
    lp = lam_ref[...]
    s1 = jnp.sum(lp[0:1] * lp[1:2], axis=1, keepdims=True)
    s2 = jnp.sum(lp[2:3] * lp[3:4], axis=1, keepdims=True)
    return jnp.exp(s1) - jnp.exp(s2) + lam_init


def _attn_prompt_kernel(lam_ref, g_ref, q_ref, k_ref, v_ref, o_ref,
                        kb_sc, vt_sc, acc_sc, m_sc, l_sc, *, lam_init):
    seq = q_ref.shape[1]
    tq, tk = ATTN_TQ, ATTN_TK
    lam = _diff_lambda(lam_ref, lam_init)
    gain = g_ref[0] * (1.0 - lam_init)

    def prep(j, c):
        sl = pl.ds(pl.multiple_of(j * tk, tk), tk)
        kb_sc[sl, :] = k_ref[0, sl, :].astype(bf16)
        vt_sc[j] = v_ref[0, sl, :].T.astype(bf16)
        return c

    lax.fori_loop(0, seq // tk, prep, 0)
    lane = lax.broadcasted_iota(i32, (tq, DV), 1)

    def q_block(qi, c):
        qsl = pl.ds(pl.multiple_of(qi * tq, tq), tq)
        qb = q_ref[0, qsl, :]
        zero = jnp.zeros_like(qb)
        qq = jnp.concatenate([jnp.where(lane < DK, qb, zero), jnp.where(lane >= DK, qb, zero)], axis=0)
        m_sc[...] = jnp.full(m_sc.shape, NEG, f32)
        l_sc[...] = jnp.zeros(l_sc.shape, f32)
        acc_sc[...] = jnp.zeros(acc_sc.shape, f32)

        def kv_block(j, masked):
            kb = kb_sc[pl.ds(pl.multiple_of(j * tk, tk), tk), :]
            s = lax.dot_general(kb, qq, (((1,), (1,)), ((), ())), preferred_element_type=f32)
            if masked:
                key_i = lax.broadcasted_iota(i32, s.shape, 0)
                qry_i = lax.broadcasted_iota(i32, s.shape, 1) & (tq - 1)
                s = jnp.where(key_i <= qry_i, s, NEG)
            m_prev = m_sc[...]
            m_new = jnp.maximum(m_prev, jnp.max(s, axis=0, keepdims=True))
            alpha = jnp.exp(m_prev - m_new)
            p = jnp.exp(s - m_new)
            l_sc[...] = alpha * l_sc[...] + jnp.sum(p, axis=0, keepdims=True)
            pv = jnp.dot(vt_sc[j], p.astype(bf16), preferred_element_type=f32)
            acc_sc[...] = acc_sc[...] * alpha + pv
            m_sc[...] = m_new

        def full_block(j, cc):
            kv_block(j, False)
            return cc

        lax.fori_loop(0, qi, full_block, 0)
        kv_block(qi, True)
        inv_l = 1.0 / l_sc[...]
        o_t = acc_sc[:, :tq] * inv_l[:, :tq] - lam * (acc_sc[:, tq:] * inv_l[:, tq:])
        o_ref[0, qsl, :] = _rms(o_t.T, gain).astype(bf16)
        return c

    lax.fori_loop(0, seq // tq, q_block, 0)


def _attn_prompt(lam_p, g_attn, q, k, v, lam_init):
    n, seq, w = q.shape
    blk = pl.BlockSpec((1, seq, DV), lambda b, h: (b, 0, h))
    return pl.pallas_call(
        functools.partial(_attn_prompt_kernel, lam_init=lam_init),
        grid=(n, N_HEADS),
        in_specs=[pl.BlockSpec(lam_p.shape, lambda b, h: (0, 0)),
                  pl.BlockSpec((1, 1, DV), lambda b, h: (h, 0, 0)),
                  blk, blk, blk],
        out_specs=blk,
        out_shape=SDS((n, seq, w), bf16),
        scratch_shapes=[pltpu.VMEM((seq, DV), bf16),
                        pltpu.VMEM((seq // ATTN_TK, DV, ATTN_TK), bf16),
                        pltpu.VMEM((DV, 2 * ATTN_TQ), f32),
                        pltpu.VMEM((1, 2 * ATTN_TQ), f32),
                        pltpu.VMEM((1, 2 * ATTN_TQ), f32)],
        compiler_params=_cparams(("arbitrary", "arbitrary")),
        name="attn_prompt",
    )(lam_p, g_attn.reshape(N_HEADS, 1, DV), q, k, v)


def _attn_sample_kernel(pt_ref, lam_ref, g_ref, q_ref, kn_ref, vn_ref, *rest, lam_init, n_pages):
    k_pages = rest[:n_pages]
    v_pages = rest[n_pages:2 * n_pages]
    o_ref, qbd_sc, m_sc, l_sc, acc_sc = rest[2 * n_pages:]
    step = pl.program_id(1)
    ds = q_ref.shape[1]
    rows = N_HEADS * 2 * ds
    width = q_ref.shape[2]
    row_i = lax.broadcasted_iota(i32, (rows, width), 0)
    col_i = lax.broadcasted_iota(i32, (rows, width), 1)

    @pl.when(step == 0)
    def _():
        qt = jnp.concatenate([q_ref[0].astype(f32)] * (N_HEADS * 2), axis=0)
        same_map = (col_i >> (DK.bit_length() - 1)) == (row_i >> (ds.bit_length() - 1))
        qbd_sc[...] = jnp.where(same_map, qt, 0.0).astype(bf16)
        m_sc[...] = jnp.full(m_sc.shape, NEG, f32)
        l_sc[...] = jnp.zeros(l_sc.shape, f32)
        acc_sc[...] = jnp.zeros(acc_sc.shape, f32)

    qbd = qbd_sc[...]

    def update(s, values):
        m_prev = m_sc[...]
        m_new = jnp.maximum(m_prev, jnp.max(s, axis=1, keepdims=True))
        alpha = jnp.exp(m_prev - m_new)
        p = jnp.exp(s - m_new)
        l_sc[...] = alpha * l_sc[...] + jnp.sum(p, axis=1, keepdims=True)
        pb = p.astype(bf16)
        pv = None
        for i, vb in enumerate(values):
            t = jnp.dot(pb[:, i * PAGE:(i + 1) * PAGE], vb, preferred_element_type=f32)
            pv = t if pv is None else pv + t
        acc_sc[...] = acc_sc[...] * alpha + pv
        m_sc[...] = m_new

    nt = (((1,), (1,)), ((), ()))
    s_past = jnp.concatenate(
        [lax.dot_general(qbd, kp[0].astype(bf16), nt, preferred_element_type=f32) for kp in k_pages], axis=1)
    update(s_past, [vp[0].astype(bf16) for vp in v_pages])

    @pl.when(step == pl.num_programs(1) - 1)
    def _():
        pad = jnp.zeros((PAGE - ds, width), f32)
        kn = jnp.concatenate([kn_ref[0], pad], axis=0).astype(bf16)
        vn = jnp.concatenate([vn_ref[0], pad], axis=0).astype(bf16)
        s_new = lax.dot_general(qbd, kn, nt, preferred_element_type=f32)
        key = lax.broadcasted_iota(i32, s_new.shape, 1)
        qry = lax.broadcasted_iota(i32, s_new.shape, 0) & (ds - 1)
        update(jnp.where(key <= qry, s_new, NEG), [vn])
        lam = _diff_lambda(lam_ref, lam_init)
        out = acc_sc[...] * (1.0 / l_sc[...])
        for h in range(N_HEADS):
            cols = slice(h * DV, (h + 1) * DV)
            o = out[2 * h * ds:(2 * h + 1) * ds, cols] - lam * out[(2 * h + 1) * ds:(2 * h + 2) * ds, cols]
            o_ref[0, :, cols] = _rms(o, g_ref[h:h + 1, :] * (1.0 - lam_init)).astype(bf16)


def _attn_sample(page_table, lam_p, g_attn, q, k_new, v_new, cache_k, cache_v, lam_init):
    nseq, ds, w = q.shape
    pages = page_table.shape[1]
    pps = PAGES_PER_STEP
    assert pages % pps == 0 and ds & (ds - 1) == 0
    seq_blk = pl.BlockSpec((1, ds, w), lambda b, g, pt: (b, 0, 0))

    def page_spec(i):
        return pl.BlockSpec((1, PAGE, w), lambda b, g, pt: (pt[b, g * pps + i], 0, 0))

    rows = N_HEADS * 2 * ds
    grid_spec = pltpu.PrefetchScalarGridSpec(
        num_scalar_prefetch=1,
        grid=(nseq, pages // pps),
        in_specs=[pl.BlockSpec(lam_p.shape, lambda b, g, pt: (0, 0)),
                  pl.BlockSpec(g_attn.shape, lambda b, g, pt: (0, 0)),
                  seq_blk, seq_blk, seq_blk]
                 + [page_spec(i) for i in range(pps)] * 2,
        out_specs=seq_blk,
        scratch_shapes=[pltpu.VMEM((rows, w), bf16), pltpu.VMEM((rows, 1), f32),
                        pltpu.VMEM((rows, 1), f32), pltpu.VMEM((rows, w), f32)],
    )
    return pl.pallas_call(
        functools.partial(_attn_sample_kernel, lam_init=lam_init, n_pages=pps),
        grid_spec=grid_spec,
        out_shape=SDS((nseq, ds, w), bf16),
        compiler_params=_cparams(("arbitrary", "arbitrary")),
        name="attn_sample",
    )(page_table, lam_p, g_attn, q, k_new, v_new, *([cache_k] * pps), *([cache_v] * pps))


def _out_route_kernel(oa_ref, ob_ref, x_ref, g1_ref, sh_ref, sc_ref, wa_ref, wb_ref, g2_ref, wr_ref, br_ref,
                      x1_ref, h2_ref, ids_ref, gates_ref, cnt_ref):
    mix = (jnp.dot(oa_ref[0], wa_ref[...], preferred_element_type=f32)
           + jnp.dot(ob_ref[0], wb_ref[...], preferred_element_type=f32))
    x1 = x_ref[0] + g1_ref[0] * mix
    x1_ref[0] = x1
    h2 = _rms(x1, g2_ref[...]) * (1.0 + sc_ref[0]) + sh_ref[0]
    h2_ref[0] = h2.astype(bf16)
    tm = h2.shape[0]
    wr = wr_ref[...]
    wr_hi = wr.astype(bf16)
    wr_lo = (wr - wr_hi.astype(f32)).astype(bf16)
    h_hi = h2.astype(bf16)
    h_lo = (h2 - h_hi.astype(f32)).astype(bf16)
    nt = (((1,), (1,)), ((), ()))
    r1 = lax.dot_general(jnp.concatenate([wr_hi, wr_lo], axis=0), h_hi, nt, preferred_element_type=f32)
    r2 = lax.dot_general(wr_hi, h_lo, nt, preferred_element_type=f32)
    lg = r1[:ROUTER_ROWS] + r1[ROUTER_ROWS:] + r2 + br_ref[:, 0:1]
    row = lax.broadcasted_iota(i32, (SUBLANES, tm), 0).astype(f32)
    big = float(SUBLANES)
    gl = jnp.where(row < N_EXPERT_GROUPS, lg[0:SUBLANES], NEG)
    gmax = jnp.max(gl, axis=0, keepdims=True)
    g_p = 1.0 / jnp.sum(jnp.exp(gl - gmax), axis=0, keepdims=True)
    gidx = jnp.min(jnp.where(gl == gmax, row, big), axis=0, keepdims=True)
    esel = jnp.zeros((SUBLANES, tm), f32)
    for g in range(N_EXPERT_GROUPS):
        esel = jnp.where(gidx == float(g), lg[SUBLANES * (g + 1):SUBLANES * (g + 2)], esel)
    e1 = jnp.max(esel, axis=0, keepdims=True)
    i1 = jnp.min(jnp.where(esel == e1, row, big), axis=0, keepdims=True)
    esel2 = jnp.where(row == i1, -jnp.inf, esel)
    e2 = jnp.max(esel2, axis=0, keepdims=True)
    i2 = jnp.min(jnp.where(esel2 == e2, row, big), axis=0, keepdims=True)
    t = jnp.exp(e2 - e1)
    w1 = g_p / (1.0 + t)
    ids = jnp.concatenate([gidx * EXPERTS_PER_GROUP + i1, gidx * EXPERTS_PER_GROUP + i2], axis=0).astype(i32)
    ids_ref[0] = ids
    gates_ref[0] = jnp.concatenate([w1, w1 * t], axis=0)
    e_iota = lax.broadcasted_iota(i32, (N_EXPERTS, tm), 0)
    onehot = jnp.where((e_iota == ids[0:1]) | (e_iota == ids[1:2]), 1.0, 0.0).astype(bf16)
    cnt_ref[0] = jnp.dot(onehot, jnp.ones((tm, LANES), bf16), preferred_element_type=f32)


def _out_route(oa, ob, x, gate1, shift2, scale2, w_a, w_b, g2, wr_t, br_b, *, tm):
    n, t, d = x.shape
    w_half = oa.shape[2]
    mod_rows = gate1.shape[1]
    tpn = t // tm

    def mod_spec():
        if mod_rows == 1:
            return pl.BlockSpec((1, 1, d), lambda b, i: (b, 0, 0))
        return pl.BlockSpec((1, tm, d), lambda b, i: (b, i, 0))

    full = lambda shp: pl.BlockSpec(shp, lambda b, i: (0,) * len(shp))
    tok = lambda w: pl.BlockSpec((1, tm, w), lambda b, i: (b, i, 0))
    return pl.pallas_call(
        _out_route_kernel,
        grid=(n, tpn),
        in_specs=[tok(w_half), tok(w_half), tok(d), mod_spec(), mod_spec(), mod_spec(),
                  full((w_half, d)), full((w_half, d)), full((1, d)),
                  full((ROUTER_ROWS, d)), full((ROUTER_ROWS, LANES))],
        out_specs=[tok(d), tok(d),
                   pl.BlockSpec((1, 2, tm), lambda b, i: (b, 0, i)),
                   pl.BlockSpec((1, 2, tm), lambda b, i: (b, 0, i)),
                   pl.BlockSpec((1, N_EXPERTS, LANES), lambda b, i: (b * tpn + i, 0, 0))],
        out_shape=[SDS((n, t, d), f32), SDS((n, t, d), bf16), SDS((n, 2, t), i32), SDS((n, 2, t), f32),
                   SDS((n * tpn, N_EXPERTS, LANES), f32)],
        compiler_params=_cparams(("arbitrary", "arbitrary")),
        name="out_route",
    )(oa, ob, x, gate1, shift2, scale2, w_a, w_b, g2, wr_t, br_b)


def _route_plan(cnt, nb_max):
    pc = _round_up(cnt, RUN_ALIGN)
    toff = jnp.cumsum(pc, axis=1) - pc
    tot = jnp.sum(pc, axis=0)
    seg = _round_up(tot, EXPERT_BLOCK)
    seg_end = jnp.cumsum(seg)
    seg_start = seg_end - seg
    run_start = seg_start[None, :] + jnp.cumsum(pc, axis=0) - pc
    nb = seg_end[-1] // EXPERT_BLOCK
    blk = jnp.arange(nb_max, dtype=i32)
    be = jnp.minimum(jnp.searchsorted(seg_end // EXPERT_BLOCK, blk, side="right"), N_EXPERTS - 1).astype(i32)
    be = jnp.where(blk < nb, be, be[jnp.maximum(nb - 1, 0)])
    return dict(
        pc8=(pc // RUN_ALIGN).reshape(-1).astype(i32), toff=toff.reshape(-1).astype(i32),
        run_start=run_start.reshape(-1).astype(i32), ntot8=(jnp.sum(pc, axis=1) // RUN_ALIGN).astype(i32),
        tail_start=(seg_start + tot).astype(i32), tail8=((seg - tot) // RUN_ALIGN).astype(i32),
        tail_tot8=(jnp.sum(seg - tot) // RUN_ALIGN).reshape(1).astype(i32),
        block_e=be, nb=nb.reshape(1).astype(i32),
        toff_v=jnp.broadcast_to(toff.astype(f32)[:, :, None], toff.shape + (LANES,)))


def _sorted_positions(ids_ref, toff_ref, upper_ref, tm):
    ids = ids_ref[0]
    idc = jnp.concatenate([ids[0:1], ids[1:2]], axis=1)
    e_iota = lax.broadcasted_iota(i32, (N_EXPERTS, 2 * tm), 0)
    onehot = jnp.where(e_iota == idc, 1.0, 0.0)
    before = jnp.dot(onehot.astype(bf16), upper_ref[...], preferred_element_type=f32)
    return jnp.sum(onehot * (before + toff_ref[0][:, 0:1]), axis=0, keepdims=True)


def _fill_upper(upper_ref):
    n = upper_ref.shape[0]
    a_src = lax.broadcasted_iota(i32, (n, n), 0)
    a_dst = lax.broadcasted_iota(i32, (n, n), 1)
    upper_ref[...] = jnp.where(a_src < a_dst, 1.0, 0.0).astype(bf16)


def _aligned(row):
    return row if isinstance(row, int) else pl.multiple_of(row, RUN_ALIGN)


def _run_chunk_copy(vmem_buf, hbm_buf, sem, vrow, hrow, to_hbm):
    v = vmem_buf.at[pl.ds(_aligned(vrow), RUN_ALIGN)]
    h = hbm_buf.at[pl.ds(_aligned(hrow), RUN_ALIGN)]
    return pltpu.make_async_copy(v, h, sem) if to_hbm else pltpu.make_async_copy(h, v, sem)


def _move_runs(i, pc8_ref, toff_s_ref, rstart_ref, ntot8_ref, vmem_buf, hbm_buf, sem, to_hbm):
    def per_expert(e, c):
        idx = i * N_EXPERTS + e
        vrow0 = toff_s_ref[idx]
        hrow0 = rstart_ref[idx]

        def per_chunk(k, cc):
            _run_chunk_copy(vmem_buf, hbm_buf, sem, vrow0 + k * RUN_ALIGN, hrow0 + k * RUN_ALIGN, to_hbm).start()
            return cc

        lax.fori_loop(0, pc8_ref[idx], per_chunk, 0)
        return c

    lax.fori_loop(0, N_EXPERTS, per_expert, 0)

    def wait_one(k, c):
        _run_chunk_copy(vmem_buf, hbm_buf, sem, 0, 0, to_hbm).wait()
        return c

    lax.fori_loop(0, ntot8_ref[i], wait_one, 0)


def _dispatch_kernel(pc8_ref, toff_s_ref, rstart_ref, ntot8_ref, tstart_ref, tail8_ref, ttot_ref, nb_ref,
                     h_ref, ids_ref, toff_ref, xs_ref, upper_sc, xbuf_sc, zero_sc, sem):
    i = pl.program_id(0)
    tm = h_ref.shape[0]
    rt = xbuf_sc.shape[0]

    @pl.when(i == 0)
    def _():
        _fill_upper(upper_sc)
        zero_sc[...] = jnp.zeros(zero_sc.shape, f32)

    pos = _sorted_positions(ids_ref, toff_ref, upper_sc, tm)
    r_iota = lax.broadcasted_iota(i32, (rt, tm), 0).astype(f32)
    perm = jnp.where((r_iota == pos[:, :tm]) | (r_iota == pos[:, tm:]), 1.0, 0.0).astype(bf16)
    xbuf_sc[...] = jnp.dot(perm, h_ref[...], preferred_element_type=f32)
    _move_runs(i, pc8_ref, toff_s_ref, rstart_ref, ntot8_ref, xbuf_sc, xs_ref, sem.at[0], True)

    @pl.when(i == pl.num_programs(0) - 1)
    def _():
        def per_expert(e, c):
            def per_chunk(k, cc):
                _run_chunk_copy(zero_sc, xs_ref, sem.at[0], 0, tstart_ref[e] + k * RUN_ALIGN, True).start()
                return cc
            lax.fori_loop(0, tail8_ref[e], per_chunk, 0)
            return c
        lax.fori_loop(0, N_EXPERTS, per_expert, 0)

        def wait_one(k, c):
            _run_chunk_copy(zero_sc, xs_ref, sem.at[0], 0, 0, True).wait()
            return c
        lax.fori_loop(0, ttot_ref[0], wait_one, 0)

        def dead_block(row):
            return pltpu.make_async_copy(zero_sc, xs_ref.at[pl.ds(row, EXPERT_BLOCK)], sem.at[1])

        n_dead = xs_ref.shape[0] // EXPERT_BLOCK - nb_ref[0]

        def start_dead(k, c):
            dead_block(pl.multiple_of((nb_ref[0] + k) * EXPERT_BLOCK, EXPERT_BLOCK)).start()
            return c
        lax.fori_loop(0, n_dead, start_dead, 0)

        def wait_dead(k, c):
            dead_block(0).wait()
            return c
        lax.fori_loop(0, n_dead, wait_dead, 0)


def _sorted_rows(tm):
    return _round_up(2 * tm + N_EXPERTS * (RUN_ALIGN - 1), LANES)


def _dispatch(plan, h2_flat, ids, toff_v, rows_total, *, tm):
    tokens, d = h2_flat.shape
    n, _, t = ids.shape
    tpn = t // tm
    rt = _sorted_rows(tm)
    grid_spec = pltpu.PrefetchScalarGridSpec(
        num_scalar_prefetch=8,
        grid=(tokens // tm,),
        in_specs=[pl.BlockSpec((tm, d), lambda i, *_: (i, 0)),
                  pl.BlockSpec((1, 2, tm), lambda i, *_: (i // tpn, 0, i % tpn)),
                  pl.BlockSpec((1, N_EXPERTS, LANES), lambda i, *_: (i, 0, 0))],
        out_specs=pl.BlockSpec(memory_space=pl.ANY),
        scratch_shapes=[pltpu.VMEM((2 * tm, 2 * tm), bf16), pltpu.VMEM((rt, d), f32),
                        pltpu.VMEM((EXPERT_BLOCK, d), f32), pltpu.SemaphoreType.DMA((2,))],
    )
    return pl.pallas_call(
        _dispatch_kernel,
        grid_spec=grid_spec,
        out_shape=SDS((rows_total, d), f32),
        compiler_params=_cparams(("arbitrary",)),
        name="dispatch",
    )(plan["pc8"], plan["toff"], plan["run_start"], plan["ntot8"], plan["tail_start"], plan["tail8"],
      plan["tail_tot8"], plan["nb"], h2_flat, ids, toff_v)


def _experts_kernel(be_ref, nb_ref, x_ref, wg_ref, wu_ref, wd_ref, y_ref, wg_sc, wu_sc, wd_sc):
    j = pl.program_id(0)
    live = j < nb_ref[0]

    @pl.when(live)
    def _():
        prev = be_ref[jnp.maximum(j - 1, 0)]

        @pl.when((j == 0) | (be_ref[j] != prev))
        def _():
            wg_sc[...] = wg_ref[0].astype(bf16)
            wu_sc[...] = wu_ref[0].astype(bf16)
            wd_sc[...] = wd_ref[0].astype(bf16)

        xb = x_ref[...].astype(bf16)
        a = jnp.dot(xb, wg_sc[...], preferred_element_type=f32)
        b = jnp.dot(xb, wu_sc[...], preferred_element_type=f32)
        hid = (jax.nn.silu(a) * b).astype(bf16)
        y_ref[...] = jnp.dot(hid, wd_sc[...], preferred_element_type=f32)

    @pl.when(jnp.logical_not(live))
    def _():
        y_ref[...] = jnp.zeros(y_ref.shape, f32)


def _experts(plan, xs, w_gate, w_up, w_down, nb_max):
    d = xs.shape[1]
    de = w_gate.shape[2]
    eb = EXPERT_BLOCK
    grid_spec = pltpu.PrefetchScalarGridSpec(
        num_scalar_prefetch=2,
        grid=(nb_max,),
        in_specs=[pl.BlockSpec((eb, d), lambda j, be, nb: (jnp.minimum(j, nb[0] - 1), 0)),
                  pl.BlockSpec((1, d, de), lambda j, be, nb: (be[j], 0, 0)),
                  pl.BlockSpec((1, d, de), lambda j, be, nb: (be[j], 0, 0)),
                  pl.BlockSpec((1, de, d), lambda j, be, nb: (be[j], 0, 0))],
        out_specs=pl.BlockSpec((eb, d), lambda j, be, nb: (j, 0)),
        scratch_shapes=[pltpu.VMEM((d, de), bf16), pltpu.VMEM((d, de), bf16), pltpu.VMEM((de, d), bf16)],
    )
    return pl.pallas_call(
        _experts_kernel,
        grid_spec=grid_spec,
        out_shape=SDS((nb_max * eb, d), f32),
        compiler_params=_cparams(("arbitrary",)),
        name="experts",
    )(plan["block_e"], plan["nb"], xs, w_gate, w_up, w_down)


def _combine_kernel(pc8_ref, toff_s_ref, rstart_ref, ntot8_ref,
                    x1_ref, g2_ref, gf_ref, ids_ref, gates_ref, toff_ref, y_ref,
                    x2_ref, yn_ref, upper_sc, ybuf_sc, sem):
    i = pl.program_id(0)
    tm = x1_ref.shape[0]
    rt = ybuf_sc.shape[0]

    @pl.when(i == 0)
    def _():
        _fill_upper(upper_sc)
        ybuf_sc[...] = jnp.zeros(ybuf_sc.shape, f32)

    _move_runs(i, pc8_ref, toff_s_ref, rstart_ref, ntot8_ref, ybuf_sc, y_ref, sem.at[0], False)
    pos = _sorted_positions(ids_ref, toff_ref, upper_sc, tm)
    gates = gates_ref[0]
    r_iota = lax.broadcasted_iota(i32, (rt, tm), 0).astype(f32)
    weights = (jnp.where(r_iota == pos[:, :tm], gates[0:1], 0.0)
               + jnp.where(r_iota == pos[:, tm:], gates[1:2], 0.0)).astype(bf16)
    ff = lax.dot_general(weights, ybuf_sc[...].astype(bf16), (((0,), (0,)), ((), ())),
                         preferred_element_type=f32)
    x2 = x1_ref[...] + g2_ref[0] * ff
    x2_ref[...] = x2
    yn_ref[...] = _rms(x2, gf_ref[...])


def _combine(plan, x1_flat, gate2, g_final, ids, gates, toff_v, y, *, tm):
    tokens, d = x1_flat.shape
    n, _, t = ids.shape
    tpn = t // tm
    rt = _sorted_rows(tm)
    mod_rows = gate2.shape[1]
    if mod_rows == 1:
        mod_spec = pl.BlockSpec((1, 1, d), lambda i, *_: (i // tpn, 0, 0))
    else:
        mod_spec = pl.BlockSpec((1, tm, d), lambda i, *_: (i // tpn, i % tpn, 0))
    slot_spec = pl.BlockSpec((1, 2, tm), lambda i, *_: (i // tpn, 0, i % tpn))
    tok_spec = pl.BlockSpec((tm, d), lambda i, *_: (i, 0))
    grid_spec = pltpu.PrefetchScalarGridSpec(
        num_scalar_prefetch=4,
        grid=(tokens // tm,),
        in_specs=[tok_spec, mod_spec, pl.BlockSpec((1, d), lambda i, *_: (0, 0)), slot_spec, slot_spec,
                  pl.BlockSpec((1, N_EXPERTS, LANES), lambda i, *_: (i, 0, 0)),
                  pl.BlockSpec(memory_space=pl.ANY)],
        out_specs=[tok_spec, tok_spec],
        scratch_shapes=[pltpu.VMEM((2 * tm, 2 * tm), bf16), pltpu.VMEM((rt, d), f32),
                        pltpu.SemaphoreType.DMA((1,))],
    )
    return pl.pallas_call(
        _combine_kernel,
        grid_spec=grid_spec,
        out_shape=[SDS((tokens, d), f32), SDS((tokens, d), f32)],
        compiler_params=_cparams(("arbitrary",)),
        name="combine",
    )(plan["pc8"], plan["toff"], plan["run_start"], plan["ntot8"],
      x1_flat, gate2, g_final, ids, gates, toff_v, y)


def _moe(x1, h2, ids, gates, cnt, gate2, g_final, w_gate, w_up, w_down, *, tm):
    n, t, d = x1.shape
    tiles = n * t // tm
    rows_max = 2 * n * t + tiles * N_EXPERTS * (RUN_ALIGN - 1) + N_EXPERTS * (EXPERT_BLOCK - 1)
    nb_max = -(-rows_max // EXPERT_BLOCK)
    plan = _route_plan(cnt[:, :, 0].astype(i32), nb_max)
    xs = _dispatch(plan, h2.reshape(n * t, d), ids, plan["toff_v"], nb_max * EXPERT_BLOCK, tm=tm)
    y = _experts(plan, xs, w_gate, w_up, w_down, nb_max)
    x2, yn = _combine(plan, x1.reshape(n * t, d), gate2, g_final, ids, gates, plan["toff_v"], y, tm=tm)
    return x2.reshape(n, t, d), yn.reshape(n, t, d)


def _layer_group(x, mods, tables, attend, p, g_final, *, tm, mix, emit_vn):
    ws_t, mask, bs_b = mix
    outs = _in_proj(x, mods[0], mods[1], p["g1"], p["w_in"], tables, ws_t, mask, bs_b, p["g_v"], p["g_mlp"],
                    tm=tm, emit_vn=emit_vn)
    q, k, v, ob = outs[:4]
    oa = attend(q, k, v)
    x1, h2, ids, gates, cnt = _out_route(oa, ob, x, mods[2], mods[3], mods[4], p["w_out_a"], p["w_out_b"],
                                         p["g2"], p["wr_t"], p["br_b"], tm=tm)
    x2, yn = _moe(x1, h2, ids, gates, cnt, mods[5], g_final, p["w_gate"], p["w_up"], p["w_down"], tm=tm)
    return x2, yn, k, v, (outs[4] if emit_vn else None)


def kernel(x_prompt, x_sample, cache_k, cache_v, page_table, c_prompt, c_sample, w_ada, b_ada, g_norm1, w_in,
           lam_p, g_attn, g_v, w_s, b_s, g_mlp, w_out, g_norm2, w_router_g, b_router_g, w_router_e, b_router_e,
           w_exp_gate, w_exp_up, w_exp_down, g_final):
    nb, seq, d = x_prompt.shape
    ndec, dseq, _ = x_sample.shape
    depth = w_ada.shape[0]
    past_len = page_table.shape[1] * PAGE
    n_pool = cache_k.shape[1]
    a_w = N_HEADS * DV
    tm_s = ndec * dseq

    tables_p, tables_s8 = _rope_tables(seq, dseq, past_len)
    tables_s = tuple(jnp.tile(t, (ndec, 1)) for t in tables_s8)
    c_all = jnp.concatenate([c_prompt, c_sample], axis=0)
    c_pad = _round_up(c_all.shape[0], SUBLANES) - c_all.shape[0]
    c_all = jnp.pad(c_all, ((0, c_pad), (0, 0)))

    tri = jnp.tril(jnp.ones((CHUNK, CHUNK), f32))
    idx = jnp.arange(tm_s)
    mask_s = ((idx[:, None] // dseq == idx[None, :] // dseq) & (idx[None, :] % dseq <= idx[:, None] % dseq)).astype(f32)

    xp, xs = x_prompt, x_sample.reshape(1, tm_s, d)
    yp = ys = None
    kp_l, vp_l, ks_l, vs_l, cv_l = [], [], [], [], []
    for l in range(depth):
        lam_init = 0.8 - 0.6 * math.exp(-0.3 * l)
        mod = _adaln(c_all, w_ada[l], b_ada[l])
        mods_p = [mod[:nb, None, j * d:(j + 1) * d] for j in range(6)]
        mods_s = [jnp.repeat(mod[nb:nb + ndec, j * d:(j + 1) * d], dseq, axis=0)[None] for j in range(6)]
        wr_t = jnp.concatenate([w_router_g[l].T, jnp.zeros((SUBLANES - N_EXPERT_GROUPS, d), f32),
                                w_router_e[l].T], axis=0)
        br = jnp.concatenate([b_router_g[l], jnp.zeros((SUBLANES - N_EXPERT_GROUPS,), f32), b_router_e[l]])
        p = dict(
            g1=g_norm1[l].reshape(1, d), w_in=w_in[l].astype(bf16), g_v=g_v[l].reshape(1, -1),
            g_mlp=g_mlp[l].reshape(1, -1), w_out_a=w_out[l][:a_w].astype(bf16), w_out_b=w_out[l][a_w:].astype(bf16),
            g2=g_norm2[l].reshape(1, d), wr_t=wr_t, br_b=jnp.broadcast_to(br[:, None], (ROUTER_ROWS, LANES)),
            w_gate=w_exp_gate[l], w_up=w_exp_up[l], w_down=w_exp_down[l])
        gf = g_final.reshape(1, d)
        mix_p = (w_s[l], tri, jnp.broadcast_to(b_s[l][:, :, None], (N_GROUPS_B, CHUNK, C_B)))
        mix_s = (jnp.tile(w_s[l][:, :dseq, :dseq], (1, ndec, ndec)), mask_s,
                 jnp.broadcast_to(jnp.tile(b_s[l][:, :dseq], (1, ndec))[:, :, None], (N_GROUPS_B, tm_s, C_B)))

        attend_p = lambda q, k, v: _attn_prompt(lam_p[l], g_attn[l], q, k, v, lam_init)
        ck = cache_k[l].reshape(n_pool, PAGE, -1)
        cv = cache_v[l].reshape(n_pool, PAGE, -1)

        def attend_s(q, k, v):
            o = _attn_sample(page_table, lam_p[l], g_attn[l], q.reshape(ndec, dseq, -1), k.reshape(ndec, dseq, -1),
                             v.reshape(ndec, dseq, -1), ck, cv, lam_init)
            return o.reshape(1, tm_s, -1)

        xp, yp, k_p, v_p, _ = _layer_group(xp, mods_p, tables_p, attend_p, p, gf, tm=TOKEN_TILE, mix=mix_p,
                                           emit_vn=False)
        xs, ys, k_s, v_s, cv_s = _layer_group(xs, mods_s, tables_s, attend_s, p, gf, tm=tm_s, mix=mix_s,
                                              emit_vn=True)
        kp_l.append(k_p.reshape(nb, seq, N_HEADS, 2, DK))
        vp_l.append(v_p.reshape(nb, seq, N_HEADS, DV))
        ks_l.append(k_s.reshape(ndec, dseq, N_HEADS, 2, DK))
        vs_l.append(v_s.reshape(ndec, dseq, N_HEADS, DV))
        cv_l.append(cv_s.reshape(ndec, dseq, -1))
    return (yp, ys.reshape(ndec, dseq, d), jnp.stack(kp_l), jnp.stack(vp_l), jnp.stack(ks_l), jnp.stack(vs_l),
            jnp.stack(cv_l))
```

```python
import functools
import math

import jax
import jax.numpy as jnp
from jax import lax
from jax.experimental import pallas as pl
from jax.experimental.pallas import tpu as pltpu

f32 = jnp.float32
bf16 = jnp.bfloat16
i32 = jnp.int32
SDS = jax.ShapeDtypeStruct

N_HEADS = 4
DK = 64
DV = 2 * DK
ROT = DK // 4
ROPE_THETA = 500000.0
N_GROUPS_B = 4
C_B = 128
CHUNK = 128
N_EXPERT_GROUPS = 4
EXPERTS_PER_GROUP = 8
N_EXPERTS = N_EXPERT_GROUPS * EXPERTS_PER_GROUP
PAGE = 128
EPS = 1e-6
NEG = -1e30
LOG2E = math.log2(math.e)

LANES = 128
SUBLANES = 8
VMEM_LIMIT = 56 * 1024 * 1024

TOKEN_TILE = 512
ATTN_TQ = 256
ATTN_TK = 256
ONES_ROWS = 16
PAGES_PER_STEP = 16
EXPERT_BLOCK = 256
RUN_ALIGN = 8
ROUTER_ROWS = 40


def _cparams(sem=None):
    return pltpu.CompilerParams(dimension_semantics=sem, vmem_limit_bytes=VMEM_LIMIT)


def _round_up(x, m):
    return (x + m - 1) // m * m


def _adaln_kernel(c_ref, w_ref, b_ref, o_ref):
    a = jax.nn.silu(c_ref[...])
    o_ref[...] = jnp.dot(a, w_ref[...], preferred_element_type=f32,
                         precision=lax.Precision.HIGHEST) + b_ref[...]


def _adaln(c_all, w_ada, b_ada):
    n, d = c_all.shape
    m = w_ada.shape[1]
    tn = 1536
    return pl.pallas_call(
        _adaln_kernel,
        grid=(m // tn,),
        in_specs=[pl.BlockSpec((n, d), lambda j: (0, 0)),
                  pl.BlockSpec((d, tn), lambda j: (0, j)),
                  pl.BlockSpec((1, tn), lambda j: (0, j))],
        out_specs=pl.BlockSpec((n, tn), lambda j: (0, j)),
        out_shape=SDS((n, m), f32),
        compiler_params=_cparams(("arbitrary",)),
        name="adaln",
    )(c_all, w_ada, b_ada.reshape(1, m))


def _rope_kernel(inv_ref, cos_ref, sin_ref, *, rows_prompt, past_len):
    shape = cos_ref.shape
    r = lax.broadcasted_iota(i32, shape, 0)
    l = lax.broadcasted_iota(i32, shape, 1)
    base = jnp.where(r < rows_prompt, r * 16, past_len + (r - rows_prompt) * 16)
    pos = base + (l >> 3)
    ang = pos.astype(f32) * inv_ref[...]
    cos_ref[...] = jnp.cos(ang)
    sin_ref[...] = jnp.sin(ang)


def _rope_tables(seq, dec_seq, past_len):
    inv = ROPE_THETA ** (-jnp.arange(0, ROT, 2, dtype=f32) / ROT)
    inv_lane = jnp.tile(inv, LANES // (ROT // 2)).reshape(1, LANES)
    rp = seq // 16
    rt = rp + SUBLANES
    cos_c, sin_c = pl.pallas_call(
        functools.partial(_rope_kernel, rows_prompt=rp, past_len=past_len),
        out_shape=(SDS((rt, LANES), f32), SDS((rt, LANES), f32)),
        name="rope_table",
    )(inv_lane)

    def expand(c8, s8):
        n = c8.shape[0]
        z8 = jnp.zeros((n, ROT // 2), f32)
        rest = DK - ROT
        cos_t = jnp.concatenate([c8, c8, jnp.ones((n, rest), f32)], axis=1)
        sin_a = jnp.concatenate([-s8, z8, jnp.zeros((n, rest), f32)], axis=1)
        sin_b = jnp.concatenate([z8, s8, jnp.zeros((n, rest), f32)], axis=1)
        return tuple(jnp.tile(t, (1, LANES // DK)) for t in (cos_t, sin_a, sin_b))

    half = ROT // 2
    prompt = expand(cos_c[:rp].reshape(seq, half), sin_c[:rp].reshape(seq, half))
    sample = expand(cos_c[rp].reshape(16, half)[:dec_seq], sin_c[rp].reshape(16, half)[:dec_seq])
    return prompt, sample


def _rms(x, g):
    return x * lax.rsqrt(jnp.mean(x * x, axis=-1, keepdims=True) + EPS) * g


def _in_proj_kernel(x_ref, sh_ref, sc_ref, g1_ref, w_ref, cos_ref, sa_ref, sb_ref,
                    ws_ref, msk_ref, bs_ref, gv_ref, gm_ref,
                    q_ref, k_ref, v_ref, kb_ref, vb_ref, ob_ref, *vn_refs, mix_rows):
    x = x_ref[0]
    tm = x.shape[0]
    h = _rms(x, g1_ref[...]) * (1.0 + sc_ref[0]) + sh_ref[0]
    z = jnp.dot(h.astype(bf16), w_ref[...], preferred_element_type=f32)
    qk_w = N_HEADS * 2 * DK
    a_w = N_HEADS * DV
    b_w = N_GROUPS_B * C_B
    cos_t, sin_a, sin_b = cos_ref[...], sa_ref[...], sb_ref[...]
    for s in range(2 * qk_w // LANES):
        zs = z[:, s * LANES:(s + 1) * LANES]
        rot = zs * cos_t + pltpu.roll(zs, LANES - ROT // 2, 1) * sin_a + pltpu.roll(zs, ROT // 2, 1) * sin_b
        if s < qk_w // LANES:
            q_ref[0, :, s * LANES:(s + 1) * LANES] = (rot * (DK ** -0.5 * LOG2E)).astype(bf16)
        else:
            k_ref[0, :, s * LANES - qk_w:(s + 1) * LANES - qk_w] = rot
            kb_ref[0, :, s * LANES - qk_w:(s + 1) * LANES - qk_w] = rot.astype(bf16)
    v = z[:, 2 * qk_w:2 * qk_w + a_w]
    v_ref[0] = v
    vb_ref[0] = v.astype(bf16)
    uv = jax.nn.gelu(z[:, 2 * qk_w + a_w:])
    nblk = tm // mix_rows
    for g in range(N_GROUPS_B):
        u = uv[:, g * C_B:(g + 1) * C_B]
        vn = _rms(uv[:, b_w + g * C_B:b_w + (g + 1) * C_B], gv_ref[:, g * C_B:(g + 1) * C_B])
        if vn_refs:
            vn_refs[0][0, :, g * C_B:(g + 1) * C_B] = vn
        wm = (ws_ref[g] * msk_ref[...]).astype(bf16)
        vcat = jnp.concatenate([vn[j * mix_rows:(j + 1) * mix_rows] for j in range(nblk)], axis=1)
        mixed = jnp.dot(wm, vcat.astype(bf16), preferred_element_type=f32)
        for j in range(nblk):
            t = u[j * mix_rows:(j + 1) * mix_rows] * (mixed[:, j * C_B:(j + 1) * C_B] + bs_ref[g])
            ob_ref[0, j * mix_rows:(j + 1) * mix_rows, g * C_B:(g + 1) * C_B] = _rms(
                t, gm_ref[:, g * C_B:(g + 1) * C_B]).astype(bf16)


def _in_proj(x, shift, scale, g1, w_in_bf, tables, ws_t, mask, bs_b, g_v, g_mlp, *, tm, emit_vn):
    n, t, d = x.shape
    in_w = w_in_bf.shape[1]
    r = ws_t.shape[1]
    mod_rows = shift.shape[1]
    tab_rows = tables[0].shape[0]
    w_half = N_HEADS * DV

    def mod_spec():
        if mod_rows == 1:
            return pl.BlockSpec((1, 1, d), lambda b, i: (b, 0, 0))
        return pl.BlockSpec((1, tm, d), lambda b, i: (b, i, 0))

    def tab_spec():
        if tab_rows == tm:
            return pl.BlockSpec((tm, LANES), lambda b, i: (0, 0))
        return pl.BlockSpec((tm, LANES), lambda b, i: (i, 0))

    full = lambda shp: pl.BlockSpec(shp, lambda b, i: (0,) * len(shp))
    tok = lambda w: pl.BlockSpec((1, tm, w), lambda b, i: (b, i, 0))
    out_shape = [SDS((n, t, w_half), bf16), SDS((n, t, w_half), f32), SDS((n, t, w_half), f32),
                 SDS((n, t, w_half), bf16), SDS((n, t, w_half), bf16), SDS((n, t, w_half), bf16)]
    out_specs = [tok(w_half)] * 6
    if emit_vn:
        out_shape.append(SDS((n, t, w_half), f32))
        out_specs.append(tok(w_half))
    return pl.pallas_call(
        functools.partial(_in_proj_kernel, mix_rows=r),
        grid=(n, t // tm),
        in_specs=[tok(d), mod_spec(), mod_spec(), full((1, d)), full((d, in_w)),
                  tab_spec(), tab_spec(), tab_spec(),
                  full((N_GROUPS_B, r, r)), full((r, r)), full((N_GROUPS_B, r, C_B)),
                  full((1, w_half)), full((1, w_half))],
        out_specs=out_specs,
        out_shape=out_shape,
        compiler_params=_cparams(("arbitrary", "arbitrary")),
        name="in_proj",
    )(x, shift, scale, g1, w_in_bf, *tables, ws_t, mask, bs_b, g_v, g_mlp)


def _diff_lambda(lam_ref, lam_init):
    lp = lam_ref[...]
    s1 = jnp.sum(lp[0:1] * lp[1:2], axis=1, keepdims=True)
    s2 = jnp.sum(lp[2:3] * lp[3:4], axis=1, keepdims=True)
    return jnp.exp(s1) - jnp.exp(s2) + lam_init


def _attn_prompt_kernel(lam_ref, g_ref, q_ref, k_ref, v_ref, o_ref, vt_sc, qq_sc, acc_sc, m_sc, s_sc, *, lam_init):
    seq = q_ref.shape[1]
    tq, tk = ATTN_TQ, ATTN_TK
    lam = _diff_lambda(lam_ref, lam_init)
    heads = range(N_HEADS)
    nt = (((1,), (1,)), ((), ()))

    def prep(j, c):
        sl = pl.ds(pl.multiple_of(j * tk, tk), tk)
        for h in heads:
            vt_sc[h, j, :DV] = v_ref[0, sl, h * DV:(h + 1) * DV].T
            vt_sc[h, j, DV:] = jnp.ones((ONES_ROWS, tk), bf16)
        return c

    lax.fori_loop(0, seq // tk, prep, 0)
    lane = lax.broadcasted_iota(i32, (tq, DV), 1)

    def q_block(qi, c):
        qsl = pl.ds(pl.multiple_of(qi * tq, tq), tq)
        for h in heads:
            qb = q_ref[0, qsl, h * DV:(h + 1) * DV]
            zero = jnp.zeros_like(qb)
            qq_sc[h, :tq] = jnp.where(lane < DK, qb, zero)
            qq_sc[h, tq:] = jnp.where(lane >= DK, qb, zero)
            m_sc[h] = jnp.full(m_sc.shape[1:], NEG, f32)
            acc_sc[h] = jnp.zeros(acc_sc.shape[1:], f32)

        def scores(j, h):
            ksl = pl.ds(pl.multiple_of(j * tk, tk), tk)
            return lax.dot_general(k_ref[0, ksl, h * DV:(h + 1) * DV], qq_sc[h], nt, preferred_element_type=f32)

        s_sc[...] = scores(0, 0)

        def kv_block(j, masked):
            s_next = None
            for h in heads:
                s = s_sc[...] if h == 0 else s_next
                if h + 1 < N_HEADS:
                    s_next = scores(j, h + 1)
                elif not masked:
                    s_sc[...] = scores(j + 1, 0)
                if masked:
                    key_i = lax.broadcasted_iota(i32, s.shape, 0)
                    qry_i = lax.broadcasted_iota(i32, s.shape, 1) & (tq - 1)
                    s = jnp.where(key_i <= qry_i, s, NEG)
                m_prev = m_sc[h]
                m_new = jnp.maximum(m_prev, jnp.max(s, axis=0, keepdims=True))
                alpha = jnp.exp2(m_prev - m_new)
                p = jnp.exp2(s - m_new)
                pv = jnp.dot(vt_sc[h, j], p.astype(bf16), preferred_element_type=f32)
                acc_sc[h] = acc_sc[h] * alpha + pv
                m_sc[h] = m_new

        def full_block(j, cc):
            kv_block(j, False)
            return cc

        def full_pair(jj, cc):
            kv_block(2 * jj, False)
            kv_block(2 * jj + 1, False)
            return cc

        pairs = qi >> 1
        lax.fori_loop(0, pairs, full_pair, 0)
        lax.fori_loop(2 * pairs, qi, full_block, 0)
        kv_block(qi, True)
        for h in heads:
            acc = acc_sc[h]
            inv_l = 1.0 / acc[DV:DV + 1, :]
            o_t = acc[:DV, :tq] * inv_l[:, :tq] - lam * (acc[:DV, tq:] * inv_l[:, tq:])
            o_ref[0, qsl, h * DV:(h + 1) * DV] = _rms(o_t.T, g_ref[h:h + 1, :] * (1.0 - lam_init)).astype(bf16)
        return c

    lax.fori_loop(0, seq // tq, q_block, 0)


def _attn_prompt(lam_p, g_attn, q, k, v, lam_init):
    n, seq, w = q.shape
    blk = pl.BlockSpec((1, seq, w), lambda b: (b, 0, 0))
    return pl.pallas_call(
        functools.partial(_attn_prompt_kernel, lam_init=lam_init),
        grid=(n,),
        in_specs=[pl.BlockSpec(lam_p.shape, lambda b: (0, 0)),
                  pl.BlockSpec(g_attn.shape, lambda b: (0, 0)),
                  blk, blk, blk],
        out_specs=blk,
        out_shape=SDS((n, seq, w), bf16),
        scratch_shapes=[pltpu.VMEM((N_HEADS, seq // ATTN_TK, DV + ONES_ROWS, ATTN_TK), bf16),
                        pltpu.VMEM((N_HEADS, 2 * ATTN_TQ, DV), bf16),
                        pltpu.VMEM((N_HEADS, DV + ONES_ROWS, 2 * ATTN_TQ), f32),
                        pltpu.VMEM((N_HEADS, 1, 2 * ATTN_TQ), f32),
                        pltpu.VMEM((ATTN_TK, 2 * ATTN_TQ), f32)],
        compiler_params=_cparams(("arbitrary",)),
        name="attn_prompt",
    )(lam_p, g_attn, q, k, v)


def _attn_sample_kernel(pt_ref, lam_ref, g_ref, q_ref, kn_ref, vn_ref, *rest, lam_init, n_pages):
    k_pages = rest[:n_pages]
    v_pages = rest[n_pages:2 * n_pages]
    o_ref, qbd_sc, m_sc, l_sc, acc_sc = rest[2 * n_pages:]
    step = pl.program_id(1)
    ds = q_ref.shape[1]
    rows_h = 2 * ds
    rows = N_HEADS * rows_h
    width = q_ref.shape[2]
    heads = range(N_HEADS)
    row_i = lax.broadcasted_iota(i32, (rows, width), 0)
    col_i = lax.broadcasted_iota(i32, (rows, width), 1)

    @pl.when(step == 0)
    def _():
        qt = jnp.concatenate([q_ref[0].astype(f32)] * (N_HEADS * 2), axis=0)
        same_map = (col_i >> (DK.bit_length() - 1)) == (row_i >> (ds.bit_length() - 1))
        qbd_sc[...] = jnp.where(same_map, qt, 0.0).astype(bf16)
        m_sc[...] = jnp.full(m_sc.shape, NEG, f32)
        l_sc[...] = jnp.zeros(l_sc.shape, f32)
        acc_sc[...] = jnp.zeros(acc_sc.shape, f32)

    qbd = qbd_sc[...]

    def update(s, value):
        m_prev = m_sc[...]
        m_new = jnp.maximum(m_prev, jnp.max(s, axis=1, keepdims=True))
        alpha = jnp.exp2(m_prev - m_new)
        p = jnp.exp2(s - m_new)
        l_sc[...] = alpha * l_sc[...] + jnp.sum(p, axis=1, keepdims=True)
        pb = p.astype(bf16)
        parts = []
        for h in heads:
            ph = pb[h * rows_h:(h + 1) * rows_h]
            t = None
            for i in range(s.shape[1] // PAGE):
                u = jnp.dot(ph[:, i * PAGE:(i + 1) * PAGE], value(i, h), preferred_element_type=f32)
                t = u if t is None else t + u
            parts.append(t)
        acc_sc[...] = acc_sc[...] * alpha + jnp.concatenate(parts, axis=0)
        m_sc[...] = m_new

    s_past = jnp.concatenate(
        [jnp.dot(qbd, kp[0].astype(bf16), preferred_element_type=f32) for kp in k_pages], axis=1)
    update(s_past, lambda i, h: v_pages[i][0, pl.ds(h, PAGE, stride=N_HEADS), :].astype(bf16))

    @pl.when(step == pl.num_programs(1) - 1)
    def _():
        pad = jnp.zeros((PAGE - ds, width), f32)
        kn = jnp.concatenate([kn_ref[0], pad], axis=0).astype(bf16)
        vn = jnp.concatenate([vn_ref[0], pad], axis=0).astype(bf16)
        s_new = lax.dot_general(qbd, kn, (((1,), (1,)), ((), ())), preferred_element_type=f32)
        key = lax.broadcasted_iota(i32, s_new.shape, 1)
        qry = lax.broadcasted_iota(i32, s_new.shape, 0) & (ds - 1)
        update(jnp.where(key <= qry, s_new, NEG), lambda i, h: vn[:, h * DV:(h + 1) * DV])
        lam = _diff_lambda(lam_ref, lam_init)
        out = acc_sc[...] * (1.0 / l_sc[...])
        for h in heads:
            o = out[h * rows_h:h * rows_h + ds] - lam * out[h * rows_h + ds:(h + 1) * rows_h]
            o_ref[0, :, h * DV:(h + 1) * DV] = _rms(o, g_ref[h:h + 1, :] * (1.0 - lam_init)).astype(bf16)


def _attn_sample(page_table, lam_p, g_attn, q, k_new, v_new, cache_kt, cache_vr, lam_init):
    nseq, ds, w = q.shape
    pages = page_table.shape[1]
    pps = PAGES_PER_STEP
    assert pages % pps == 0 and ds & (ds - 1) == 0
    seq_blk = pl.BlockSpec((1, ds, w), lambda b, g, pt: (b, 0, 0))

    def page_spec(i, shape):
        return pl.BlockSpec((1,) + shape, lambda b, g, pt: (pt[b, g * pps + i], 0, 0))

    rows = N_HEADS * 2 * ds
    grid_spec = pltpu.PrefetchScalarGridSpec(
        num_scalar_prefetch=1,
        grid=(nseq, pages // pps),
        in_specs=[pl.BlockSpec(lam_p.shape, lambda b, g, pt: (0, 0)),
                  pl.BlockSpec(g_attn.shape, lambda b, g, pt: (0, 0)),
                  seq_blk, seq_blk, seq_blk]
                 + [page_spec(i, cache_kt.shape[1:]) for i in range(pps)]
                 + [page_spec(i, cache_vr.shape[1:]) for i in range(pps)],
        out_specs=seq_blk,
        scratch_shapes=[pltpu.VMEM((rows, w), bf16), pltpu.VMEM((rows, 1), f32),
                        pltpu.VMEM((rows, 1), f32), pltpu.VMEM((rows, DV), f32)],
    )
    return pl.pallas_call(
        functools.partial(_attn_sample_kernel, lam_init=lam_init, n_pages=pps),
        grid_spec=grid_spec,
        out_shape=SDS((nseq, ds, w), bf16),
        compiler_params=_cparams(("arbitrary", "arbitrary")),
        name="attn_sample",
    )(page_table, lam_p, g_attn, q, k_new, v_new, *([cache_kt] * pps), *([cache_vr] * pps))


def _out_route_kernel(oa_ref, ob_ref, x_ref, g1_ref, sh_ref, sc_ref, wa_ref, wb_ref, g2_ref, wr_ref, br_ref,
                      x1_ref, h2_ref, ids_ref, gates_ref, cnt_ref):
    mix = (jnp.dot(oa_ref[0], wa_ref[...], preferred_element_type=f32)
           + jnp.dot(ob_ref[0], wb_ref[...], preferred_element_type=f32))
    x1 = x_ref[0] + g1_ref[0] * mix
    x1_ref[0] = x1
    h2 = _rms(x1, g2_ref[...]) * (1.0 + sc_ref[0]) + sh_ref[0]
    h2_ref[0] = h2.astype(bf16)
    tm = h2.shape[0]
    wr = wr_ref[...]
    wr_hi = wr.astype(bf16)
    wr_lo = (wr - wr_hi.astype(f32)).astype(bf16)
    h_hi = h2.astype(bf16)
    h_lo = (h2 - h_hi.astype(f32)).astype(bf16)
    nt = (((1,), (1,)), ((), ()))
    r1 = lax.dot_general(jnp.concatenate([wr_hi, wr_lo], axis=0), h_hi, nt, preferred_element_type=f32)
    r2 = lax.dot_general(wr_hi, h_lo, nt, preferred_element_type=f32)
    lg = r1[:ROUTER_ROWS] + r1[ROUTER_ROWS:] + r2 + br_ref[:, 0:1]
    row = lax.broadcasted_iota(i32, (SUBLANES, tm), 0).astype(f32)
    big = float(SUBLANES)
    gl = jnp.where(row < N_EXPERT_GROUPS, lg[0:SUBLANES], NEG)
    gmax = jnp.max(gl, axis=0, keepdims=True)
    g_p = 1.0 / jnp.sum(jnp.exp(gl - gmax), axis=0, keepdims=True)
    gidx = jnp.min(jnp.where(gl == gmax, row, big), axis=0, keepdims=True)
    esel = jnp.zeros((SUBLANES, tm), f32)
    for g in range(N_EXPERT_GROUPS):
        esel = jnp.where(gidx == float(g), lg[SUBLANES * (g + 1):SUBLANES * (g + 2)], esel)
    e1 = jnp.max(esel, axis=0, keepdims=True)
    i1 = jnp.min(jnp.where(esel == e1, row, big), axis=0, keepdims=True)
    esel2 = jnp.where(row == i1, -jnp.inf, esel)
    e2 = jnp.max(esel2, axis=0, keepdims=True)
    i2 = jnp.min(jnp.where(esel2 == e2, row, big), axis=0, keepdims=True)
    t = jnp.exp(e2 - e1)
    w1 = g_p / (1.0 + t)
    ids = jnp.concatenate([gidx * EXPERTS_PER_GROUP + i1, gidx * EXPERTS_PER_GROUP + i2], axis=0).astype(i32)
    ids_ref[0] = ids
    gates_ref[0] = jnp.concatenate([w1, w1 * t], axis=0)
    e_iota = lax.broadcasted_iota(i32, (N_EXPERTS, tm), 0)
    onehot = jnp.where((e_iota == ids[0:1]) | (e_iota == ids[1:2]), 1.0, 0.0).astype(bf16)
    cnt_ref[0] = jnp.dot(onehot, jnp.ones((tm, LANES), bf16), preferred_element_type=f32)


def _out_route(oa, ob, x, gate1, shift2, scale2, w_a, w_b, g2, wr_t, br_b, *, tm):
    n, t, d = x.shape
    w_half = oa.shape[2]
    mod_rows = gate1.shape[1]
    tpn = t // tm

    def mod_spec():
        if mod_rows == 1:
            return pl.BlockSpec((1, 1, d), lambda b, i: (b, 0, 0))
        return pl.BlockSpec((1, tm, d), lambda b, i: (b, i, 0))

    full = lambda shp: pl.BlockSpec(shp, lambda b, i: (0,) * len(shp))
    tok = lambda w: pl.BlockSpec((1, tm, w), lambda b, i: (b, i, 0))
    return pl.pallas_call(
        _out_route_kernel,
        grid=(n, tpn),
        in_specs=[tok(w_half), tok(w_half), tok(d), mod_spec(), mod_spec(), mod_spec(),
                  full((w_half, d)), full((w_half, d)), full((1, d)),
                  full((ROUTER_ROWS, d)), full((ROUTER_ROWS, LANES))],
        out_specs=[tok(d), tok(d),
                   pl.BlockSpec((1, 2, tm), lambda b, i: (b, 0, i)),
                   pl.BlockSpec((1, 2, tm), lambda b, i: (b, 0, i)),
                   pl.BlockSpec((1, N_EXPERTS, LANES), lambda b, i: (b * tpn + i, 0, 0))],
        out_shape=[SDS((n, t, d), f32), SDS((n, t, d), bf16), SDS((n, 2, t), i32), SDS((n, 2, t), f32),
                   SDS((n * tpn, N_EXPERTS, LANES), f32)],
        compiler_params=_cparams(("arbitrary", "arbitrary")),
        name="out_route",
    )(oa, ob, x, gate1, shift2, scale2, w_a, w_b, g2, wr_t, br_b)


def _route_plan(cnt, nb_max):
    pc = _round_up(cnt, RUN_ALIGN)
    toff = jnp.cumsum(pc, axis=1) - pc
    tot = jnp.sum(pc, axis=0)
    seg = _round_up(tot, EXPERT_BLOCK)
    seg_end = jnp.cumsum(seg)
    seg_start = seg_end - seg
    run_start = seg_start[None, :] + jnp.cumsum(pc, axis=0) - pc
    nb = seg_end[-1] // EXPERT_BLOCK
    blk = jnp.arange(nb_max, dtype=i32)
    be = jnp.sum(((seg_end // EXPERT_BLOCK)[None, :] <= blk[:, None]).astype(i32), axis=1)
    be = jnp.minimum(be, N_EXPERTS - 1)
    be = jnp.where(blk < nb, be, be[jnp.maximum(nb - 1, 0)])
    return dict(
        pc8=(pc // RUN_ALIGN).reshape(-1).astype(i32), toff=toff.reshape(-1).astype(i32),
        run_start=run_start.reshape(-1).astype(i32), ntot8=(jnp.sum(pc, axis=1) // RUN_ALIGN).astype(i32),
        tail_start=(seg_start + tot).astype(i32), tail8=((seg - tot) // RUN_ALIGN).astype(i32),
        tail_tot8=(jnp.sum(seg - tot) // RUN_ALIGN).reshape(1).astype(i32),
        block_e=be, nb=nb.reshape(1).astype(i32),
        toff_v=jnp.broadcast_to(toff.astype(f32)[:, :, None], toff.shape + (LANES,)))


def _sorted_positions(ids_ref, toff_ref, upper_ref, tm):
    ids = ids_ref[0]
    idc = jnp.concatenate([ids[0:1], ids[1:2]], axis=1)
    e_iota = lax.broadcasted_iota(i32, (N_EXPERTS, 2 * tm), 0)
    onehot = jnp.where(e_iota == idc, 1.0, 0.0)
    before = jnp.dot(onehot.astype(bf16), upper_ref[...], preferred_element_type=f32)
    return jnp.sum(onehot * (before + toff_ref[0][:, 0:1]), axis=0, keepdims=True)


def _fill_upper(upper_ref):
    n = upper_ref.shape[0]
    a_src = lax.broadcasted_iota(i32, (n, n), 0)
    a_dst = lax.broadcasted_iota(i32, (n, n), 1)
    upper_ref[...] = jnp.where(a_src < a_dst, 1.0, 0.0).astype(bf16)


def _aligned(row):
    return row if isinstance(row, int) else pl.multiple_of(row, RUN_ALIGN)


def _run_chunk_copy(vmem_buf, hbm_buf, sem, vrow, hrow, to_hbm):
    v = vmem_buf.at[pl.ds(_aligned(vrow), RUN_ALIGN)]
    h = hbm_buf.at[pl.ds(_aligned(hrow), RUN_ALIGN)]
    return pltpu.make_async_copy(v, h, sem) if to_hbm else pltpu.make_async_copy(h, v, sem)


def _move_runs(i, pc8_ref, toff_s_ref, rstart_ref, ntot8_ref, vmem_buf, hbm_buf, sem, to_hbm):
    def per_expert(e, c):
        idx = i * N_EXPERTS + e
        vrow0 = toff_s_ref[idx]
        hrow0 = rstart_ref[idx]

        def per_chunk(k, cc):
            _run_chunk_copy(vmem_buf, hbm_buf, sem, vrow0 + k * RUN_ALIGN, hrow0 + k * RUN_ALIGN, to_hbm).start()
            return cc

        lax.fori_loop(0, pc8_ref[idx], per_chunk, 0)
        return c

    lax.fori_loop(0, N_EXPERTS, per_expert, 0)

    def wait_one(k, c):
        _run_chunk_copy(vmem_buf, hbm_buf, sem, 0, 0, to_hbm).wait()
        return c

    lax.fori_loop(0, ntot8_ref[i], wait_one, 0)


def _dispatch_kernel(pc8_ref, toff_s_ref, rstart_ref, ntot8_ref, tstart_ref, tail8_ref, ttot_ref, nb_ref,
                     h_ref, ids_ref, toff_ref, xs_ref, upper_sc, xbuf_sc, zero_sc, sem):
    i = pl.program_id(0)
    tm = h_ref.shape[0]
    rt = xbuf_sc.shape[0]

    @pl.when(i == 0)
    def _():
        _fill_upper(upper_sc)
        zero_sc[...] = jnp.zeros(zero_sc.shape, f32)

    pos = _sorted_positions(ids_ref, toff_ref, upper_sc, tm)
    r_iota = lax.broadcasted_iota(i32, (rt, tm), 0).astype(f32)
    perm = jnp.where((r_iota == pos[:, :tm]) | (r_iota == pos[:, tm:]), 1.0, 0.0).astype(bf16)
    xbuf_sc[...] = jnp.dot(perm, h_ref[...], preferred_element_type=f32)
    _move_runs(i, pc8_ref, toff_s_ref, rstart_ref, ntot8_ref, xbuf_sc, xs_ref, sem.at[0], True)

    @pl.when(i == pl.num_programs(0) - 1)
    def _():
        def per_expert(e, c):
            def per_chunk(k, cc):
                _run_chunk_copy(zero_sc, xs_ref, sem.at[0], 0, tstart_ref[e] + k * RUN_ALIGN, True).start()
                return cc
            lax.fori_loop(0, tail8_ref[e], per_chunk, 0)
            return c
        lax.fori_loop(0, N_EXPERTS, per_expert, 0)

        def wait_one(k, c):
            _run_chunk_copy(zero_sc, xs_ref, sem.at[0], 0, 0, True).wait()
            return c
        lax.fori_loop(0, ttot_ref[0], wait_one, 0)

        def dead_block(row):
            return pltpu.make_async_copy(zero_sc, xs_ref.at[pl.ds(row, EXPERT_BLOCK)], sem.at[1])

        n_dead = xs_ref.shape[0] // EXPERT_BLOCK - nb_ref[0]

        def start_dead(k, c):
            dead_block(pl.multiple_of((nb_ref[0] + k) * EXPERT_BLOCK, EXPERT_BLOCK)).start()
            return c
        lax.fori_loop(0, n_dead, start_dead, 0)

        def wait_dead(k, c):
            dead_block(0).wait()
            return c
        lax.fori_loop(0, n_dead, wait_dead, 0)


def _sorted_rows(tm):
    return _round_up(2 * tm + N_EXPERTS * (RUN_ALIGN - 1), LANES)


def _dispatch(plan, h2_flat, ids, toff_v, rows_total, *, tm):
    tokens, d = h2_flat.shape
    n, _, t = ids.shape
    tpn = t // tm
    rt = _sorted_rows(tm)
    grid_spec = pltpu.PrefetchScalarGridSpec(
        num_scalar_prefetch=8,
        grid=(tokens // tm,),
        in_specs=[pl.BlockSpec((tm, d), lambda i, *_: (i, 0)),
                  pl.BlockSpec((1, 2, tm), lambda i, *_: (i // tpn, 0, i % tpn)),
                  pl.BlockSpec((1, N_EXPERTS, LANES), lambda i, *_: (i, 0, 0))],
        out_specs=pl.BlockSpec(memory_space=pl.ANY),
        scratch_shapes=[pltpu.VMEM((2 * tm, 2 * tm), bf16), pltpu.VMEM((rt, d), f32),
                        pltpu.VMEM((EXPERT_BLOCK, d), f32), pltpu.SemaphoreType.DMA((2,))],
    )
    return pl.pallas_call(
        _dispatch_kernel,
        grid_spec=grid_spec,
        out_shape=SDS((rows_total, d), f32),
        compiler_params=_cparams(("arbitrary",)),
        name="dispatch",
    )(plan["pc8"], plan["toff"], plan["run_start"], plan["ntot8"], plan["tail_start"], plan["tail8"],
      plan["tail_tot8"], plan["nb"], h2_flat, ids, toff_v)


def _experts_kernel(be_ref, nb_ref, x_ref, wg_ref, wu_ref, wd_ref, y_ref, wg_sc, wu_sc, wd_sc):
    j = pl.program_id(0)
    live = j < nb_ref[0]

    @pl.when(live)
    def _():
        prev = be_ref[jnp.maximum(j - 1, 0)]

        @pl.when((j == 0) | (be_ref[j] != prev))
        def _():
            wg_sc[...] = wg_ref[0].astype(bf16)
            wu_sc[...] = wu_ref[0].astype(bf16)
            wd_sc[...] = wd_ref[0].astype(bf16)

        xb = x_ref[...].astype(bf16)
        a = jnp.dot(xb, wg_sc[...], preferred_element_type=f32)
        b = jnp.dot(xb, wu_sc[...], preferred_element_type=f32)
        hid = (jax.nn.silu(a) * b).astype(bf16)
        y_ref[...] = jnp.dot(hid, wd_sc[...], preferred_element_type=f32)

    @pl.when(jnp.logical_not(live))
    def _():
        y_ref[...] = jnp.zeros(y_ref.shape, f32)


def _experts(plan, xs, w_gate, w_up, w_down, nb_max):
    d = xs.shape[1]
    de = w_gate.shape[2]
    eb = EXPERT_BLOCK
    grid_spec = pltpu.PrefetchScalarGridSpec(
        num_scalar_prefetch=2,
        grid=(nb_max,),
        in_specs=[pl.BlockSpec((eb, d), lambda j, be, nb: (jnp.minimum(j, nb[0] - 1), 0)),
                  pl.BlockSpec((1, d, de), lambda j, be, nb: (be[j], 0, 0)),
                  pl.BlockSpec((1, d, de), lambda j, be, nb: (be[j], 0, 0)),
                  pl.BlockSpec((1, de, d), lambda j, be, nb: (be[j], 0, 0))],
        out_specs=pl.BlockSpec((eb, d), lambda j, be, nb: (j, 0)),
        scratch_shapes=[pltpu.VMEM((d, de), bf16), pltpu.VMEM((d, de), bf16), pltpu.VMEM((de, d), bf16)],
    )
    return pl.pallas_call(
        _experts_kernel,
        grid_spec=grid_spec,
        out_shape=SDS((nb_max * eb, d), f32),
        compiler_params=_cparams(("arbitrary",)),
        name="experts",
    )(plan["block_e"], plan["nb"], xs, w_gate, w_up, w_down)


def _combine_kernel(pc8_ref, toff_s_ref, rstart_ref, ntot8_ref,
                    x1_ref, g2_ref, gf_ref, ids_ref, gates_ref, toff_ref, y_ref,
                    out_ref, upper_sc, ybuf_sc, sem, *, final):
    i = pl.program_id(0)
    tm = x1_ref.shape[0]
    rt = ybuf_sc.shape[0]

    @pl.when(i == 0)
    def _():
        _fill_upper(upper_sc)
        ybuf_sc[...] = jnp.zeros(ybuf_sc.shape, f32)

    _move_runs(i, pc8_ref, toff_s_ref, rstart_ref, ntot8_ref, ybuf_sc, y_ref, sem.at[0], False)
    pos = _sorted_positions(ids_ref, toff_ref, upper_sc, tm)
    gates = gates_ref[0]
    r_iota = lax.broadcasted_iota(i32, (rt, tm), 0).astype(f32)
    weights = (jnp.where(r_iota == pos[:, :tm], gates[0:1], 0.0)
               + jnp.where(r_iota == pos[:, tm:], gates[1:2], 0.0)).astype(bf16)
    ff = lax.dot_general(weights, ybuf_sc[...].astype(bf16), (((0,), (0,)), ((), ())),
                         preferred_element_type=f32)
    x2 = x1_ref[...] + g2_ref[0] * ff
    out_ref[...] = _rms(x2, gf_ref[...]) if final else x2


def _combine(plan, x1_flat, gate2, g_final, ids, gates, toff_v, y, *, tm, final):
    tokens, d = x1_flat.shape
    n, _, t = ids.shape
    tpn = t // tm
    rt = _sorted_rows(tm)
    mod_rows = gate2.shape[1]
    if mod_rows == 1:
        mod_spec = pl.BlockSpec((1, 1, d), lambda i, *_: (i // tpn, 0, 0))
    else:
        mod_spec = pl.BlockSpec((1, tm, d), lambda i, *_: (i // tpn, i % tpn, 0))
    slot_spec = pl.BlockSpec((1, 2, tm), lambda i, *_: (i // tpn, 0, i % tpn))
    tok_spec = pl.BlockSpec((tm, d), lambda i, *_: (i, 0))
    grid_spec = pltpu.PrefetchScalarGridSpec(
        num_scalar_prefetch=4,
        grid=(tokens // tm,),
        in_specs=[tok_spec, mod_spec, pl.BlockSpec((1, d), lambda i, *_: (0, 0)), slot_spec, slot_spec,
                  pl.BlockSpec((1, N_EXPERTS, LANES), lambda i, *_: (i, 0, 0)),
                  pl.BlockSpec(memory_space=pl.ANY)],
        out_specs=tok_spec,
        scratch_shapes=[pltpu.VMEM((2 * tm, 2 * tm), bf16), pltpu.VMEM((rt, d), f32),
                        pltpu.SemaphoreType.DMA((1,))],
    )
    return pl.pallas_call(
        functools.partial(_combine_kernel, final=final),
        grid_spec=grid_spec,
        out_shape=SDS((tokens, d), f32),
        compiler_params=_cparams(("arbitrary",)),
        name="combine",
    )(plan["pc8"], plan["toff"], plan["run_start"], plan["ntot8"],
      x1_flat, gate2, g_final, ids, gates, toff_v, y)


def _moe(x1, h2, ids, gates, cnt, gate2, g_final, w_gate, w_up, w_down, *, tm, final):
    n, t, d = x1.shape
    tiles = n * t // tm
    rows_max = 2 * n * t + tiles * N_EXPERTS * (RUN_ALIGN - 1) + N_EXPERTS * (EXPERT_BLOCK - 1)
    nb_max = -(-rows_max // EXPERT_BLOCK)
    plan = _route_plan(cnt[:, :, 0].astype(i32), nb_max)
    xs = _dispatch(plan, h2.reshape(n * t, d), ids, plan["toff_v"], nb_max * EXPERT_BLOCK, tm=tm)
    y = _experts(plan, xs, w_gate, w_up, w_down, nb_max)
    out = _combine(plan, x1.reshape(n * t, d), gate2, g_final, ids, gates, plan["toff_v"], y, tm=tm, final=final)
    return out.reshape(n, t, d)


def _layer_group(x, mods, tables, attend, p, g_final, *, tm, mix, emit_vn, final):
    ws_t, mask, bs_b = mix
    outs = _in_proj(x, mods[0], mods[1], p["g1"], p["w_in"], tables, ws_t, mask, bs_b, p["g_v"], p["g_mlp"],
                    tm=tm, emit_vn=emit_vn)
    q, k, v, kb, vb, ob = outs[:6]
    oa = attend(q, k, v, kb, vb)
    x1, h2, ids, gates, cnt = _out_route(oa, ob, x, mods[2], mods[3], mods[4], p["w_out_a"], p["w_out_b"],
                                         p["g2"], p["wr_t"], p["br_b"], tm=tm)
    x2 = _moe(x1, h2, ids, gates, cnt, mods[5], g_final, p["w_gate"], p["w_up"], p["w_down"], tm=tm, final=final)
    return x2, k, v, (outs[6] if emit_vn else None)


def kernel(x_prompt, x_sample, cache_k, cache_v, page_table, c_prompt, c_sample, w_ada, b_ada, g_norm1, w_in,
           lam_p, g_attn, g_v, w_s, b_s, g_mlp, w_out, g_norm2, w_router_g, b_router_g, w_router_e, b_router_e,
           w_exp_gate, w_exp_up, w_exp_down, g_final):
    nb, seq, d = x_prompt.shape
    ndec, dseq, _ = x_sample.shape
    depth = w_ada.shape[0]
    past_len = page_table.shape[1] * PAGE
    n_pool = cache_k.shape[1]
    a_w = N_HEADS * DV
    tm_s = ndec * dseq

    tables_p, tables_s8 = _rope_tables(seq, dseq, past_len)
    tables_s = tuple(jnp.tile(t, (ndec, 1)) for t in tables_s8)
    c_all = jnp.concatenate([c_prompt, c_sample], axis=0)
    c_pad = _round_up(c_all.shape[0], SUBLANES) - c_all.shape[0]
    c_all = jnp.pad(c_all, ((0, c_pad), (0, 0)))

    tri = jnp.tril(jnp.ones((CHUNK, CHUNK), f32))
    idx = jnp.arange(tm_s)
    mask_s = ((idx[:, None] // dseq == idx[None, :] // dseq) & (idx[None, :] % dseq <= idx[:, None] % dseq)).astype(f32)

    xp, xs = x_prompt, x_sample.reshape(1, tm_s, d)
    kp_l, vp_l, ks_l, vs_l, cv_l = [], [], [], [], []
    for l in range(depth):
        lam_init = 0.8 - 0.6 * math.exp(-0.3 * l)
        mod = _adaln(c_all, w_ada[l], b_ada[l])
        mods_p = [mod[:nb, None, j * d:(j + 1) * d] for j in range(6)]
        mods_s = [jnp.repeat(mod[nb:nb + ndec, j * d:(j + 1) * d], dseq, axis=0)[None] for j in range(6)]
        wr_t = jnp.concatenate([w_router_g[l].T, jnp.zeros((SUBLANES - N_EXPERT_GROUPS, d), f32),
                                w_router_e[l].T], axis=0)
        br = jnp.concatenate([b_router_g[l], jnp.zeros((SUBLANES - N_EXPERT_GROUPS,), f32), b_router_e[l]])
        p = dict(
            g1=g_norm1[l].reshape(1, d), w_in=w_in[l].astype(bf16), g_v=g_v[l].reshape(1, -1),
            g_mlp=g_mlp[l].reshape(1, -1), w_out_a=w_out[l][:a_w].astype(bf16), w_out_b=w_out[l][a_w:].astype(bf16),
            g2=g_norm2[l].reshape(1, d), wr_t=wr_t, br_b=jnp.broadcast_to(br[:, None], (ROUTER_ROWS, LANES)),
            w_gate=w_exp_gate[l], w_up=w_exp_up[l], w_down=w_exp_down[l])
        gf = g_final.reshape(1, d)
        mix_p = (w_s[l], tri, jnp.broadcast_to(b_s[l][:, :, None], (N_GROUPS_B, CHUNK, C_B)))
        mix_s = (jnp.tile(w_s[l][:, :dseq, :dseq], (1, ndec, ndec)), mask_s,
                 jnp.broadcast_to(jnp.tile(b_s[l][:, :dseq], (1, ndec))[:, :, None], (N_GROUPS_B, tm_s, C_B)))

        attend_p = lambda q, k, v, kb, vb: _attn_prompt(lam_p[l], g_attn[l], q, kb, vb, lam_init)
        ck = jnp.transpose(cache_k[l], (0, 2, 3, 4, 1)).reshape(n_pool, -1, PAGE)
        cv = cache_v[l].reshape(n_pool, PAGE * N_HEADS, DV)

        def attend_s(q, k, v, kb, vb):
            o = _attn_sample(page_table, lam_p[l], g_attn[l], q.reshape(ndec, dseq, -1), k.reshape(ndec, dseq, -1),
                             v.reshape(ndec, dseq, -1), ck, cv, lam_init)
            return o.reshape(1, tm_s, -1)

        final = l == depth - 1
        xp, k_p, v_p, _ = _layer_group(xp, mods_p, tables_p, attend_p, p, gf, tm=TOKEN_TILE, mix=mix_p,
                                       emit_vn=False, final=final)
        xs, k_s, v_s, cv_s = _layer_group(xs, mods_s, tables_s, attend_s, p, gf, tm=tm_s, mix=mix_s,
                                          emit_vn=True, final=final)
        kp_l.append(k_p.reshape(nb, seq, N_HEADS, 2, DK))
        vp_l.append(v_p.reshape(nb, seq, N_HEADS, DV))
        ks_l.append(k_s.reshape(ndec, dseq, N_HEADS, 2, DK))
        vs_l.append(v_s.reshape(ndec, dseq, N_HEADS, DV))
        cv_l.append(cv_s.reshape(ndec, dseq, -1))
    return (xp, xs.reshape(ndec, dseq, d), jnp.stack(kp_l), jnp.stack(vp_l), jnp.stack(ks_l), jnp.stack(vs_l),
            jnp.stack(cv_l))
```

```python
import functools
import math

import jax
import jax.numpy as jnp
from jax import lax
from jax.experimental import pallas as pl
from jax.experimental.pallas import tpu as pltpu

f32 = jnp.float32
bf16 = jnp.bfloat16
i32 = jnp.int32
SDS = jax.ShapeDtypeStruct

N_HEADS = 4
DK = 64
DV = 2 * DK
ROT = DK // 4
ROPE_THETA = 500000.0
N_GROUPS_B = 4
C_B = 128
CHUNK = 128
N_EXPERT_GROUPS = 4
EXPERTS_PER_GROUP = 8
N_EXPERTS = N_EXPERT_GROUPS * EXPERTS_PER_GROUP
PAGE = 128
EPS = 1e-6
NEG = -1e30
LOG2E = math.log2(math.e)

LANES = 128
SUBLANES = 8
VMEM_LIMIT = 56 * 1024 * 1024

TOKEN_TILE = 512
ATTN_TQ = 256
ATTN_TK = 256
ONES_ROWS = 16
PAGES_PER_STEP = 16
EXPERT_BLOCK = 256
RUN_ALIGN = 8
CHUNK_ROWS = (32, 16, 8)
N_LIST = 3 * len(CHUNK_ROWS)
ROUTER_ROWS = 40


def _cparams(sem=None):
    return pltpu.CompilerParams(dimension_semantics=sem, vmem_limit_bytes=VMEM_LIMIT)


def _round_up(x, m):
    return (x + m - 1) // m * m


def _adaln_kernel(c_ref, w_ref, b_ref, o_ref):
    a = jax.nn.silu(c_ref[...])
    o_ref[...] = jnp.dot(a, w_ref[...], preferred_element_type=f32,
                         precision=lax.Precision.HIGHEST) + b_ref[...]


def _adaln(c_all, w_ada, b_ada):
    n, d = c_all.shape
    m = w_ada.shape[1]
    tn = 1536
    return pl.pallas_call(
        _adaln_kernel,
        grid=(m // tn,),
        in_specs=[pl.BlockSpec((n, d), lambda j: (0, 0)),
                  pl.BlockSpec((d, tn), lambda j: (0, j)),
                  pl.BlockSpec((1, tn), lambda j: (0, j))],
        out_specs=pl.BlockSpec((n, tn), lambda j: (0, j)),
        out_shape=SDS((n, m), f32),
        compiler_params=_cparams(("arbitrary",)),
        name="adaln",
    )(c_all, w_ada, b_ada.reshape(1, m))


def _rope_kernel(inv_ref, cos_ref, sin_ref, *, rows_prompt, past_len):
    shape = cos_ref.shape
    r = lax.broadcasted_iota(i32, shape, 0)
    l = lax.broadcasted_iota(i32, shape, 1)
    base = jnp.where(r < rows_prompt, r * 16, past_len + (r - rows_prompt) * 16)
    pos = base + (l >> 3)
    ang = pos.astype(f32) * inv_ref[...]
    cos_ref[...] = jnp.cos(ang)
    sin_ref[...] = jnp.sin(ang)


def _rope_tables(seq, dec_seq, past_len):
    inv = ROPE_THETA ** (-jnp.arange(0, ROT, 2, dtype=f32) / ROT)
    inv_lane = jnp.tile(inv, LANES // (ROT // 2)).reshape(1, LANES)
    rp = seq // 16
    rt = rp + SUBLANES
    cos_c, sin_c = pl.pallas_call(
        functools.partial(_rope_kernel, rows_prompt=rp, past_len=past_len),
        out_shape=(SDS((rt, LANES), f32), SDS((rt, LANES), f32)),
        name="rope_table",
    )(inv_lane)

    def expand(c8, s8):
        n = c8.shape[0]
        z8 = jnp.zeros((n, ROT // 2), f32)
        rest = DK - ROT
        cos_t = jnp.concatenate([c8, c8, jnp.ones((n, rest), f32)], axis=1)
        sin_a = jnp.concatenate([-s8, z8, jnp.zeros((n, rest), f32)], axis=1)
        sin_b = jnp.concatenate([z8, s8, jnp.zeros((n, rest), f32)], axis=1)
        return tuple(jnp.tile(t, (1, LANES // DK)) for t in (cos_t, sin_a, sin_b))

    half = ROT // 2
    prompt = expand(cos_c[:rp].reshape(seq, half), sin_c[:rp].reshape(seq, half))
    sample = expand(cos_c[rp].reshape(16, half)[:dec_seq], sin_c[rp].reshape(16, half)[:dec_seq])
    return prompt, sample


def _rms(x, g):
    return x * lax.rsqrt(jnp.mean(x * x, axis=-1, keepdims=True) + EPS) * g


def _in_proj_kernel(x_ref, sh_ref, sc_ref, g1_ref, w_ref, cos_ref, sa_ref, sb_ref,
                    ws_ref, msk_ref, bs_ref, gv_ref, gm_ref,
                    q_ref, k_ref, v_ref, kb_ref, vb_ref, ob_ref, *vn_refs, mix_rows):
    x = x_ref[0]
    tm = x.shape[0]
    h = _rms(x, g1_ref[...]) * (1.0 + sc_ref[0]) + sh_ref[0]
    z = jnp.dot(h.astype(bf16), w_ref[...], preferred_element_type=f32)
    qk_w = N_HEADS * 2 * DK
    a_w = N_HEADS * DV
    b_w = N_GROUPS_B * C_B
    cos_t, sin_a, sin_b = cos_ref[...], sa_ref[...], sb_ref[...]
    for s in range(2 * qk_w // LANES):
        zs = z[:, s * LANES:(s + 1) * LANES]
        rot = zs * cos_t + pltpu.roll(zs, LANES - ROT // 2, 1) * sin_a + pltpu.roll(zs, ROT // 2, 1) * sin_b
        if s < qk_w // LANES:
            q_ref[0, :, s * LANES:(s + 1) * LANES] = (rot * (DK ** -0.5 * LOG2E)).astype(bf16)
        else:
            k_ref[0, :, s * LANES - qk_w:(s + 1) * LANES - qk_w] = rot
            kb_ref[0, :, s * LANES - qk_w:(s + 1) * LANES - qk_w] = rot.astype(bf16)
    v = z[:, 2 * qk_w:2 * qk_w + a_w]
    v_ref[0] = v
    vb_ref[0] = v.astype(bf16)
    uv = jax.nn.gelu(z[:, 2 * qk_w + a_w:])
    nblk = tm // mix_rows
    for g in range(N_GROUPS_B):
        u = uv[:, g * C_B:(g + 1) * C_B]
        vn = _rms(uv[:, b_w + g * C_B:b_w + (g + 1) * C_B], gv_ref[:, g * C_B:(g + 1) * C_B])
        if vn_refs:
            vn_refs[0][0, :, g * C_B:(g + 1) * C_B] = vn
        wm = (ws_ref[g] * msk_ref[...]).astype(bf16)
        vcat = jnp.concatenate([vn[j * mix_rows:(j + 1) * mix_rows] for j in range(nblk)], axis=1)
        mixed = jnp.dot(wm, vcat.astype(bf16), preferred_element_type=f32)
        for j in range(nblk):
            t = u[j * mix_rows:(j + 1) * mix_rows] * (mixed[:, j * C_B:(j + 1) * C_B] + bs_ref[g])
            ob_ref[0, j * mix_rows:(j + 1) * mix_rows, g * C_B:(g + 1) * C_B] = _rms(
                t, gm_ref[:, g * C_B:(g + 1) * C_B]).astype(bf16)


def _in_proj(x, shift, scale, g1, w_in_bf, tables, ws_t, mask, bs_b, g_v, g_mlp, *, tm, emit_vn):
    n, t, d = x.shape
    in_w = w_in_bf.shape[1]
    r = ws_t.shape[1]
    mod_rows = shift.shape[1]
    tab_rows = tables[0].shape[0]
    w_half = N_HEADS * DV

    def mod_spec():
        if mod_rows == 1:
            return pl.BlockSpec((1, 1, d), lambda b, i: (b, 0, 0))
        return pl.BlockSpec((1, tm, d), lambda b, i: (b, i, 0))

    def tab_spec():
        if tab_rows == tm:
            return pl.BlockSpec((tm, LANES), lambda b, i: (0, 0))
        return pl.BlockSpec((tm, LANES), lambda b, i: (i, 0))

    full = lambda shp: pl.BlockSpec(shp, lambda b, i: (0,) * len(shp))
    tok = lambda w: pl.BlockSpec((1, tm, w), lambda b, i: (b, i, 0))
    out_shape = [SDS((n, t, w_half), bf16), SDS((n, t, w_half), f32), SDS((n, t, w_half), f32),
                 SDS((n, t, w_half), bf16), SDS((n, t, w_half), bf16), SDS((n, t, w_half), bf16)]
    out_specs = [tok(w_half)] * 6
    if emit_vn:
        out_shape.append(SDS((n, t, w_half), f32))
        out_specs.append(tok(w_half))
    return pl.pallas_call(
        functools.partial(_in_proj_kernel, mix_rows=r),
        grid=(n, t // tm),
        in_specs=[tok(d), mod_spec(), mod_spec(), full((1, d)), full((d, in_w)),
                  tab_spec(), tab_spec(), tab_spec(),
                  full((N_GROUPS_B, r, r)), full((r, r)), full((N_GROUPS_B, r, C_B)),
                  full((1, w_half)), full((1, w_half))],
        out_specs=out_specs,
        out_shape=out_shape,
        compiler_params=_cparams(("arbitrary", "arbitrary")),
        name="in_proj",
    )(x, shift, scale, g1, w_in_bf, *tables, ws_t, mask, bs_b, g_v, g_mlp)


def _diff_lambda(lam_ref, lam_init):
    lp = lam_ref[...]
    s1 = jnp.sum(lp[0:1] * lp[1:2], axis=1, keepdims=True)
    s2 = jnp.sum(lp[2:3] * lp[3:4], axis=1, keepdims=True)
    return jnp.exp(s1) - jnp.exp(s2) + lam_init


def _attn_prompt_kernel(lam_ref, g_ref, q_ref, k_ref, v_ref, o_ref, vt_sc, qq_sc, acc_sc, m_sc, s_sc, *, lam_init):
    seq = q_ref.shape[1]
    tq, tk = ATTN_TQ, ATTN_TK
    lam = _diff_lambda(lam_ref, lam_init)
    heads = range(N_HEADS)
    nt = (((1,), (1,)), ((), ()))

    def prep(j, c):
        sl = pl.ds(pl.multiple_of(j * tk, tk), tk)
        for h in heads:
            vt_sc[h, j, :DV] = v_ref[0, sl, h * DV:(h + 1) * DV].T
            vt_sc[h, j, DV:] = jnp.ones((ONES_ROWS, tk), bf16)
        return c

    lax.fori_loop(0, seq // tk, prep, 0)
    lane = lax.broadcasted_iota(i32, (tq, DV), 1)

    def q_block(qi, c):
        qsl = pl.ds(pl.multiple_of(qi * tq, tq), tq)
        for h in heads:
            qb = q_ref[0, qsl, h * DV:(h + 1) * DV]
            zero = jnp.zeros_like(qb)
            qq_sc[h, :tq] = jnp.where(lane < DK, qb, zero)
            qq_sc[h, tq:] = jnp.where(lane >= DK, qb, zero)
            m_sc[h] = jnp.full(m_sc.shape[1:], NEG, f32)
            acc_sc[h] = jnp.zeros(acc_sc.shape[1:], f32)

        def scores(j, h):
            ksl = pl.ds(pl.multiple_of(j * tk, tk), tk)
            return lax.dot_general(k_ref[0, ksl, h * DV:(h + 1) * DV], qq_sc[h], nt, preferred_element_type=f32)

        s_sc[...] = scores(0, 0)

        def kv_block(j, masked):
            s_next = None
            for h in heads:
                s = s_sc[...] if h == 0 else s_next
                if h + 1 < N_HEADS:
                    s_next = scores(j, h + 1)
                elif not masked:
                    s_sc[...] = scores(j + 1, 0)
                if masked:
                    key_i = lax.broadcasted_iota(i32, s.shape, 0)
                    qry_i = lax.broadcasted_iota(i32, s.shape, 1) & (tq - 1)
                    s = jnp.where(key_i <= qry_i, s, NEG)
                m_prev = m_sc[h]
                m_new = jnp.maximum(m_prev, jnp.max(s, axis=0, keepdims=True))
                alpha = jnp.exp2(m_prev - m_new)
                p = jnp.exp2(s - m_new)
                pv = jnp.dot(vt_sc[h, j], p.astype(bf16), preferred_element_type=f32)
                acc_sc[h] = acc_sc[h] * alpha + pv
                m_sc[h] = m_new

        def full_block(j, cc):
            kv_block(j, False)
            return cc

        def full_pair(jj, cc):
            kv_block(2 * jj, False)
            kv_block(2 * jj + 1, False)
            return cc

        pairs = qi >> 1
        lax.fori_loop(0, pairs, full_pair, 0)
        lax.fori_loop(2 * pairs, qi, full_block, 0)
        kv_block(qi, True)
        for h in heads:
            acc = acc_sc[h]
            inv_l = 1.0 / acc[DV:DV + 1, :]
            o_t = acc[:DV, :tq] * inv_l[:, :tq] - lam * (acc[:DV, tq:] * inv_l[:, tq:])
            o_ref[0, qsl, h * DV:(h + 1) * DV] = _rms(o_t.T, g_ref[h:h + 1, :] * (1.0 - lam_init)).astype(bf16)
        return c

    lax.fori_loop(0, seq // tq, q_block, 0)


def _attn_prompt(lam_p, g_attn, q, k, v, lam_init):
    n, seq, w = q.shape
    blk = pl.BlockSpec((1, seq, w), lambda b: (b, 0, 0))
    return pl.pallas_call(
        functools.partial(_attn_prompt_kernel, lam_init=lam_init),
        grid=(n,),
        in_specs=[pl.BlockSpec(lam_p.shape, lambda b: (0, 0)),
                  pl.BlockSpec(g_attn.shape, lambda b: (0, 0)),
                  blk, blk, blk],
        out_specs=blk,
        out_shape=SDS((n, seq, w), bf16),
        scratch_shapes=[pltpu.VMEM((N_HEADS, seq // ATTN_TK, DV + ONES_ROWS, ATTN_TK), bf16),
                        pltpu.VMEM((N_HEADS, 2 * ATTN_TQ, DV), bf16),
                        pltpu.VMEM((N_HEADS, DV + ONES_ROWS, 2 * ATTN_TQ), f32),
                        pltpu.VMEM((N_HEADS, 1, 2 * ATTN_TQ), f32),
                        pltpu.VMEM((ATTN_TK, 2 * ATTN_TQ), f32)],
        compiler_params=_cparams(("arbitrary",)),
        name="attn_prompt",
    )(lam_p, g_attn, q, k, v)


def _attn_sample_kernel(pt_ref, lam_ref, g_ref, q_ref, kn_ref, vn_ref, *rest, lam_init, n_pages):
    k_pages = rest[:n_pages]
    v_pages = rest[n_pages:2 * n_pages]
    o_ref, qbd_sc, m_sc, l_sc, acc_sc = rest[2 * n_pages:]
    step = pl.program_id(1)
    ds = q_ref.shape[1]
    rows_h = 2 * ds
    rows = N_HEADS * rows_h
    width = q_ref.shape[2]
    heads = range(N_HEADS)
    row_i = lax.broadcasted_iota(i32, (rows, width), 0)
    col_i = lax.broadcasted_iota(i32, (rows, width), 1)

    @pl.when(step == 0)
    def _():
        qt = jnp.concatenate([q_ref[0].astype(f32)] * (N_HEADS * 2), axis=0)
        same_map = (col_i >> (DK.bit_length() - 1)) == (row_i >> (ds.bit_length() - 1))
        qbd_sc[...] = jnp.where(same_map, qt, 0.0).astype(bf16)
        m_sc[...] = jnp.full(m_sc.shape, NEG, f32)
        l_sc[...] = jnp.zeros(l_sc.shape, f32)
        acc_sc[...] = jnp.zeros(acc_sc.shape, f32)

    qbd = qbd_sc[...]

    def update(s, value):
        m_prev = m_sc[...]
        m_new = jnp.maximum(m_prev, jnp.max(s, axis=1, keepdims=True))
        alpha = jnp.exp2(m_prev - m_new)
        p = jnp.exp2(s - m_new)
        l_sc[...] = alpha * l_sc[...] + jnp.sum(p, axis=1, keepdims=True)
        pb = p.astype(bf16)
        parts = []
        for h in heads:
            ph = pb[h * rows_h:(h + 1) * rows_h]
            t = None
            for i in range(s.shape[1] // PAGE):
                u = jnp.dot(ph[:, i * PAGE:(i + 1) * PAGE], value(i, h), preferred_element_type=f32)
                t = u if t is None else t + u
            parts.append(t)
        acc_sc[...] = acc_sc[...] * alpha + jnp.concatenate(parts, axis=0)
        m_sc[...] = m_new

    s_past = jnp.concatenate(
        [jnp.dot(qbd, kp[0].astype(bf16), preferred_element_type=f32) for kp in k_pages], axis=1)
    update(s_past, lambda i, h: v_pages[i][0, pl.ds(h, PAGE, stride=N_HEADS), :].astype(bf16))

    @pl.when(step == pl.num_programs(1) - 1)
    def _():
        pad = jnp.zeros((PAGE - ds, width), f32)
        kn = jnp.concatenate([kn_ref[0], pad], axis=0).astype(bf16)
        vn = jnp.concatenate([vn_ref[0], pad], axis=0).astype(bf16)
        s_new = lax.dot_general(qbd, kn, (((1,), (1,)), ((), ())), preferred_element_type=f32)
        key = lax.broadcasted_iota(i32, s_new.shape, 1)
        qry = lax.broadcasted_iota(i32, s_new.shape, 0) & (ds - 1)
        update(jnp.where(key <= qry, s_new, NEG), lambda i, h: vn[:, h * DV:(h + 1) * DV])
        lam = _diff_lambda(lam_ref, lam_init)
        out = acc_sc[...] * (1.0 / l_sc[...])
        for h in heads:
            o = out[h * rows_h:h * rows_h + ds] - lam * out[h * rows_h + ds:(h + 1) * rows_h]
            o_ref[0, :, h * DV:(h + 1) * DV] = _rms(o, g_ref[h:h + 1, :] * (1.0 - lam_init)).astype(bf16)


def _attn_sample(page_table, lam_p, g_attn, q, k_new, v_new, cache_kt, cache_vr, lam_init):
    nseq, ds, w = q.shape
    pages = page_table.shape[1]
    pps = PAGES_PER_STEP
    assert pages % pps == 0 and ds & (ds - 1) == 0
    seq_blk = pl.BlockSpec((1, ds, w), lambda b, g, pt: (b, 0, 0))

    def page_spec(i, shape):
        return pl.BlockSpec((1,) + shape, lambda b, g, pt: (pt[b, g * pps + i], 0, 0))

    rows = N_HEADS * 2 * ds
    grid_spec = pltpu.PrefetchScalarGridSpec(
        num_scalar_prefetch=1,
        grid=(nseq, pages // pps),
        in_specs=[pl.BlockSpec(lam_p.shape, lambda b, g, pt: (0, 0)),
                  pl.BlockSpec(g_attn.shape, lambda b, g, pt: (0, 0)),
                  seq_blk, seq_blk, seq_blk]
                 + [page_spec(i, cache_kt.shape[1:]) for i in range(pps)]
                 + [page_spec(i, cache_vr.shape[1:]) for i in range(pps)],
        out_specs=seq_blk,
        scratch_shapes=[pltpu.VMEM((rows, w), bf16), pltpu.VMEM((rows, 1), f32),
                        pltpu.VMEM((rows, 1), f32), pltpu.VMEM((rows, DV), f32)],
    )
    return pl.pallas_call(
        functools.partial(_attn_sample_kernel, lam_init=lam_init, n_pages=pps),
        grid_spec=grid_spec,
        out_shape=SDS((nseq, ds, w), bf16),
        compiler_params=_cparams(("arbitrary", "arbitrary")),
        name="attn_sample",
    )(page_table, lam_p, g_attn, q, k_new, v_new, *([cache_kt] * pps), *([cache_vr] * pps))


def _out_route_kernel(oa_ref, ob_ref, x_ref, g1_ref, sh_ref, sc_ref, wa_ref, wb_ref, g2_ref, wr_ref, br_ref,
                      x1_ref, h2_ref, ids_ref, gates_ref, cnt_ref):
    mix = (jnp.dot(oa_ref[0], wa_ref[...], preferred_element_type=f32)
           + jnp.dot(ob_ref[0], wb_ref[...], preferred_element_type=f32))
    x1 = x_ref[0] + g1_ref[0] * mix
    x1_ref[0] = x1
    h2 = _rms(x1, g2_ref[...]) * (1.0 + sc_ref[0]) + sh_ref[0]
    h2_ref[0] = h2.astype(bf16)
    tm = h2.shape[0]
    wr = wr_ref[...]
    wr_hi = wr.astype(bf16)
    wr_lo = (wr - wr_hi.astype(f32)).astype(bf16)
    h_hi = h2.astype(bf16)
    h_lo = (h2 - h_hi.astype(f32)).astype(bf16)
    nt = (((1,), (1,)), ((), ()))
    r1 = lax.dot_general(jnp.concatenate([wr_hi, wr_lo], axis=0), h_hi, nt, preferred_element_type=f32)
    r2 = lax.dot_general(wr_hi, h_lo, nt, preferred_element_type=f32)
    lg = r1[:ROUTER_ROWS] + r1[ROUTER_ROWS:] + r2 + br_ref[:, 0:1]
    row = lax.broadcasted_iota(i32, (SUBLANES, tm), 0).astype(f32)
    big = float(SUBLANES)
    gl = jnp.where(row < N_EXPERT_GROUPS, lg[0:SUBLANES], NEG)
    gmax = jnp.max(gl, axis=0, keepdims=True)
    g_p = 1.0 / jnp.sum(jnp.exp(gl - gmax), axis=0, keepdims=True)
    gidx = jnp.min(jnp.where(gl == gmax, row, big), axis=0, keepdims=True)
    esel = jnp.zeros((SUBLANES, tm), f32)
    for g in range(N_EXPERT_GROUPS):
        esel = jnp.where(gidx == float(g), lg[SUBLANES * (g + 1):SUBLANES * (g + 2)], esel)
    e1 = jnp.max(esel, axis=0, keepdims=True)
    i1 = jnp.min(jnp.where(esel == e1, row, big), axis=0, keepdims=True)
    esel2 = jnp.where(row == i1, -jnp.inf, esel)
    e2 = jnp.max(esel2, axis=0, keepdims=True)
    i2 = jnp.min(jnp.where(esel2 == e2, row, big), axis=0, keepdims=True)
    t = jnp.exp(e2 - e1)
    w1 = g_p / (1.0 + t)
    ids = jnp.concatenate([gidx * EXPERTS_PER_GROUP + i1, gidx * EXPERTS_PER_GROUP + i2], axis=0).astype(i32)
    ids_ref[0] = ids
    gates_ref[0] = jnp.concatenate([w1, w1 * t], axis=0)
    e_iota = lax.broadcasted_iota(i32, (N_EXPERTS, tm), 0)
    onehot = jnp.where((e_iota == ids[0:1]) | (e_iota == ids[1:2]), 1.0, 0.0).astype(bf16)
    cnt_ref[0] = jnp.dot(onehot, jnp.ones((tm, LANES), bf16), preferred_element_type=f32)


def _out_route(oa, ob, x, gate1, shift2, scale2, w_a, w_b, g2, wr_t, br_b, *, tm):
    n, t, d = x.shape
    w_half = oa.shape[2]
    mod_rows = gate1.shape[1]
    tpn = t // tm

    def mod_spec():
        if mod_rows == 1:
            return pl.BlockSpec((1, 1, d), lambda b, i: (b, 0, 0))
        return pl.BlockSpec((1, tm, d), lambda b, i: (b, i, 0))

    full = lambda shp: pl.BlockSpec(shp, lambda b, i: (0,) * len(shp))
    tok = lambda w: pl.BlockSpec((1, tm, w), lambda b, i: (b, i, 0))
    return pl.pallas_call(
        _out_route_kernel,
        grid=(n, tpn),
        in_specs=[tok(w_half), tok(w_half), tok(d), mod_spec(), mod_spec(), mod_spec(),
                  full((w_half, d)), full((w_half, d)), full((1, d)),
                  full((ROUTER_ROWS, d)), full((ROUTER_ROWS, LANES))],
        out_specs=[tok(d), tok(d),
                   pl.BlockSpec((1, 2, tm), lambda b, i: (b * tpn + i, 0, 0)),
                   pl.BlockSpec((1, 2, tm), lambda b, i: (b * tpn + i, 0, 0)),
                   pl.BlockSpec((1, N_EXPERTS, LANES), lambda b, i: (b * tpn + i, 0, 0))],
        out_shape=[SDS((n, t, d), f32), SDS((n, t, d), bf16), SDS((n * tpn, 2, tm), i32),
                   SDS((n * tpn, 2, tm), f32), SDS((n * tpn, N_EXPERTS, LANES), f32)],
        compiler_params=_cparams(("arbitrary", "arbitrary")),
        name="out_route",
    )(oa, ob, x, gate1, shift2, scale2, w_a, w_b, g2, wr_t, br_b)


def _chunk_list(counts, src0, dst0, rows, kmax):
    cin = jnp.cumsum(counts, axis=1)
    cex = cin - counts
    k = jnp.arange(kmax, dtype=i32)
    owner = jnp.sum((cin[:, None, :] <= k[None, :, None]).astype(i32), axis=2)
    owner = jnp.minimum(owner, N_EXPERTS - 1)
    pick = (owner[:, :, None] == jnp.arange(N_EXPERTS, dtype=i32)).astype(i32)
    at_owner = lambda v: jnp.sum(pick * v[:, None, :], axis=2)
    within = (k[None, :] - at_owner(cex)) * rows
    src = at_owner(src0) + within
    dst = at_owner(dst0) + within
    return src.reshape(-1).astype(i32), dst.reshape(-1).astype(i32), cin[:, -1].astype(i32)


def _route_plan(cnt, nb_max, sorted_rows):
    pc = _round_up(cnt, RUN_ALIGN)
    toff = jnp.cumsum(pc, axis=1) - pc
    tot = jnp.sum(pc, axis=0)
    seg = _round_up(tot, EXPERT_BLOCK)
    seg_end = jnp.cumsum(seg)
    seg_start = seg_end - seg
    run_start = seg_start[None, :] + jnp.cumsum(pc, axis=0) - pc
    nb = seg_end[-1] // EXPERT_BLOCK
    lists = []
    done = jnp.zeros_like(pc)
    for rows in CHUNK_ROWS:
        n = (pc - done) // rows
        kmax = sorted_rows // rows if rows == CHUNK_ROWS[0] else N_EXPERTS
        lists.extend(_chunk_list(n, toff + done, run_start + done, rows, kmax))
        done = done + n * rows
    return dict(
        lists=lists,
        tail_start=(seg_start + tot).astype(i32), tail8=((seg - tot) // RUN_ALIGN).astype(i32),
        tail_tot8=(jnp.sum(seg - tot) // RUN_ALIGN).reshape(1).astype(i32),
        seg_start=seg_start.astype(i32), seg_blocks=(seg // EXPERT_BLOCK).astype(i32),
        nb=nb.reshape(1).astype(i32),
        toff_v=jnp.broadcast_to(toff.astype(f32)[:, :, None], toff.shape + (LANES,)))


def _sorted_positions(ids_ref, toff_ref, upper_ref, tm):
    ids = ids_ref[0]
    idc = jnp.concatenate([ids[0:1], ids[1:2]], axis=1)
    e_iota = lax.broadcasted_iota(i32, (N_EXPERTS, 2 * tm), 0)
    onehot = jnp.where(e_iota == idc, 1.0, 0.0)
    before = jnp.dot(onehot.astype(bf16), upper_ref[...], preferred_element_type=f32)
    pos = jnp.sum(onehot * (before + toff_ref[0][:, 0:1]), axis=0, keepdims=True)
    return jnp.where(idc >= 0, pos, -1.0)


def _fill_upper(upper_ref):
    n = upper_ref.shape[0]
    a_src = lax.broadcasted_iota(i32, (n, n), 0)
    a_dst = lax.broadcasted_iota(i32, (n, n), 1)
    upper_ref[...] = jnp.where(a_src < a_dst, 1.0, 0.0).astype(bf16)


def _aligned(row):
    return row if isinstance(row, int) else pl.multiple_of(row, RUN_ALIGN)


def _run_chunk_copy(vmem_buf, hbm_buf, sem, vrow, hrow, to_hbm, rows=RUN_ALIGN):
    v = vmem_buf.at[pl.ds(_aligned(vrow), rows)]
    h = hbm_buf.at[pl.ds(_aligned(hrow), rows)]
    return pltpu.make_async_copy(v, h, sem) if to_hbm else pltpu.make_async_copy(h, v, sem)


def _tile_chunks(i, list_refs, vmem_buf, hbm_buf, sems, to_hbm, wait):
    for c, rows in enumerate(CHUNK_ROWS):
        src_ref, dst_ref, n_ref = list_refs[3 * c:3 * c + 3]
        stride = src_ref.shape[0] // n_ref.shape[0]

        def one(k, carry, rows=rows, src_ref=src_ref, dst_ref=dst_ref, stride=stride, sem=sems.at[c]):
            if wait:
                _run_chunk_copy(vmem_buf, hbm_buf, sem, 0, 0, to_hbm, rows).wait()
            else:
                _run_chunk_copy(vmem_buf, hbm_buf, sem, src_ref[i * stride + k], dst_ref[i * stride + k],
                                to_hbm, rows).start()
            return carry

        lax.fori_loop(0, n_ref[i], one, 0)


def _dispatch_kernel(*refs):
    lists = refs[:N_LIST]
    tstart_ref, tail8_ref, ttot_ref, nb_ref = refs[N_LIST:N_LIST + 4]
    hm_ref, ht_ref, ids_ref, toff_ref, xs_ref, upper_sc, xbuf_sc, zero_sc, sem = refs[N_LIST + 4:]
    i = pl.program_id(0)
    last = pl.num_programs(0) - 1
    tm = hm_ref.shape[0]
    rt = xbuf_sc.shape[1]
    slot = i & 1
    sem_tail, sem_dead = sem.at[len(CHUNK_ROWS)], sem.at[len(CHUNK_ROWS) + 1]

    @pl.when(i == 0)
    def _():
        _fill_upper(upper_sc)
        zero_sc[...] = jnp.zeros(zero_sc.shape, f32)

    h = jnp.where(i == last, ht_ref[...], hm_ref[...])
    pos = _sorted_positions(ids_ref, toff_ref, upper_sc, tm)
    r_iota = lax.broadcasted_iota(i32, (rt, tm), 0).astype(f32)
    perm = jnp.where((r_iota == pos[:, :tm]) | (r_iota == pos[:, tm:]), 1.0, 0.0).astype(bf16)
    xbuf_sc[slot] = jnp.dot(perm, h, preferred_element_type=f32)

    @pl.when(i > 0)
    def _():
        _tile_chunks(i - 1, lists, xbuf_sc.at[1 - slot], xs_ref, sem, True, wait=True)

    _tile_chunks(i, lists, xbuf_sc.at[slot], xs_ref, sem, True, wait=False)

    @pl.when(i == last)
    def _():
        _tile_chunks(i, lists, xbuf_sc.at[slot], xs_ref, sem, True, wait=True)

        def per_expert(e, c):
            def per_chunk(k, cc):
                _run_chunk_copy(zero_sc, xs_ref, sem_tail, 0, tstart_ref[e] + k * RUN_ALIGN, True).start()
                return cc
            lax.fori_loop(0, tail8_ref[e], per_chunk, 0)
            return c
        lax.fori_loop(0, N_EXPERTS, per_expert, 0)

        def wait_one(k, c):
            _run_chunk_copy(zero_sc, xs_ref, sem_tail, 0, 0, True).wait()
            return c
        lax.fori_loop(0, ttot_ref[0], wait_one, 0)

        def dead_block(row):
            return pltpu.make_async_copy(zero_sc, xs_ref.at[pl.ds(row, EXPERT_BLOCK)], sem_dead)

        n_dead = xs_ref.shape[0] // EXPERT_BLOCK - nb_ref[0]

        def start_dead(k, c):
            dead_block(pl.multiple_of((nb_ref[0] + k) * EXPERT_BLOCK, EXPERT_BLOCK)).start()
            return c
        lax.fori_loop(0, n_dead, start_dead, 0)

        def wait_dead(k, c):
            dead_block(0).wait()
            return c
        lax.fori_loop(0, n_dead, wait_dead, 0)


def _sorted_rows(tm):
    return _round_up(2 * tm + N_EXPERTS * (RUN_ALIGN - 1), LANES)


def _dispatch(plan, h_main, h_tail, ids, toff_v, rows_total):
    tiles, _, tm = ids.shape
    d = h_main.shape[1]
    tiles_main = h_main.shape[0] // tm
    assert tiles == tiles_main + 1 and h_tail.shape[0] == tm
    rt = _sorted_rows(tm)
    grid_spec = pltpu.PrefetchScalarGridSpec(
        num_scalar_prefetch=N_LIST + 4,
        grid=(tiles,),
        in_specs=[pl.BlockSpec((tm, d), lambda i, *_: (jnp.minimum(i, tiles_main - 1), 0)),
                  pl.BlockSpec((tm, d), lambda i, *_: (0, 0)),
                  pl.BlockSpec((1, 2, tm), lambda i, *_: (i, 0, 0)),
                  pl.BlockSpec((1, N_EXPERTS, LANES), lambda i, *_: (i, 0, 0))],
        out_specs=pl.BlockSpec(memory_space=pl.ANY),
        scratch_shapes=[pltpu.VMEM((2 * tm, 2 * tm), bf16), pltpu.VMEM((2, rt, d), f32),
                        pltpu.VMEM((EXPERT_BLOCK, d), f32), pltpu.SemaphoreType.DMA((len(CHUNK_ROWS) + 2,))],
    )
    return pl.pallas_call(
        _dispatch_kernel,
        grid_spec=grid_spec,
        out_shape=SDS((rows_total, d), f32),
        compiler_params=_cparams(("arbitrary",)),
        name="dispatch",
    )(*plan["lists"], plan["tail_start"], plan["tail8"], plan["tail_tot8"], plan["nb"],
      h_main, h_tail, ids, toff_v)


def _experts_kernel(seg_ref, nblk_ref, nb_ref, x_ref, wg_ref, wu_ref, wd_ref, y_ref,
                    wg_sc, wu_sc, wd_sc, xbuf, ybuf, sem_in, sem_out):
    e = pl.program_id(0)
    eb = EXPERT_BLOCK
    nblk = nblk_ref[e]
    row0 = seg_ref[e]

    def rows(b):
        return pl.ds(pl.multiple_of(row0 + b * eb, eb), eb)

    def x_copy(b, slot):
        return pltpu.make_async_copy(x_ref.at[rows(b)], xbuf.at[slot], sem_in.at[slot])

    def y_copy(b, slot):
        return pltpu.make_async_copy(ybuf.at[slot], y_ref.at[rows(b)], sem_out.at[slot])

    @pl.when(nblk > 0)
    def _():
        x_copy(0, 0).start()
        wg_sc[...] = wg_ref[0].astype(bf16)
        wu_sc[...] = wu_ref[0].astype(bf16)
        wd_sc[...] = wd_ref[0].astype(bf16)

        def block(b, c):
            slot = b & 1
            x_copy(b, slot).wait()

            @pl.when(b + 1 < nblk)
            def _():
                x_copy(b + 1, 1 - slot).start()

            xb = xbuf[slot].astype(bf16)
            a = jnp.dot(xb, wg_sc[...], preferred_element_type=f32)
            g = jnp.dot(xb, wu_sc[...], preferred_element_type=f32)
            hid = (jax.nn.silu(a) * g).astype(bf16)
            y = jnp.dot(hid, wd_sc[...], preferred_element_type=f32)

            @pl.when(b >= 2)
            def _():
                y_copy(b - 2, slot).wait()

            ybuf[slot] = y
            y_copy(b, slot).start()
            return c

        lax.fori_loop(0, nblk, block, 0)

        @pl.when(nblk >= 2)
        def _():
            y_copy(nblk - 2, nblk & 1).wait()

        y_copy(nblk - 1, (nblk - 1) & 1).wait()

    @pl.when(e == pl.num_programs(0) - 1)
    def _():
        def dead_block(b):
            return pltpu.make_async_copy(ybuf.at[0], y_ref.at[pl.ds(pl.multiple_of(b * eb, eb), eb)], sem_out.at[0])

        n_dead = y_ref.shape[0] // eb - nb_ref[0]
        ybuf[0] = jnp.zeros(ybuf.shape[1:], f32)

        def start_dead(k, c):
            dead_block(nb_ref[0] + k).start()
            return c
        lax.fori_loop(0, n_dead, start_dead, 0)

        def wait_dead(k, c):
            dead_block(0).wait()
            return c
        lax.fori_loop(0, n_dead, wait_dead, 0)


def _experts(plan, xs, w_gate, w_up, w_down):
    rows_total, d = xs.shape
    n_exp, _, de = w_gate.shape
    eb = EXPERT_BLOCK
    grid_spec = pltpu.PrefetchScalarGridSpec(
        num_scalar_prefetch=3,
        grid=(n_exp,),
        in_specs=[pl.BlockSpec(memory_space=pl.ANY),
                  pl.BlockSpec((1, d, de), lambda e, *_: (e, 0, 0)),
                  pl.BlockSpec((1, d, de), lambda e, *_: (e, 0, 0)),
                  pl.BlockSpec((1, de, d), lambda e, *_: (e, 0, 0))],
        out_specs=pl.BlockSpec(memory_space=pl.ANY),
        scratch_shapes=[pltpu.VMEM((d, de), bf16), pltpu.VMEM((d, de), bf16), pltpu.VMEM((de, d), bf16),
                        pltpu.VMEM((2, eb, d), f32), pltpu.VMEM((2, eb, d), f32),
                        pltpu.SemaphoreType.DMA((2,)), pltpu.SemaphoreType.DMA((2,))],
    )
    return pl.pallas_call(
        _experts_kernel,
        grid_spec=grid_spec,
        out_shape=SDS((rows_total, d), f32),
        compiler_params=_cparams(("arbitrary",)),
        name="experts",
    )(plan["seg_start"], plan["seg_blocks"], plan["nb"], xs, w_gate, w_up, w_down)


def _combine_kernel(*refs, final):
    lists = refs[:N_LIST]
    (xm_ref, xt_ref, gm_ref, gt_ref, gf_ref, ids_ref, gates_ref, toff_ref, y_ref,
     om_ref, ot_ref, upper_sc, ybuf_sc, sem) = refs[N_LIST:]
    i = pl.program_id(0)
    last = pl.num_programs(0) - 1
    tm = xm_ref.shape[0]
    rt = ybuf_sc.shape[1]
    slot = i & 1

    @pl.when(i == 0)
    def _():
        _fill_upper(upper_sc)
        ybuf_sc[...] = jnp.zeros(ybuf_sc.shape, f32)
        _tile_chunks(0, lists, ybuf_sc.at[0], y_ref, sem, False, wait=False)

    _tile_chunks(i, lists, ybuf_sc.at[slot], y_ref, sem, False, wait=True)

    @pl.when(i < last)
    def _():
        _tile_chunks(i + 1, lists, ybuf_sc.at[1 - slot], y_ref, sem, False, wait=False)

    pos = _sorted_positions(ids_ref, toff_ref, upper_sc, tm)
    gates = gates_ref[0]
    r_iota = lax.broadcasted_iota(i32, (rt, tm), 0).astype(f32)
    weights = (jnp.where(r_iota == pos[:, :tm], gates[0:1], 0.0)
               + jnp.where(r_iota == pos[:, tm:], gates[1:2], 0.0)).astype(bf16)
    ff = lax.dot_general(weights, ybuf_sc[slot].astype(bf16), (((0,), (0,)), ((), ())),
                         preferred_element_type=f32)

    def finish(x1, gate):
        x2 = x1 + gate * ff
        return _rms(x2, gf_ref[...]) if final else x2

    @pl.when(i < last)
    def _():
        om_ref[...] = finish(xm_ref[...], gm_ref[0])

    @pl.when(i == last)
    def _():
        ot_ref[...] = finish(xt_ref[...], gt_ref[...])


def _combine(plan, x_main, x_tail, gate_main, gate_tail, g_final, ids, gates, toff_v, y, *, final):
    tiles, _, tm = ids.shape
    d = x_main.shape[1]
    tiles_main = x_main.shape[0] // tm
    tpr = tiles_main // gate_main.shape[0]
    rt = _sorted_rows(tm)
    main_idx = lambda i, *_: (jnp.minimum(i, tiles_main - 1), 0)
    tail_spec = pl.BlockSpec((tm, d), lambda i, *_: (0, 0))
    slot_spec = pl.BlockSpec((1, 2, tm), lambda i, *_: (i, 0, 0))
    grid_spec = pltpu.PrefetchScalarGridSpec(
        num_scalar_prefetch=N_LIST,
        grid=(tiles,),
        in_specs=[pl.BlockSpec((tm, d), main_idx), tail_spec,
                  pl.BlockSpec((1, 1, d), lambda i, *_: (jnp.minimum(i, tiles_main - 1) // tpr, 0, 0)), tail_spec,
                  pl.BlockSpec((1, d), lambda i, *_: (0, 0)), slot_spec, slot_spec,
                  pl.BlockSpec((1, N_EXPERTS, LANES), lambda i, *_: (i, 0, 0)),
                  pl.BlockSpec(memory_space=pl.ANY)],
        out_specs=[pl.BlockSpec((tm, d), main_idx), tail_spec],
        scratch_shapes=[pltpu.VMEM((2 * tm, 2 * tm), bf16), pltpu.VMEM((2, rt, d), f32),
                        pltpu.SemaphoreType.DMA((len(CHUNK_ROWS),))],
    )
    return pl.pallas_call(
        functools.partial(_combine_kernel, final=final),
        grid_spec=grid_spec,
        out_shape=[SDS(x_main.shape, f32), SDS((tm, d), f32)],
        compiler_params=_cparams(("arbitrary",)),
        name="combine",
    )(*plan["lists"], x_main, x_tail, gate_main, gate_tail, g_final, ids, gates, toff_v, y)


def _moe(main, tail, g_final, w_gate, w_up, w_down, *, tm, final):
    n, t, d = main["x1"].shape
    t_tail = tail["x1"].shape[1]
    pad = tm - t_tail
    tiles = n * t // tm + 1
    rows_max = 2 * (n * t + t_tail) + tiles * N_EXPERTS * (RUN_ALIGN - 1) + N_EXPERTS * (EXPERT_BLOCK - 1)
    nb_max = -(-rows_max // EXPERT_BLOCK)
    rows2 = lambda a: jnp.pad(a.reshape(t_tail, d), ((0, pad), (0, 0)))
    ids = jnp.concatenate([main["ids"], jnp.pad(tail["ids"], ((0, 0), (0, 0), (0, pad)), constant_values=-1)])
    gates = jnp.concatenate([main["gates"], jnp.pad(tail["gates"], ((0, 0), (0, 0), (0, pad)))])
    cnt = jnp.concatenate([main["cnt"], tail["cnt"]])[:, :, 0].astype(i32)
    plan = _route_plan(cnt, nb_max, _sorted_rows(tm))
    xs = _dispatch(plan, main["h2"].reshape(n * t, d), rows2(tail["h2"]), ids, plan["toff_v"],
                   nb_max * EXPERT_BLOCK)
    y = _experts(plan, xs, w_gate, w_up, w_down)
    out_m, out_t = _combine(plan, main["x1"].reshape(n * t, d), rows2(tail["x1"]), main["gate"],
                            rows2(tail["gate"]), g_final, ids, gates, plan["toff_v"], y, final=final)
    return out_m.reshape(n, t, d), out_t[:t_tail].reshape(1, t_tail, d)


def _layer_group(x, mods, tables, attend, p, *, tm, mix, emit_vn):
    ws_t, mask, bs_b = mix
    outs = _in_proj(x, mods[0], mods[1], p["g1"], p["w_in"], tables, ws_t, mask, bs_b, p["g_v"], p["g_mlp"],
                    tm=tm, emit_vn=emit_vn)
    q, k, v, kb, vb, ob = outs[:6]
    oa = attend(q, k, v, kb, vb)
    x1, h2, ids, gates, cnt = _out_route(oa, ob, x, mods[2], mods[3], mods[4], p["w_out_a"], p["w_out_b"],
                                         p["g2"], p["wr_t"], p["br_b"], tm=tm)
    routed = dict(x1=x1, h2=h2, ids=ids, gates=gates, cnt=cnt, gate=mods[5])
    return routed, k, v, (outs[6] if emit_vn else None)


def kernel(x_prompt, x_sample, cache_k, cache_v, page_table, c_prompt, c_sample, w_ada, b_ada, g_norm1, w_in,
           lam_p, g_attn, g_v, w_s, b_s, g_mlp, w_out, g_norm2, w_router_g, b_router_g, w_router_e, b_router_e,
           w_exp_gate, w_exp_up, w_exp_down, g_final):
    nb, seq, d = x_prompt.shape
    ndec, dseq, _ = x_sample.shape
    depth = w_ada.shape[0]
    past_len = page_table.shape[1] * PAGE
    n_pool = cache_k.shape[1]
    a_w = N_HEADS * DV
    tm_s = ndec * dseq

    tables_p, tables_s8 = _rope_tables(seq, dseq, past_len)
    tables_s = tuple(jnp.tile(t, (ndec, 1)) for t in tables_s8)
    c_all = jnp.concatenate([c_prompt, c_sample], axis=0)
    c_pad = _round_up(c_all.shape[0], SUBLANES) - c_all.shape[0]
    c_all = jnp.pad(c_all, ((0, c_pad), (0, 0)))

    tri = jnp.tril(jnp.ones((CHUNK, CHUNK), f32))
    idx = jnp.arange(tm_s)
    mask_s = ((idx[:, None] // dseq == idx[None, :] // dseq) & (idx[None, :] % dseq <= idx[:, None] % dseq)).astype(f32)

    xp, xs = x_prompt, x_sample.reshape(1, tm_s, d)
    kp_l, vp_l, ks_l, vs_l, cv_l = [], [], [], [], []
    for l in range(depth):
        lam_init = 0.8 - 0.6 * math.exp(-0.3 * l)
        mod = _adaln(c_all, w_ada[l], b_ada[l])
        mods_p = [mod[:nb, None, j * d:(j + 1) * d] for j in range(6)]
        mods_s = [jnp.repeat(mod[nb:nb + ndec, j * d:(j + 1) * d], dseq, axis=0)[None] for j in range(6)]
        wr_t = jnp.concatenate([w_router_g[l].T, jnp.zeros((SUBLANES - N_EXPERT_GROUPS, d), f32),
                                w_router_e[l].T], axis=0)
        br = jnp.concatenate([b_router_g[l], jnp.zeros((SUBLANES - N_EXPERT_GROUPS,), f32), b_router_e[l]])
        p = dict(
            g1=g_norm1[l].reshape(1, d), w_in=w_in[l].astype(bf16), g_v=g_v[l].reshape(1, -1),
            g_mlp=g_mlp[l].reshape(1, -1), w_out_a=w_out[l][:a_w].astype(bf16), w_out_b=w_out[l][a_w:].astype(bf16),
            g2=g_norm2[l].reshape(1, d), wr_t=wr_t, br_b=jnp.broadcast_to(br[:, None], (ROUTER_ROWS, LANES)),
            w_gate=w_exp_gate[l], w_up=w_exp_up[l], w_down=w_exp_down[l])
        gf = g_final.reshape(1, d)
        mix_p = (w_s[l], tri, jnp.broadcast_to(b_s[l][:, :, None], (N_GROUPS_B, CHUNK, C_B)))
        mix_s = (jnp.tile(w_s[l][:, :dseq, :dseq], (1, ndec, ndec)), mask_s,
                 jnp.broadcast_to(jnp.tile(b_s[l][:, :dseq], (1, ndec))[:, :, None], (N_GROUPS_B, tm_s, C_B)))

        attend_p = lambda q, k, v, kb, vb: _attn_prompt(lam_p[l], g_attn[l], q, kb, vb, lam_init)
        ck = jnp.transpose(cache_k[l], (0, 2, 3, 4, 1)).reshape(n_pool, -1, PAGE)
        cv = cache_v[l].reshape(n_pool, PAGE * N_HEADS, DV)

        def attend_s(q, k, v, kb, vb):
            o = _attn_sample(page_table, lam_p[l], g_attn[l], q.reshape(ndec, dseq, -1), k.reshape(ndec, dseq, -1),
                             v.reshape(ndec, dseq, -1), ck, cv, lam_init)
            return o.reshape(1, tm_s, -1)

        routed_p, k_p, v_p, _ = _layer_group(xp, mods_p, tables_p, attend_p, p, tm=TOKEN_TILE, mix=mix_p,
                                             emit_vn=False)
        routed_s, k_s, v_s, cv_s = _layer_group(xs, mods_s, tables_s, attend_s, p, tm=tm_s, mix=mix_s,
                                                emit_vn=True)
        xp, xs = _moe(routed_p, routed_s, gf, p["w_gate"], p["w_up"], p["w_down"], tm=TOKEN_TILE,
                      final=l == depth - 1)
        kp_l.append(k_p.reshape(nb, seq, N_HEADS, 2, DK))
        vp_l.append(v_p.reshape(nb, seq, N_HEADS, DV))
        ks_l.append(k_s.reshape(ndec, dseq, N_HEADS, 2, DK))
        vs_l.append(v_s.reshape(ndec, dseq, N_HEADS, DV))
        cv_l.append(cv_s.reshape(ndec, dseq, -1))
    return (xp, xs.reshape(ndec, dseq, d), jnp.stack(kp_l), jnp.stack(vp_l), jnp.stack(ks_l), jnp.stack(vs_l),
            jnp.stack(cv_l))
```

```python
import functools
import math

import jax
import jax.numpy as jnp
from jax import lax
from jax.experimental import pallas as pl
from jax.experimental.pallas import tpu as pltpu

f32 = jnp.float32
bf16 = jnp.bfloat16
i32 = jnp.int32
SDS = jax.ShapeDtypeStruct

N_HEADS = 4
DK = 64
DV = 2 * DK
ROT = DK // 4
ROPE_THETA = 500000.0
N_GROUPS_B = 4
C_B = 128
CHUNK = 128
N_EXPERT_GROUPS = 4
EXPERTS_PER_GROUP = 8
N_EXPERTS = N_EXPERT_GROUPS * EXPERTS_PER_GROUP
PAGE = 128
EPS = 1e-6
NEG = -1e30
LOG2E = math.log2(math.e)

LANES = 128
SUBLANES = 8
VMEM_LIMIT = 56 * 1024 * 1024

TOKEN_TILE = 512
ATTN_TQ = 256
ATTN_TK = 256
ONES_ROWS = 16
PAGES_PER_STEP = 16
EXPERT_BLOCK = 256
X_AHEAD = 2
RUN_ALIGN = 8
CHUNK_ROWS = (32, 16, 8)
N_LIST = 3 * len(CHUNK_ROWS)
ROUTER_ROWS = 40


def _cparams(sem=None):
    return pltpu.CompilerParams(dimension_semantics=sem, vmem_limit_bytes=VMEM_LIMIT)


def _round_up(x, m):
    return (x + m - 1) // m * m


def _adaln_kernel(c_ref, w_ref, b_ref, o_ref):
    a = jax.nn.silu(c_ref[...])
    o_ref[...] = jnp.dot(a, w_ref[...], preferred_element_type=f32,
                         precision=lax.Precision.HIGHEST) + b_ref[...]


def _adaln(c_all, w_ada, b_ada):
    n, d = c_all.shape
    m = w_ada.shape[1]
    tn = 1536
    return pl.pallas_call(
        _adaln_kernel,
        grid=(m // tn,),
        in_specs=[pl.BlockSpec((n, d), lambda j: (0, 0)),
                  pl.BlockSpec((d, tn), lambda j: (0, j)),
                  pl.BlockSpec((1, tn), lambda j: (0, j))],
        out_specs=pl.BlockSpec((n, tn), lambda j: (0, j)),
        out_shape=SDS((n, m), f32),
        compiler_params=_cparams(("arbitrary",)),
        name="adaln",
    )(c_all, w_ada, b_ada.reshape(1, m))


def _rope_kernel(inv_ref, cos_ref, sin_ref, *, rows_prompt, past_len):
    shape = cos_ref.shape
    r = lax.broadcasted_iota(i32, shape, 0)
    l = lax.broadcasted_iota(i32, shape, 1)
    base = jnp.where(r < rows_prompt, r * 16, past_len + (r - rows_prompt) * 16)
    pos = base + (l >> 3)
    ang = pos.astype(f32) * inv_ref[...]
    cos_ref[...] = jnp.cos(ang)
    sin_ref[...] = jnp.sin(ang)


def _rope_tables(seq, dec_seq, past_len):
    inv = ROPE_THETA ** (-jnp.arange(0, ROT, 2, dtype=f32) / ROT)
    inv_lane = jnp.tile(inv, LANES // (ROT // 2)).reshape(1, LANES)
    rp = seq // 16
    rt = rp + SUBLANES
    cos_c, sin_c = pl.pallas_call(
        functools.partial(_rope_kernel, rows_prompt=rp, past_len=past_len),
        out_shape=(SDS((rt, LANES), f32), SDS((rt, LANES), f32)),
        name="rope_table",
    )(inv_lane)

    def expand(c8, s8):
        n = c8.shape[0]
        z8 = jnp.zeros((n, ROT // 2), f32)
        rest = DK - ROT
        cos_t = jnp.concatenate([c8, c8, jnp.ones((n, rest), f32)], axis=1)
        sin_a = jnp.concatenate([-s8, z8, jnp.zeros((n, rest), f32)], axis=1)
        sin_b = jnp.concatenate([z8, s8, jnp.zeros((n, rest), f32)], axis=1)
        return tuple(jnp.tile(t, (1, LANES // DK)) for t in (cos_t, sin_a, sin_b))

    half = ROT // 2
    prompt = expand(cos_c[:rp].reshape(seq, half), sin_c[:rp].reshape(seq, half))
    sample = expand(cos_c[rp].reshape(16, half)[:dec_seq], sin_c[rp].reshape(16, half)[:dec_seq])
    return prompt, sample


def _rms(x, g):
    return x * lax.rsqrt(jnp.mean(x * x, axis=-1, keepdims=True) + EPS) * g


def _in_proj_kernel(x_ref, sh_ref, sc_ref, g1_ref, w_ref, cos_ref, sa_ref, sb_ref,
                    ws_ref, msk_ref, bs_ref, gv_ref, gm_ref,
                    q_ref, k_ref, v_ref, kb_ref, vb_ref, ob_ref, *vn_refs, mix_rows):
    x = x_ref[0]
    tm = x.shape[0]
    h = _rms(x, g1_ref[...]) * (1.0 + sc_ref[0]) + sh_ref[0]
    z = jnp.dot(h.astype(bf16), w_ref[...], preferred_element_type=f32)
    qk_w = N_HEADS * 2 * DK
    a_w = N_HEADS * DV
    b_w = N_GROUPS_B * C_B
    cos_t, sin_a, sin_b = cos_ref[...], sa_ref[...], sb_ref[...]
    for s in range(2 * qk_w // LANES):
        zs = z[:, s * LANES:(s + 1) * LANES]
        rot = zs * cos_t + pltpu.roll(zs, LANES - ROT // 2, 1) * sin_a + pltpu.roll(zs, ROT // 2, 1) * sin_b
        if s < qk_w // LANES:
            q_ref[0, :, s * LANES:(s + 1) * LANES] = (rot * (DK ** -0.5 * LOG2E)).astype(bf16)
        else:
            k_ref[0, :, s * LANES - qk_w:(s + 1) * LANES - qk_w] = rot
            kb_ref[0, :, s * LANES - qk_w:(s + 1) * LANES - qk_w] = rot.astype(bf16)
    v = z[:, 2 * qk_w:2 * qk_w + a_w]
    for h in range(N_HEADS):
        v_ref[0, pl.ds(h, tm, stride=N_HEADS), :] = v[:, h * DV:(h + 1) * DV]
    vb_ref[0] = v.astype(bf16)
    uv = jax.nn.gelu(z[:, 2 * qk_w + a_w:])
    nblk = tm // mix_rows
    for g in range(N_GROUPS_B):
        u = uv[:, g * C_B:(g + 1) * C_B]
        vn = _rms(uv[:, b_w + g * C_B:b_w + (g + 1) * C_B], gv_ref[:, g * C_B:(g + 1) * C_B])
        if vn_refs:
            vn_refs[0][0, :, g * C_B:(g + 1) * C_B] = vn
        wm = (ws_ref[g] * msk_ref[...]).astype(bf16)
        vcat = jnp.concatenate([vn[j * mix_rows:(j + 1) * mix_rows] for j in range(nblk)], axis=1)
        mixed = jnp.dot(wm, vcat.astype(bf16), preferred_element_type=f32)
        for j in range(nblk):
            t = u[j * mix_rows:(j + 1) * mix_rows] * (mixed[:, j * C_B:(j + 1) * C_B] + bs_ref[g])
            ob_ref[0, j * mix_rows:(j + 1) * mix_rows, g * C_B:(g + 1) * C_B] = _rms(
                t, gm_ref[:, g * C_B:(g + 1) * C_B]).astype(bf16)


def _in_proj(x, shift, scale, g1, w_in_bf, tables, ws_t, mask, bs_b, g_v, g_mlp, *, tm, emit_vn):
    n, t, d = x.shape
    in_w = w_in_bf.shape[1]
    r = ws_t.shape[1]
    mod_rows = shift.shape[1]
    tab_rows = tables[0].shape[0]
    w_half = N_HEADS * DV

    def mod_spec():
        if mod_rows == 1:
            return pl.BlockSpec((1, 1, d), lambda b, i: (b, 0, 0))
        return pl.BlockSpec((1, tm, d), lambda b, i: (b, i, 0))

    def tab_spec():
        if tab_rows == tm:
            return pl.BlockSpec((tm, LANES), lambda b, i: (0, 0))
        return pl.BlockSpec((tm, LANES), lambda b, i: (i, 0))

    full = lambda shp: pl.BlockSpec(shp, lambda b, i: (0,) * len(shp))
    tok = lambda w: pl.BlockSpec((1, tm, w), lambda b, i: (b, i, 0))
    out_shape = [SDS((n, t, w_half), bf16), SDS((n, t, w_half), f32), SDS((n, t * N_HEADS, DV), f32),
                 SDS((n, t, w_half), bf16), SDS((n, t, w_half), bf16), SDS((n, t, w_half), bf16)]
    out_specs = [tok(w_half)] * 6
    out_specs[2] = pl.BlockSpec((1, tm * N_HEADS, DV), lambda b, i: (b, i, 0))
    if emit_vn:
        out_shape.append(SDS((n, t, w_half), f32))
        out_specs.append(tok(w_half))
    return pl.pallas_call(
        functools.partial(_in_proj_kernel, mix_rows=r),
        grid=(n, t // tm),
        in_specs=[tok(d), mod_spec(), mod_spec(), full((1, d)), full((d, in_w)),
                  tab_spec(), tab_spec(), tab_spec(),
                  full((N_GROUPS_B, r, r)), full((r, r)), full((N_GROUPS_B, r, C_B)),
                  full((1, w_half)), full((1, w_half))],
        out_specs=out_specs,
        out_shape=out_shape,
        compiler_params=_cparams(("arbitrary", "arbitrary")),
        name="in_proj",
    )(x, shift, scale, g1, w_in_bf, *tables, ws_t, mask, bs_b, g_v, g_mlp)


def _diff_lambda(lam_ref, lam_init):
    lp = lam_ref[...]
    s1 = jnp.sum(lp[0:1] * lp[1:2], axis=1, keepdims=True)
    s2 = jnp.sum(lp[2:3] * lp[3:4], axis=1, keepdims=True)
    return jnp.exp(s1) - jnp.exp(s2) + lam_init


def _attn_prompt_kernel(lam_ref, g_ref, q_ref, k_ref, v_ref, o_ref, vt_sc, qq_sc, acc_sc, m_sc, s_sc, *, lam_init):
    seq = q_ref.shape[1]
    tq, tk = ATTN_TQ, ATTN_TK
    lam = _diff_lambda(lam_ref, lam_init)
    heads = range(N_HEADS)
    nt = (((1,), (1,)), ((), ()))

    def prep(j, c):
        sl = pl.ds(pl.multiple_of(j * tk, tk), tk)
        for h in heads:
            vt_sc[h, j, :DV] = v_ref[0, sl, h * DV:(h + 1) * DV].T
            vt_sc[h, j, DV:] = jnp.ones((ONES_ROWS, tk), bf16)
        return c

    lax.fori_loop(0, seq // tk, prep, 0)
    lane = lax.broadcasted_iota(i32, (tq, DV), 1)

    def q_block(qi, c):
        qsl = pl.ds(pl.multiple_of(qi * tq, tq), tq)
        for h in heads:
            qb = q_ref[0, qsl, h * DV:(h + 1) * DV]
            zero = jnp.zeros_like(qb)
            qq_sc[h, :tq] = jnp.where(lane < DK, qb, zero)
            qq_sc[h, tq:] = jnp.where(lane >= DK, qb, zero)
            m_sc[h] = jnp.full(m_sc.shape[1:], NEG, f32)
            acc_sc[h] = jnp.zeros(acc_sc.shape[1:], f32)

        def scores(j, h):
            ksl = pl.ds(pl.multiple_of(j * tk, tk), tk)
            return lax.dot_general(k_ref[0, ksl, h * DV:(h + 1) * DV], qq_sc[h], nt, preferred_element_type=f32)

        s_sc[...] = scores(0, 0)

        def kv_block(j, masked):
            s_next = None
            for h in heads:
                s = s_sc[...] if h == 0 else s_next
                if h + 1 < N_HEADS:
                    s_next = scores(j, h + 1)
                elif not masked:
                    s_sc[...] = scores(j + 1, 0)
                if masked:
                    key_i = lax.broadcasted_iota(i32, s.shape, 0)
                    qry_i = lax.broadcasted_iota(i32, s.shape, 1) & (tq - 1)
                    s = jnp.where(key_i <= qry_i, s, NEG)
                m_prev = m_sc[h]
                m_new = jnp.maximum(m_prev, jnp.max(s, axis=0, keepdims=True))
                alpha = jnp.exp2(m_prev - m_new)
                p = jnp.exp2(s - m_new)
                pv = jnp.dot(vt_sc[h, j], p.astype(bf16), preferred_element_type=f32)
                acc_sc[h] = acc_sc[h] * alpha + pv
                m_sc[h] = m_new

        def full_block(j, cc):
            kv_block(j, False)
            return cc

        def full_pair(jj, cc):
            kv_block(2 * jj, False)
            kv_block(2 * jj + 1, False)
            return cc

        pairs = qi >> 1
        lax.fori_loop(0, pairs, full_pair, 0)
        lax.fori_loop(2 * pairs, qi, full_block, 0)
        kv_block(qi, True)
        for h in heads:
            acc = acc_sc[h]
            inv_l = 1.0 / acc[DV:DV + 1, :]
            o_t = acc[:DV, :tq] * inv_l[:, :tq] - lam * (acc[:DV, tq:] * inv_l[:, tq:])
            o_ref[0, qsl, h * DV:(h + 1) * DV] = _rms(o_t.T, g_ref[h:h + 1, :] * (1.0 - lam_init)).astype(bf16)
        return c

    lax.fori_loop(0, seq // tq, q_block, 0)


def _attn_prompt(lam_p, g_attn, q, k, v, lam_init):
    n, seq, w = q.shape
    blk = pl.BlockSpec((1, seq, w), lambda b: (b, 0, 0))
    return pl.pallas_call(
        functools.partial(_attn_prompt_kernel, lam_init=lam_init),
        grid=(n,),
        in_specs=[pl.BlockSpec(lam_p.shape, lambda b: (0, 0)),
                  pl.BlockSpec(g_attn.shape, lambda b: (0, 0)),
                  blk, blk, blk],
        out_specs=blk,
        out_shape=SDS((n, seq, w), bf16),
        scratch_shapes=[pltpu.VMEM((N_HEADS, seq // ATTN_TK, DV + ONES_ROWS, ATTN_TK), bf16),
                        pltpu.VMEM((N_HEADS, 2 * ATTN_TQ, DV), bf16),
                        pltpu.VMEM((N_HEADS, DV + ONES_ROWS, 2 * ATTN_TQ), f32),
                        pltpu.VMEM((N_HEADS, 1, 2 * ATTN_TQ), f32),
                        pltpu.VMEM((ATTN_TK, 2 * ATTN_TQ), f32)],
        compiler_params=_cparams(("arbitrary",)),
        name="attn_prompt",
    )(lam_p, g_attn, q, k, v)


def _attn_sample_kernel(pt_ref, lam_ref, g_ref, q_ref, kn_ref, vn_ref, *rest, lam_init, n_pages):
    k_pages = rest[:n_pages]
    v_pages = rest[n_pages:2 * n_pages]
    o_ref, qbd_sc, m_sc, l_sc, acc_sc = rest[2 * n_pages:]
    step = pl.program_id(1)
    ds = q_ref.shape[1]
    rows_h = 2 * ds
    rows = N_HEADS * rows_h
    width = q_ref.shape[2]
    heads = range(N_HEADS)
    row_i = lax.broadcasted_iota(i32, (rows, width), 0)
    col_i = lax.broadcasted_iota(i32, (rows, width), 1)

    @pl.when(step == 0)
    def _():
        qt = jnp.concatenate([q_ref[0].astype(f32)] * (N_HEADS * 2), axis=0)
        same_map = (col_i >> (DK.bit_length() - 1)) == (row_i >> (ds.bit_length() - 1))
        qbd_sc[...] = jnp.where(same_map, qt, 0.0).astype(bf16)
        m_sc[...] = jnp.full(m_sc.shape, NEG, f32)
        l_sc[...] = jnp.zeros(l_sc.shape, f32)
        acc_sc[...] = jnp.zeros(acc_sc.shape, f32)

    qbd = qbd_sc[...]

    def update(s, value):
        m_prev = m_sc[...]
        m_new = jnp.maximum(m_prev, jnp.max(s, axis=1, keepdims=True))
        alpha = jnp.exp2(m_prev - m_new)
        p = jnp.exp2(s - m_new)
        l_sc[...] = alpha * l_sc[...] + jnp.sum(p, axis=1, keepdims=True)
        pb = p.astype(bf16)
        parts = []
        for h in heads:
            ph = pb[h * rows_h:(h + 1) * rows_h]
            t = None
            for i in range(s.shape[1] // PAGE):
                u = jnp.dot(ph[:, i * PAGE:(i + 1) * PAGE], value(i, h), preferred_element_type=f32)
                t = u if t is None else t + u
            parts.append(t)
        acc_sc[...] = acc_sc[...] * alpha + jnp.concatenate(parts, axis=0)
        m_sc[...] = m_new

    s_past = jnp.concatenate(
        [jnp.dot(qbd, kp[0].astype(bf16), preferred_element_type=f32) for kp in k_pages], axis=1)
    update(s_past, lambda i, h: v_pages[i][0, pl.ds(h, PAGE, stride=N_HEADS), :].astype(bf16))

    @pl.when(step == pl.num_programs(1) - 1)
    def _():
        pad = jnp.zeros((PAGE - ds, width), f32)
        kn = jnp.concatenate([kn_ref[0], pad], axis=0).astype(bf16)
        vn = jnp.concatenate([vn_ref[0], pad], axis=0).astype(bf16)
        s_new = lax.dot_general(qbd, kn, (((1,), (1,)), ((), ())), preferred_element_type=f32)
        key = lax.broadcasted_iota(i32, s_new.shape, 1)
        qry = lax.broadcasted_iota(i32, s_new.shape, 0) & (ds - 1)
        update(jnp.where(key <= qry, s_new, NEG), lambda i, h: vn[:, h * DV:(h + 1) * DV])
        lam = _diff_lambda(lam_ref, lam_init)
        out = acc_sc[...] * (1.0 / l_sc[...])
        for h in heads:
            o = out[h * rows_h:h * rows_h + ds] - lam * out[h * rows_h + ds:(h + 1) * rows_h]
            o_ref[0, :, h * DV:(h + 1) * DV] = _rms(o, g_ref[h:h + 1, :] * (1.0 - lam_init)).astype(bf16)


def _attn_sample(page_table, lam_p, g_attn, q, k_new, v_new, cache_kt, cache_vr, lam_init):
    nseq, ds, w = q.shape
    pages = page_table.shape[1]
    pps = PAGES_PER_STEP
    assert pages % pps == 0 and ds & (ds - 1) == 0
    seq_blk = pl.BlockSpec((1, ds, w), lambda b, g, pt: (b, 0, 0))

    def page_spec(i, shape):
        return pl.BlockSpec((1,) + shape, lambda b, g, pt: (pt[b, g * pps + i], 0, 0))

    rows = N_HEADS * 2 * ds
    grid_spec = pltpu.PrefetchScalarGridSpec(
        num_scalar_prefetch=1,
        grid=(nseq, pages // pps),
        in_specs=[pl.BlockSpec(lam_p.shape, lambda b, g, pt: (0, 0)),
                  pl.BlockSpec(g_attn.shape, lambda b, g, pt: (0, 0)),
                  seq_blk, seq_blk, seq_blk]
                 + [page_spec(i, cache_kt.shape[1:]) for i in range(pps)]
                 + [page_spec(i, cache_vr.shape[1:]) for i in range(pps)],
        out_specs=seq_blk,
        scratch_shapes=[pltpu.VMEM((rows, w), bf16), pltpu.VMEM((rows, 1), f32),
                        pltpu.VMEM((rows, 1), f32), pltpu.VMEM((rows, DV), f32)],
    )
    return pl.pallas_call(
        functools.partial(_attn_sample_kernel, lam_init=lam_init, n_pages=pps),
        grid_spec=grid_spec,
        out_shape=SDS((nseq, ds, w), bf16),
        compiler_params=_cparams(("arbitrary", "arbitrary")),
        name="attn_sample",
    )(page_table, lam_p, g_attn, q, k_new, v_new, *([cache_kt] * pps), *([cache_vr] * pps))


def _out_route_kernel(oa_ref, ob_ref, x_ref, g1_ref, sh_ref, sc_ref, wa_ref, wb_ref, g2_ref, wr_ref, br_ref,
                      x1_ref, h2_ref, ids_ref, gates_ref, cnt_ref):
    mix = (jnp.dot(oa_ref[0], wa_ref[...], preferred_element_type=f32)
           + jnp.dot(ob_ref[0], wb_ref[...], preferred_element_type=f32))
    x1 = x_ref[0] + g1_ref[0] * mix
    x1_ref[0] = x1
    h2 = _rms(x1, g2_ref[...]) * (1.0 + sc_ref[0]) + sh_ref[0]
    h2_ref[0] = h2.astype(bf16)
    tm = h2.shape[0]
    wr = wr_ref[...]
    wr_hi = wr.astype(bf16)
    wr_lo = (wr - wr_hi.astype(f32)).astype(bf16)
    h_hi = h2.astype(bf16)
    h_lo = (h2 - h_hi.astype(f32)).astype(bf16)
    nt = (((1,), (1,)), ((), ()))
    r1 = lax.dot_general(jnp.concatenate([wr_hi, wr_lo], axis=0), h_hi, nt, preferred_element_type=f32)
    r2 = lax.dot_general(wr_hi, h_lo, nt, preferred_element_type=f32)
    lg = r1[:ROUTER_ROWS] + r1[ROUTER_ROWS:] + r2 + br_ref[:, 0:1]
    row = lax.broadcasted_iota(i32, (SUBLANES, tm), 0).astype(f32)
    big = float(SUBLANES)
    gl = jnp.where(row < N_EXPERT_GROUPS, lg[0:SUBLANES], NEG)
    gmax = jnp.max(gl, axis=0, keepdims=True)
    g_p = 1.0 / jnp.sum(jnp.exp(gl - gmax), axis=0, keepdims=True)
    gidx = jnp.min(jnp.where(gl == gmax, row, big), axis=0, keepdims=True)
    esel = jnp.zeros((SUBLANES, tm), f32)
    for g in range(N_EXPERT_GROUPS):
        esel = jnp.where(gidx == float(g), lg[SUBLANES * (g + 1):SUBLANES * (g + 2)], esel)
    e1 = jnp.max(esel, axis=0, keepdims=True)
    i1 = jnp.min(jnp.where(esel == e1, row, big), axis=0, keepdims=True)
    esel2 = jnp.where(row == i1, -jnp.inf, esel)
    e2 = jnp.max(esel2, axis=0, keepdims=True)
    i2 = jnp.min(jnp.where(esel2 == e2, row, big), axis=0, keepdims=True)
    t = jnp.exp(e2 - e1)
    w1 = g_p / (1.0 + t)
    ids = jnp.concatenate([gidx * EXPERTS_PER_GROUP + i1, gidx * EXPERTS_PER_GROUP + i2], axis=0).astype(i32)
    ids_ref[0] = ids
    gates_ref[0] = jnp.concatenate([w1, w1 * t], axis=0)
    e_iota = lax.broadcasted_iota(i32, (N_EXPERTS, tm), 0)
    onehot = jnp.where((e_iota == ids[0:1]) | (e_iota == ids[1:2]), 1.0, 0.0).astype(bf16)
    cnt_ref[0] = jnp.dot(onehot, jnp.ones((tm, LANES), bf16), preferred_element_type=f32)


def _out_route(oa, ob, x, gate1, shift2, scale2, w_a, w_b, g2, wr_t, br_b, *, tm):
    n, t, d = x.shape
    w_half = oa.shape[2]
    mod_rows = gate1.shape[1]
    tpn = t // tm

    def mod_spec():
        if mod_rows == 1:
            return pl.BlockSpec((1, 1, d), lambda b, i: (b, 0, 0))
        return pl.BlockSpec((1, tm, d), lambda b, i: (b, i, 0))

    full = lambda shp: pl.BlockSpec(shp, lambda b, i: (0,) * len(shp))
    tok = lambda w: pl.BlockSpec((1, tm, w), lambda b, i: (b, i, 0))
    return pl.pallas_call(
        _out_route_kernel,
        grid=(n, tpn),
        in_specs=[tok(w_half), tok(w_half), tok(d), mod_spec(), mod_spec(), mod_spec(),
                  full((w_half, d)), full((w_half, d)), full((1, d)),
                  full((ROUTER_ROWS, d)), full((ROUTER_ROWS, LANES))],
        out_specs=[tok(d), tok(d),
                   pl.BlockSpec((1, 2, tm), lambda b, i: (b * tpn + i, 0, 0)),
                   pl.BlockSpec((1, 2, tm), lambda b, i: (b * tpn + i, 0, 0)),
                   pl.BlockSpec((1, N_EXPERTS, LANES), lambda b, i: (b * tpn + i, 0, 0))],
        out_shape=[SDS((n, t, d), f32), SDS((n, t, d), bf16), SDS((n * tpn, 2, tm), i32),
                   SDS((n * tpn, 2, tm), f32), SDS((n * tpn, N_EXPERTS, LANES), f32)],
        compiler_params=_cparams(("arbitrary", "arbitrary")),
        name="out_route",
    )(oa, ob, x, gate1, shift2, scale2, w_a, w_b, g2, wr_t, br_b)


def _chunk_list(counts, src0, dst0, rows, kmax):
    cin = jnp.cumsum(counts, axis=1)
    cex = cin - counts
    k = jnp.arange(kmax, dtype=i32)
    owner = jnp.sum((cin[:, None, :] <= k[None, :, None]).astype(i32), axis=2)
    owner = jnp.minimum(owner, N_EXPERTS - 1)
    pick = (owner[:, :, None] == jnp.arange(N_EXPERTS, dtype=i32)).astype(i32)
    at_owner = lambda v: jnp.sum(pick * v[:, None, :], axis=2)
    within = (k[None, :] - at_owner(cex)) * rows
    src = at_owner(src0) + within
    dst = at_owner(dst0) + within
    return src.reshape(-1).astype(i32), dst.reshape(-1).astype(i32), cin[:, -1].astype(i32)


def _route_plan(cnt, nb_max, sorted_rows):
    pc = _round_up(cnt, RUN_ALIGN)
    toff = jnp.cumsum(pc, axis=1) - pc
    tot = jnp.sum(pc, axis=0)
    seg = _round_up(tot, EXPERT_BLOCK)
    seg_end = jnp.cumsum(seg)
    seg_start = seg_end - seg
    run_start = seg_start[None, :] + jnp.cumsum(pc, axis=0) - pc
    nb = seg_end[-1] // EXPERT_BLOCK
    lists = []
    done = jnp.zeros_like(pc)
    for rows in CHUNK_ROWS:
        n = (pc - done) // rows
        kmax = sorted_rows // rows if rows == CHUNK_ROWS[0] else N_EXPERTS
        lists.extend(_chunk_list(n, toff + done, run_start + done, rows, kmax))
        done = done + n * rows
    return dict(
        lists=lists,
        tail_start=(seg_start + tot).astype(i32), tail8=((seg - tot) // RUN_ALIGN).astype(i32),
        tail_tot8=(jnp.sum(seg - tot) // RUN_ALIGN).reshape(1).astype(i32),
        seg_start=seg_start.astype(i32), seg_blocks=(seg // EXPERT_BLOCK).astype(i32),
        nb=nb.reshape(1).astype(i32),
        toff_v=jnp.broadcast_to(toff.astype(f32)[:, :, None], toff.shape + (LANES,)))


def _sorted_positions(ids_ref, toff_ref, upper_ref, tm):
    ids = ids_ref[0]
    idc = jnp.concatenate([ids[0:1], ids[1:2]], axis=1)
    e_iota = lax.broadcasted_iota(i32, (N_EXPERTS, 2 * tm), 0)
    onehot = jnp.where(e_iota == idc, 1.0, 0.0)
    before = jnp.dot(onehot.astype(bf16), upper_ref[...], preferred_element_type=f32)
    pos = jnp.sum(onehot * (before + toff_ref[0][:, 0:1]), axis=0, keepdims=True)
    return jnp.where(idc >= 0, pos, -1.0)


def _fill_upper(upper_ref):
    n = upper_ref.shape[0]
    a_src = lax.broadcasted_iota(i32, (n, n), 0)
    a_dst = lax.broadcasted_iota(i32, (n, n), 1)
    upper_ref[...] = jnp.where(a_src < a_dst, 1.0, 0.0).astype(bf16)


def _aligned(row):
    return row if isinstance(row, int) else pl.multiple_of(row, RUN_ALIGN)


def _run_chunk_copy(vmem_buf, hbm_buf, sem, vrow, hrow, to_hbm, rows=RUN_ALIGN):
    v = vmem_buf.at[pl.ds(_aligned(vrow), rows)]
    h = hbm_buf.at[pl.ds(_aligned(hrow), rows)]
    return pltpu.make_async_copy(v, h, sem) if to_hbm else pltpu.make_async_copy(h, v, sem)


def _tile_chunks(i, list_refs, vmem_buf, hbm_buf, sems, to_hbm, wait):
    for c, rows in enumerate(CHUNK_ROWS):
        src_ref, dst_ref, n_ref = list_refs[3 * c:3 * c + 3]
        stride = src_ref.shape[0] // n_ref.shape[0]

        def one(k, carry, rows=rows, src_ref=src_ref, dst_ref=dst_ref, stride=stride, sem=sems.at[c]):
            if wait:
                _run_chunk_copy(vmem_buf, hbm_buf, sem, 0, 0, to_hbm, rows).wait()
            else:
                _run_chunk_copy(vmem_buf, hbm_buf, sem, src_ref[i * stride + k], dst_ref[i * stride + k],
                                to_hbm, rows).start()
            return carry

        lax.fori_loop(0, n_ref[i], one, 0)


def _dispatch_kernel(*refs):
    lists = refs[:N_LIST]
    tstart_ref, tail8_ref, ttot_ref, nb_ref = refs[N_LIST:N_LIST + 4]
    hm_ref, ht_ref, ids_ref, toff_ref, xs_ref, upper_sc, xbuf_sc, zero_sc, sem = refs[N_LIST + 4:]
    i = pl.program_id(0)
    last = pl.num_programs(0) - 1
    tm = hm_ref.shape[0]
    rt = xbuf_sc.shape[1]
    slot = i & 1
    sem_tail, sem_dead = sem.at[len(CHUNK_ROWS)], sem.at[len(CHUNK_ROWS) + 1]

    @pl.when(i == 0)
    def _():
        _fill_upper(upper_sc)
        zero_sc[...] = jnp.zeros(zero_sc.shape, f32)

    h = jnp.where(i == last, ht_ref[...], hm_ref[...])
    pos = _sorted_positions(ids_ref, toff_ref, upper_sc, tm)
    r_iota = lax.broadcasted_iota(i32, (rt, tm), 0).astype(f32)
    perm = jnp.where((r_iota == pos[:, :tm]) | (r_iota == pos[:, tm:]), 1.0, 0.0).astype(bf16)
    xbuf_sc[slot] = jnp.dot(perm, h, preferred_element_type=f32)

    @pl.when(i > 0)
    def _():
        _tile_chunks(i - 1, lists, xbuf_sc.at[1 - slot], xs_ref, sem, True, wait=True)

    _tile_chunks(i, lists, xbuf_sc.at[slot], xs_ref, sem, True, wait=False)

    @pl.when(i == last)
    def _():
        _tile_chunks(i, lists, xbuf_sc.at[slot], xs_ref, sem, True, wait=True)

        def per_expert(e, c):
            def per_chunk(k, cc):
                _run_chunk_copy(zero_sc, xs_ref, sem_tail, 0, tstart_ref[e] + k * RUN_ALIGN, True).start()
                return cc
            lax.fori_loop(0, tail8_ref[e], per_chunk, 0)
            return c
        lax.fori_loop(0, N_EXPERTS, per_expert, 0)

        def wait_one(k, c):
            _run_chunk_copy(zero_sc, xs_ref, sem_tail, 0, 0, True).wait()
            return c
        lax.fori_loop(0, ttot_ref[0], wait_one, 0)

        def dead_block(row):
            return pltpu.make_async_copy(zero_sc, xs_ref.at[pl.ds(row, EXPERT_BLOCK)], sem_dead)

        n_dead = xs_ref.shape[0] // EXPERT_BLOCK - nb_ref[0]

        def start_dead(k, c):
            dead_block(pl.multiple_of((nb_ref[0] + k) * EXPERT_BLOCK, EXPERT_BLOCK)).start()
            return c
        lax.fori_loop(0, n_dead, start_dead, 0)

        def wait_dead(k, c):
            dead_block(0).wait()
            return c
        lax.fori_loop(0, n_dead, wait_dead, 0)


def _sorted_rows(tm):
    return _round_up(2 * tm + N_EXPERTS * (RUN_ALIGN - 1), LANES)


def _dispatch(plan, h_main, h_tail, ids, toff_v, rows_total):
    tiles, _, tm = ids.shape
    d = h_main.shape[1]
    tiles_main = h_main.shape[0] // tm
    assert tiles == tiles_main + 1 and h_tail.shape[0] == tm
    rt = _sorted_rows(tm)
    grid_spec = pltpu.PrefetchScalarGridSpec(
        num_scalar_prefetch=N_LIST + 4,
        grid=(tiles,),
        in_specs=[pl.BlockSpec((tm, d), lambda i, *_: (jnp.minimum(i, tiles_main - 1), 0)),
                  pl.BlockSpec((tm, d), lambda i, *_: (0, 0)),
                  pl.BlockSpec((1, 2, tm), lambda i, *_: (i, 0, 0)),
                  pl.BlockSpec((1, N_EXPERTS, LANES), lambda i, *_: (i, 0, 0))],
        out_specs=pl.BlockSpec(memory_space=pl.ANY),
        scratch_shapes=[pltpu.VMEM((2 * tm, 2 * tm), bf16), pltpu.VMEM((2, rt, d), f32),
                        pltpu.VMEM((EXPERT_BLOCK, d), f32), pltpu.SemaphoreType.DMA((len(CHUNK_ROWS) + 2,))],
    )
    return pl.pallas_call(
        _dispatch_kernel,
        grid_spec=grid_spec,
        out_shape=SDS((rows_total, d), f32),
        compiler_params=_cparams(("arbitrary",)),
        name="dispatch",
    )(*plan["lists"], plan["tail_start"], plan["tail8"], plan["tail_tot8"], plan["nb"],
      h_main, h_tail, ids, toff_v)


def _experts_kernel(seg_ref, nblk_ref, nb_ref, x_ref, wg_ref, wu_ref, wd_ref, y_ref,
                    wg_sc, wu_sc, wd_sc, xbuf, ybuf, sem_in, sem_out):
    e = pl.program_id(0)
    eb = EXPERT_BLOCK
    nblk = nblk_ref[e]
    g0 = seg_ref[e] // eb
    n_all = nb_ref[0]

    def rows(g):
        return pl.ds(pl.multiple_of(g * eb, eb), eb)

    def x_copy(g):
        slot = lax.rem(g, X_AHEAD + 1)
        return pltpu.make_async_copy(x_ref.at[rows(g)], xbuf.at[slot], sem_in.at[slot])

    def y_copy(g):
        return pltpu.make_async_copy(ybuf.at[g & 1], y_ref.at[rows(g)], sem_out.at[g & 1])

    @pl.when(e == 0)
    def _():
        for g in range(X_AHEAD):
            @pl.when(g < n_all)
            def _():
                x_copy(g).start()

    @pl.when(nblk > 0)
    def _():
        wg_sc[...] = wg_ref[0].astype(bf16)
        wu_sc[...] = wu_ref[0].astype(bf16)
        wd_sc[...] = wd_ref[0].astype(bf16)

        def block(b, c):
            g = g0 + b
            x_copy(g).wait()

            @pl.when(g + X_AHEAD < n_all)
            def _():
                x_copy(g + X_AHEAD).start()

            xb = xbuf[lax.rem(g, X_AHEAD + 1)].astype(bf16)
            a = jnp.dot(xb, wg_sc[...], preferred_element_type=f32)
            u = jnp.dot(xb, wu_sc[...], preferred_element_type=f32)
            hid = (jax.nn.silu(a) * u).astype(bf16)
            y = jnp.dot(hid, wd_sc[...], preferred_element_type=f32)

            @pl.when(g >= 2)
            def _():
                y_copy(g - 2).wait()

            ybuf[g & 1] = y
            y_copy(g).start()
            return c

        lax.fori_loop(0, nblk, block, 0)

    @pl.when(e == pl.num_programs(0) - 1)
    def _():
        @pl.when(n_all >= 2)
        def _():
            y_copy(n_all - 2).wait()

        @pl.when(n_all >= 1)
        def _():
            y_copy(n_all - 1).wait()

        def dead_block(b):
            return pltpu.make_async_copy(ybuf.at[0], y_ref.at[pl.ds(pl.multiple_of(b * eb, eb), eb)], sem_out.at[0])

        n_dead = y_ref.shape[0] // eb - nb_ref[0]
        ybuf[0] = jnp.zeros(ybuf.shape[1:], f32)

        def start_dead(k, c):
            dead_block(nb_ref[0] + k).start()
            return c
        lax.fori_loop(0, n_dead, start_dead, 0)

        def wait_dead(k, c):
            dead_block(0).wait()
            return c
        lax.fori_loop(0, n_dead, wait_dead, 0)


def _experts(plan, xs, w_gate, w_up, w_down):
    rows_total, d = xs.shape
    n_exp, _, de = w_gate.shape
    eb = EXPERT_BLOCK
    grid_spec = pltpu.PrefetchScalarGridSpec(
        num_scalar_prefetch=3,
        grid=(n_exp,),
        in_specs=[pl.BlockSpec(memory_space=pl.ANY),
                  pl.BlockSpec((1, d, de), lambda e, *_: (e, 0, 0)),
                  pl.BlockSpec((1, d, de), lambda e, *_: (e, 0, 0)),
                  pl.BlockSpec((1, de, d), lambda e, *_: (e, 0, 0))],
        out_specs=pl.BlockSpec(memory_space=pl.ANY),
        scratch_shapes=[pltpu.VMEM((d, de), bf16), pltpu.VMEM((d, de), bf16), pltpu.VMEM((de, d), bf16),
                        pltpu.VMEM((X_AHEAD + 1, eb, d), f32), pltpu.VMEM((2, eb, d), f32),
                        pltpu.SemaphoreType.DMA((X_AHEAD + 1,)), pltpu.SemaphoreType.DMA((2,))],
    )
    return pl.pallas_call(
        _experts_kernel,
        grid_spec=grid_spec,
        out_shape=SDS((rows_total, d), f32),
        compiler_params=_cparams(("arbitrary",)),
        name="experts",
    )(plan["seg_start"], plan["seg_blocks"], plan["nb"], xs, w_gate, w_up, w_down)


def _combine_kernel(*refs, final):
    lists = refs[:N_LIST]
    (xm_ref, xt_ref, gm_ref, gt_ref, gf_ref, ids_ref, gates_ref, toff_ref, y_ref,
     om_ref, ot_ref, upper_sc, ybuf_sc, sem) = refs[N_LIST:]
    i = pl.program_id(0)
    last = pl.num_programs(0) - 1
    tm = xm_ref.shape[0]
    rt = ybuf_sc.shape[1]
    slot = i & 1

    @pl.when(i == 0)
    def _():
        _fill_upper(upper_sc)
        ybuf_sc[...] = jnp.zeros(ybuf_sc.shape, f32)
        _tile_chunks(0, lists, ybuf_sc.at[0], y_ref, sem, False, wait=False)

    _tile_chunks(i, lists, ybuf_sc.at[slot], y_ref, sem, False, wait=True)

    @pl.when(i < last)
    def _():
        _tile_chunks(i + 1, lists, ybuf_sc.at[1 - slot], y_ref, sem, False, wait=False)

    pos = _sorted_positions(ids_ref, toff_ref, upper_sc, tm)
    gates = gates_ref[0]
    r_iota = lax.broadcasted_iota(i32, (rt, tm), 0).astype(f32)
    weights = (jnp.where(r_iota == pos[:, :tm], gates[0:1], 0.0)
               + jnp.where(r_iota == pos[:, tm:], gates[1:2], 0.0)).astype(bf16)
    ff = lax.dot_general(weights, ybuf_sc[slot].astype(bf16), (((0,), (0,)), ((), ())),
                         preferred_element_type=f32)

    def finish(x1, gate):
        x2 = x1 + gate * ff
        return _rms(x2, gf_ref[...]) if final else x2

    @pl.when(i < last)
    def _():
        om_ref[...] = finish(xm_ref[...], gm_ref[0])

    @pl.when(i == last)
    def _():
        ot_ref[...] = finish(xt_ref[...], gt_ref[...])


def _combine(plan, x_main, x_tail, gate_main, gate_tail, g_final, ids, gates, toff_v, y, *, final):
    tiles, _, tm = ids.shape
    d = x_main.shape[1]
    tiles_main = x_main.shape[0] // tm
    tpr = tiles_main // gate_main.shape[0]
    rt = _sorted_rows(tm)
    main_idx = lambda i, *_: (jnp.minimum(i, tiles_main - 1), 0)
    tail_spec = pl.BlockSpec((tm, d), lambda i, *_: (0, 0))
    slot_spec = pl.BlockSpec((1, 2, tm), lambda i, *_: (i, 0, 0))
    grid_spec = pltpu.PrefetchScalarGridSpec(
        num_scalar_prefetch=N_LIST,
        grid=(tiles,),
        in_specs=[pl.BlockSpec((tm, d), main_idx), tail_spec,
                  pl.BlockSpec((1, 1, d), lambda i, *_: (jnp.minimum(i, tiles_main - 1) // tpr, 0, 0)), tail_spec,
                  pl.BlockSpec((1, d), lambda i, *_: (0, 0)), slot_spec, slot_spec,
                  pl.BlockSpec((1, N_EXPERTS, LANES), lambda i, *_: (i, 0, 0)),
                  pl.BlockSpec(memory_space=pl.ANY)],
        out_specs=[pl.BlockSpec((tm, d), main_idx), tail_spec],
        scratch_shapes=[pltpu.VMEM((2 * tm, 2 * tm), bf16), pltpu.VMEM((2, rt, d), f32),
                        pltpu.SemaphoreType.DMA((len(CHUNK_ROWS),))],
    )
    return pl.pallas_call(
        functools.partial(_combine_kernel, final=final),
        grid_spec=grid_spec,
        out_shape=[SDS(x_main.shape, f32), SDS((tm, d), f32)],
        compiler_params=_cparams(("arbitrary",)),
        name="combine",
    )(*plan["lists"], x_main, x_tail, gate_main, gate_tail, g_final, ids, gates, toff_v, y)


def _moe(main, tail, g_final, w_gate, w_up, w_down, *, tm, final):
    n, t, d = main["x1"].shape
    t_tail = tail["x1"].shape[1]
    pad = tm - t_tail
    tiles = n * t // tm + 1
    rows_max = 2 * (n * t + t_tail) + tiles * N_EXPERTS * (RUN_ALIGN - 1) + N_EXPERTS * (EXPERT_BLOCK - 1)
    nb_max = -(-rows_max // EXPERT_BLOCK)
    rows2 = lambda a: jnp.pad(a.reshape(t_tail, d), ((0, pad), (0, 0)))
    ids = jnp.concatenate([main["ids"], jnp.pad(tail["ids"], ((0, 0), (0, 0), (0, pad)), constant_values=-1)])
    gates = jnp.concatenate([main["gates"], jnp.pad(tail["gates"], ((0, 0), (0, 0), (0, pad)))])
    cnt = jnp.concatenate([main["cnt"], tail["cnt"]])[:, :, 0].astype(i32)
    plan = _route_plan(cnt, nb_max, _sorted_rows(tm))
    xs = _dispatch(plan, main["h2"].reshape(n * t, d), rows2(tail["h2"]), ids, plan["toff_v"],
                   nb_max * EXPERT_BLOCK)
    y = _experts(plan, xs, w_gate, w_up, w_down)
    out_m, out_t = _combine(plan, main["x1"].reshape(n * t, d), rows2(tail["x1"]), main["gate"],
                            rows2(tail["gate"]), g_final, ids, gates, plan["toff_v"], y, final=final)
    return out_m.reshape(n, t, d), out_t[:t_tail].reshape(1, t_tail, d)


def _layer_group(x, mods, tables, attend, p, *, tm, mix, emit_vn):
    ws_t, mask, bs_b = mix
    outs = _in_proj(x, mods[0], mods[1], p["g1"], p["w_in"], tables, ws_t, mask, bs_b, p["g_v"], p["g_mlp"],
                    tm=tm, emit_vn=emit_vn)
    q, k, v, kb, vb, ob = outs[:6]
    oa = attend(q, k, v, kb, vb)
    x1, h2, ids, gates, cnt = _out_route(oa, ob, x, mods[2], mods[3], mods[4], p["w_out_a"], p["w_out_b"],
                                         p["g2"], p["wr_t"], p["br_b"], tm=tm)
    routed = dict(x1=x1, h2=h2, ids=ids, gates=gates, cnt=cnt, gate=mods[5])
    return routed, k, v, (outs[6] if emit_vn else None)


def kernel(x_prompt, x_sample, cache_k, cache_v, page_table, c_prompt, c_sample, w_ada, b_ada, g_norm1, w_in,
           lam_p, g_attn, g_v, w_s, b_s, g_mlp, w_out, g_norm2, w_router_g, b_router_g, w_router_e, b_router_e,
           w_exp_gate, w_exp_up, w_exp_down, g_final):
    nb, seq, d = x_prompt.shape
    ndec, dseq, _ = x_sample.shape
    depth = w_ada.shape[0]
    past_len = page_table.shape[1] * PAGE
    n_pool = cache_k.shape[1]
    a_w = N_HEADS * DV
    tm_s = ndec * dseq

    tables_p, tables_s8 = _rope_tables(seq, dseq, past_len)
    tables_s = tuple(jnp.tile(t, (ndec, 1)) for t in tables_s8)
    c_all = jnp.concatenate([c_prompt, c_sample], axis=0)
    c_pad = _round_up(c_all.shape[0], SUBLANES) - c_all.shape[0]
    c_all = jnp.pad(c_all, ((0, c_pad), (0, 0)))

    tri = jnp.tril(jnp.ones((CHUNK, CHUNK), f32))
    idx = jnp.arange(tm_s)
    mask_s = ((idx[:, None] // dseq == idx[None, :] // dseq) & (idx[None, :] % dseq <= idx[:, None] % dseq)).astype(f32)
    sel_s = (idx[:, None] % dseq == jnp.arange(dseq)[None, :]).astype(f32)

    xp, xs = x_prompt, x_sample.reshape(1, tm_s, d)
    kp_l, vp_l, ks_l, vs_l, cv_l = [], [], [], [], []
    for l in range(depth):
        lam_init = 0.8 - 0.6 * math.exp(-0.3 * l)
        mod = _adaln(c_all, w_ada[l], b_ada[l])
        mods_p = [mod[:nb, None, j * d:(j + 1) * d] for j in range(6)]
        mods_s = [jnp.repeat(mod[nb:nb + ndec, j * d:(j + 1) * d], dseq, axis=0)[None] for j in range(6)]
        wr_t = jnp.concatenate([w_router_g[l].T, jnp.zeros((SUBLANES - N_EXPERT_GROUPS, d), f32),
                                w_router_e[l].T], axis=0)
        br = jnp.concatenate([b_router_g[l], jnp.zeros((SUBLANES - N_EXPERT_GROUPS,), f32), b_router_e[l]])
        p = dict(
            g1=g_norm1[l].reshape(1, d), w_in=w_in[l].astype(bf16), g_v=g_v[l].reshape(1, -1),
            g_mlp=g_mlp[l].reshape(1, -1), w_out_a=w_out[l][:a_w].astype(bf16), w_out_b=w_out[l][a_w:].astype(bf16),
            g2=g_norm2[l].reshape(1, d), wr_t=wr_t, br_b=jnp.broadcast_to(br[:, None], (ROUTER_ROWS, LANES)),
            w_gate=w_exp_gate[l], w_up=w_exp_up[l], w_down=w_exp_down[l])
        gf = g_final.reshape(1, d)
        mix_p = (w_s[l], tri, jnp.broadcast_to(b_s[l][:, :, None], (N_GROUPS_B, CHUNK, C_B)))
        rep = lambda eq, *ops: jnp.einsum(eq, *ops, precision=lax.Precision.HIGHEST)
        mix_s = (rep("ia,gab,jb->gij", sel_s, w_s[l][:, :dseq, :dseq], sel_s), mask_s,
                 jnp.broadcast_to(rep("ia,ga->gi", sel_s, b_s[l][:, :dseq])[:, :, None], (N_GROUPS_B, tm_s, C_B)))

        attend_p = lambda q, k, v, kb, vb: _attn_prompt(lam_p[l], g_attn[l], q, kb, vb, lam_init)
        ck = jnp.transpose(cache_k[l], (0, 2, 3, 4, 1)).reshape(n_pool, -1, PAGE)
        cv = cache_v[l].reshape(n_pool, PAGE * N_HEADS, DV)

        def attend_s(q, k, v, kb, vb):
            o = _attn_sample(page_table, lam_p[l], g_attn[l], q.reshape(ndec, dseq, -1), k.reshape(ndec, dseq, -1),
                             v.reshape(ndec, dseq, -1), ck, cv, lam_init)
            return o.reshape(1, tm_s, -1)

        routed_p, k_p, v_p, _ = _layer_group(xp, mods_p, tables_p, attend_p, p, tm=TOKEN_TILE, mix=mix_p,
                                             emit_vn=False)
        routed_s, k_s, v_s, cv_s = _layer_group(xs, mods_s, tables_s, attend_s, p, tm=tm_s, mix=mix_s,
                                                emit_vn=True)
        xp, xs = _moe(routed_p, routed_s, gf, p["w_gate"], p["w_up"], p["w_down"], tm=TOKEN_TILE,
                      final=l == depth - 1)
        kp_l.append(k_p.reshape(nb, seq, N_HEADS, 2, DK))
        vp_l.append(v_p.reshape(nb, seq, N_HEADS, DV))
        ks_l.append(k_s.reshape(ndec, dseq, N_HEADS, 2, DK))
        vs_l.append(v_s.reshape(ndec, dseq, N_HEADS, DV))
        cv_l.append(cv_s.reshape(ndec, dseq, -1))
    return (xp, xs.reshape(ndec, dseq, d), jnp.stack(kp_l), jnp.stack(vp_l), jnp.stack(ks_l), jnp.stack(vs_l),
            jnp.stack(cv_l))
```

```python
import functools
import math

import jax
import jax.numpy as jnp
from jax import lax
from jax.experimental import pallas as pl
from jax.experimental.pallas import tpu as pltpu

f32 = jnp.float32
bf16 = jnp.bfloat16
i32 = jnp.int32
SDS = jax.ShapeDtypeStruct

N_HEADS = 4
DK = 64
DV = 2 * DK
ROT = DK // 4
ROPE_THETA = 500000.0
N_GROUPS_B = 4
C_B = 128
CHUNK = 128
N_EXPERT_GROUPS = 4
EXPERTS_PER_GROUP = 8
N_EXPERTS = N_EXPERT_GROUPS * EXPERTS_PER_GROUP
PAGE = 128
EPS = 1e-6
NEG = -1e30
LOG2E = math.log2(math.e)

LANES = 128
SUBLANES = 8
VMEM_LIMIT = 56 * 1024 * 1024
ATTN_VMEM_LIMIT = 60 * 1024 * 1024

TOKEN_TILE = 512
ATTN_TQ = 256
ATTN_TK = 256
ONES_ROWS = 16
PAGES_PER_STEP = 16
EXPERT_BLOCK = 256
X_AHEAD = 2
RUN_ALIGN = 8
CHUNK_ROWS = (32, 16, 8)
N_LIST = 3 * len(CHUNK_ROWS)
ROUTER_ROWS = 40


def _cparams(sem=None):
    return pltpu.CompilerParams(dimension_semantics=sem, vmem_limit_bytes=VMEM_LIMIT)


def _round_up(x, m):
    return (x + m - 1) // m * m


def _adaln_kernel(c_ref, w_ref, b_ref, o_ref):
    a = jax.nn.silu(c_ref[...])
    o_ref[...] = jnp.dot(a, w_ref[...], preferred_element_type=f32,
                         precision=lax.Precision.HIGHEST) + b_ref[...]


def _adaln(c_all, w_ada, b_ada):
    n, d = c_all.shape
    m = w_ada.shape[1]
    tn = 1536
    return pl.pallas_call(
        _adaln_kernel,
        grid=(m // tn,),
        in_specs=[pl.BlockSpec((n, d), lambda j: (0, 0)),
                  pl.BlockSpec((d, tn), lambda j: (0, j)),
                  pl.BlockSpec((1, tn), lambda j: (0, j))],
        out_specs=pl.BlockSpec((n, tn), lambda j: (0, j)),
        out_shape=SDS((n, m), f32),
        compiler_params=_cparams(("arbitrary",)),
        name="adaln",
    )(c_all, w_ada, b_ada.reshape(1, m))


def _rope_kernel(inv_ref, cos_ref, sin_ref, *, rows_prompt, past_len):
    shape = cos_ref.shape
    r = lax.broadcasted_iota(i32, shape, 0)
    l = lax.broadcasted_iota(i32, shape, 1)
    base = jnp.where(r < rows_prompt, r * 16, past_len + (r - rows_prompt) * 16)
    pos = base + (l >> 3)
    ang = pos.astype(f32) * inv_ref[...]
    cos_ref[...] = jnp.cos(ang)
    sin_ref[...] = jnp.sin(ang)


def _rope_tables(seq, dec_seq, past_len):
    inv = ROPE_THETA ** (-jnp.arange(0, ROT, 2, dtype=f32) / ROT)
    inv_lane = jnp.tile(inv, LANES // (ROT // 2)).reshape(1, LANES)
    rp = seq // 16
    rt = rp + SUBLANES
    cos_c, sin_c = pl.pallas_call(
        functools.partial(_rope_kernel, rows_prompt=rp, past_len=past_len),
        out_shape=(SDS((rt, LANES), f32), SDS((rt, LANES), f32)),
        name="rope_table",
    )(inv_lane)

    def expand(c8, s8):
        n = c8.shape[0]
        z8 = jnp.zeros((n, ROT // 2), f32)
        rest = DK - ROT
        cos_t = jnp.concatenate([c8, c8, jnp.ones((n, rest), f32)], axis=1)
        sin_a = jnp.concatenate([-s8, z8, jnp.zeros((n, rest), f32)], axis=1)
        sin_b = jnp.concatenate([z8, s8, jnp.zeros((n, rest), f32)], axis=1)
        return tuple(jnp.tile(t, (1, LANES // DK)) for t in (cos_t, sin_a, sin_b))

    half = ROT // 2
    prompt = expand(cos_c[:rp].reshape(seq, half), sin_c[:rp].reshape(seq, half))
    sample = expand(cos_c[rp].reshape(16, half)[:dec_seq], sin_c[rp].reshape(16, half)[:dec_seq])
    return prompt, sample


def _rms(x, g):
    return x * lax.rsqrt(jnp.mean(x * x, axis=-1, keepdims=True) + EPS) * g


def _in_proj_kernel(x_ref, sh_ref, sc_ref, g1_ref, w_ref, cos_ref, sa_ref, sb_ref,
                    ws_ref, msk_ref, bs_ref, gv_ref, gm_ref,
                    q_ref, k_ref, v_ref, kb_ref, vb_ref, ob_ref, *vn_refs, mix_rows):
    x = x_ref[0]
    tm = x.shape[0]
    h = _rms(x, g1_ref[...]) * (1.0 + sc_ref[0]) + sh_ref[0]
    z = jnp.dot(h.astype(bf16), w_ref[...], preferred_element_type=f32)
    qk_w = N_HEADS * 2 * DK
    a_w = N_HEADS * DV
    b_w = N_GROUPS_B * C_B
    cos_t, sin_a, sin_b = cos_ref[...], sa_ref[...], sb_ref[...]
    for s in range(2 * qk_w // LANES):
        zs = z[:, s * LANES:(s + 1) * LANES]
        rot = zs * cos_t + pltpu.roll(zs, LANES - ROT // 2, 1) * sin_a + pltpu.roll(zs, ROT // 2, 1) * sin_b
        if s < qk_w // LANES:
            q_ref[0, :, s * LANES:(s + 1) * LANES] = (rot * (DK ** -0.5 * LOG2E)).astype(bf16)
        else:
            k_ref[0, :, s * LANES - qk_w:(s + 1) * LANES - qk_w] = rot
            kb_ref[0, :, s * LANES - qk_w:(s + 1) * LANES - qk_w] = rot.astype(bf16)
    v = z[:, 2 * qk_w:2 * qk_w + a_w]
    for h in range(N_HEADS):
        v_ref[0, pl.ds(h, tm, stride=N_HEADS), :] = v[:, h * DV:(h + 1) * DV]
    vb_ref[0] = v.astype(bf16)
    uv = jax.nn.gelu(z[:, 2 * qk_w + a_w:])
    nblk = tm // mix_rows
    for g in range(N_GROUPS_B):
        u = uv[:, g * C_B:(g + 1) * C_B]
        vn = _rms(uv[:, b_w + g * C_B:b_w + (g + 1) * C_B], gv_ref[:, g * C_B:(g + 1) * C_B])
        if vn_refs:
            vn_refs[0][0, :, g * C_B:(g + 1) * C_B] = vn
        wm = (ws_ref[g] * msk_ref[...]).astype(bf16)
        vcat = jnp.concatenate([vn[j * mix_rows:(j + 1) * mix_rows] for j in range(nblk)], axis=1)
        mixed = jnp.dot(wm, vcat.astype(bf16), preferred_element_type=f32)
        for j in range(nblk):
            t = u[j * mix_rows:(j + 1) * mix_rows] * (mixed[:, j * C_B:(j + 1) * C_B] + bs_ref[g])
            ob_ref[0, j * mix_rows:(j + 1) * mix_rows, g * C_B:(g + 1) * C_B] = _rms(
                t, gm_ref[:, g * C_B:(g + 1) * C_B]).astype(bf16)


def _in_proj(x, shift, scale, g1, w_in_bf, tables, ws_t, mask, bs_b, g_v, g_mlp, *, tm, emit_vn):
    n, t, d = x.shape
    in_w = w_in_bf.shape[1]
    r = ws_t.shape[1]
    mod_rows = shift.shape[1]
    tab_rows = tables[0].shape[0]
    w_half = N_HEADS * DV

    def mod_spec():
        if mod_rows == 1:
            return pl.BlockSpec((1, 1, d), lambda b, i: (b, 0, 0))
        return pl.BlockSpec((1, tm, d), lambda b, i: (b, i, 0))

    def tab_spec():
        if tab_rows == tm:
            return pl.BlockSpec((tm, LANES), lambda b, i: (0, 0))
        return pl.BlockSpec((tm, LANES), lambda b, i: (i, 0))

    full = lambda shp: pl.BlockSpec(shp, lambda b, i: (0,) * len(shp))
    tok = lambda w: pl.BlockSpec((1, tm, w), lambda b, i: (b, i, 0))
    out_shape = [SDS((n, t, w_half), bf16), SDS((n, t, w_half), f32), SDS((n, t * N_HEADS, DV), f32),
                 SDS((n, t, w_half), bf16), SDS((n, t, w_half), bf16), SDS((n, t, w_half), bf16)]
    out_specs = [tok(w_half)] * 6
    out_specs[2] = pl.BlockSpec((1, tm * N_HEADS, DV), lambda b, i: (b, i, 0))
    if emit_vn:
        out_shape.append(SDS((n, t, w_half), f32))
        out_specs.append(tok(w_half))
    return pl.pallas_call(
        functools.partial(_in_proj_kernel, mix_rows=r),
        grid=(n, t // tm),
        in_specs=[tok(d), mod_spec(), mod_spec(), full((1, d)), full((d, in_w)),
                  tab_spec(), tab_spec(), tab_spec(),
                  full((N_GROUPS_B, r, r)), full((r, r)), full((N_GROUPS_B, r, C_B)),
                  full((1, w_half)), full((1, w_half))],
        out_specs=out_specs,
        out_shape=out_shape,
        compiler_params=_cparams(("arbitrary", "arbitrary")),
        name="in_proj",
    )(x, shift, scale, g1, w_in_bf, *tables, ws_t, mask, bs_b, g_v, g_mlp)


def _diff_lambda(lam_ref, lam_init):
    lp = lam_ref[...]
    s1 = jnp.sum(lp[0:1] * lp[1:2], axis=1, keepdims=True)
    s2 = jnp.sum(lp[2:3] * lp[3:4], axis=1, keepdims=True)
    return jnp.exp(s1) - jnp.exp(s2) + lam_init


def _attn_kernel(pt_ref, lam_ref, g_ref, q_ref, k_ref, v_ref, qs_ref, kn_ref, vn_ref, ck_ref, cv_ref,
                 o_ref, os_ref, vt_sc, qq_sc, acc_sc, m_sc, s_sc, qbd_sc, sm_sc, sl_sc, sacc_sc, kbuf, vbuf, sem,
                 *, lam_init):
    b = pl.program_id(0)
    n_steps = pl.num_programs(0)
    seq = q_ref.shape[1]
    tq, tk = ATTN_TQ, ATTN_TK
    n_qb = seq // tq
    lam = _diff_lambda(lam_ref, lam_init)
    heads = range(N_HEADS)
    nt = (((1,), (1,)), ((), ()))

    seq_per_step, ds, width = qs_ref.shape
    pps = kbuf.shape[1]
    groups = pt_ref.shape[1] // pps
    slots = seq_per_step * groups
    total_slots = n_steps * slots
    rows_h = 2 * ds
    rows = N_HEADS * rows_h

    assert groups & (groups - 1) == 0 and slots % 2 == 0
    g_shift = groups.bit_length() - 1

    def page_copies(g, buf):
        sq = g >> g_shift
        first_page = (g & (groups - 1)) * pps
        out = []
        for i in range(pps):
            page = pt_ref[sq, first_page + i]
            out.append(pltpu.make_async_copy(ck_ref.at[page], kbuf.at[buf, i], sem.at[buf]))
            out.append(pltpu.make_async_copy(cv_ref.at[page], vbuf.at[buf, i], sem.at[buf]))
        return out

    def wait_pages(buf):
        for i in range(pps):
            pltpu.make_async_copy(ck_ref.at[0], kbuf.at[buf, i], sem.at[buf]).wait()
            pltpu.make_async_copy(cv_ref.at[0], vbuf.at[buf, i], sem.at[buf]).wait()

    def sample_update(s, value):
        m_prev = sm_sc[...]
        m_new = jnp.maximum(m_prev, jnp.max(s, axis=1, keepdims=True))
        alpha = jnp.exp2(m_prev - m_new)
        p = jnp.exp2(s - m_new)
        sl_sc[...] = alpha * sl_sc[...] + jnp.sum(p, axis=1, keepdims=True)
        pb = p.astype(bf16)
        n_blk = s.shape[1] // PAGE
        parts = [jnp.dot(pb[h * rows_h:(h + 1) * rows_h],
                         jnp.concatenate([value(i, h) for i in range(n_blk)], axis=0), preferred_element_type=f32)
                 for h in heads]
        sacc_sc[...] = sacc_sc[...] * alpha + jnp.concatenate(parts, axis=0)
        sm_sc[...] = m_new

    def slot_begin(t):
        buf = t & 1
        wait_pages(buf)
        g_next = jnp.minimum(b * slots + t + 1, total_slots - 1)
        for cp in page_copies(g_next, 1 - buf):
            cp.start()

        @pl.when(t & (groups - 1) == 0)
        def _():
            row_i = lax.broadcasted_iota(i32, (rows, width), 0)
            col_i = lax.broadcasted_iota(i32, (rows, width), 1)
            qt = jnp.concatenate([qs_ref[t >> g_shift].astype(f32)] * (N_HEADS * 2), axis=0)
            same_map = (col_i >> (DK.bit_length() - 1)) == (row_i >> (ds.bit_length() - 1))
            qbd_sc[...] = jnp.where(same_map, qt, 0.0).astype(bf16)
            sm_sc[...] = jnp.full(sm_sc.shape, NEG, f32)
            sl_sc[...] = jnp.zeros(sl_sc.shape, f32)
            sacc_sc[...] = jnp.zeros(sacc_sc.shape, f32)

    def slot_scores(t):
        buf = t & 1
        k_all = jnp.concatenate([kbuf[buf, i].astype(bf16) for i in range(pps)], axis=1)
        return jnp.dot(qbd_sc[...], k_all, preferred_element_type=f32)

    def slot_values(t, s_past):
        buf = t & 1
        sample_update(s_past, lambda i, h: vbuf[buf, i, pl.ds(h, PAGE, stride=N_HEADS), :].astype(bf16))

    def slot_end(t):
        @pl.when(t & (groups - 1) == groups - 1)
        def _():
            sq = t >> g_shift
            qbd = qbd_sc[...]
            pad = jnp.zeros((PAGE - ds, width), f32)
            kn = jnp.concatenate([kn_ref[sq], pad], axis=0).astype(bf16)
            vn = jnp.concatenate([vn_ref[sq], pad], axis=0).astype(bf16)
            s_new = lax.dot_general(qbd, kn, nt, preferred_element_type=f32)
            key = lax.broadcasted_iota(i32, s_new.shape, 1)
            qry = lax.broadcasted_iota(i32, s_new.shape, 0) & (ds - 1)
            sample_update(jnp.where(key <= qry, s_new, NEG), lambda i, h: vn[:, h * DV:(h + 1) * DV])
            out = sacc_sc[...] * (1.0 / sl_sc[...])
            for h in heads:
                o = out[h * rows_h:h * rows_h + ds] - lam * out[h * rows_h + ds:(h + 1) * rows_h]
                os_ref[sq, :, h * DV:(h + 1) * DV] = _rms(o, g_ref[h:h + 1, :] * (1.0 - lam_init)).astype(bf16)

    @pl.when(b == 0)
    def _():
        for cp in page_copies(0, 0):
            cp.start()

    def prep(j, c):
        sl = pl.ds(pl.multiple_of(j * tk, tk), tk)
        for h in heads:
            vt_sc[h, j, :DV] = v_ref[0, sl, h * DV:(h + 1) * DV].T
            vt_sc[h, j, DV:] = jnp.ones((ONES_ROWS, tk), bf16)
        return c

    lax.fori_loop(0, seq // tk, prep, 0)
    lane = lax.broadcasted_iota(i32, (tq, DV), 1)

    def q_block(qi, t, slot_in_diag):
        qsl = pl.ds(pl.multiple_of(qi * tq, tq), tq)
        for h in heads:
            qb = q_ref[0, qsl, h * DV:(h + 1) * DV]
            zero = jnp.zeros_like(qb)
            qq_sc[h, :tq] = jnp.where(lane < DK, qb, zero)
            qq_sc[h, tq:] = jnp.where(lane >= DK, qb, zero)
            m_sc[h] = jnp.full(m_sc.shape[1:], NEG, f32)
            acc_sc[h] = jnp.zeros(acc_sc.shape[1:], f32)

        def scores(j, h):
            ksl = pl.ds(pl.multiple_of(j * tk, tk), tk)
            return lax.dot_general(k_ref[0, ksl, h * DV:(h + 1) * DV], qq_sc[h], nt, preferred_element_type=f32)

        s_sc[...] = scores(0, 0)

        def kv_block(j, masked, between=None):
            s_next = None
            extra = None
            for h in heads:
                s = s_sc[...] if h == 0 else s_next
                if h + 1 < N_HEADS:
                    s_next = scores(j, h + 1)
                elif not masked:
                    s_sc[...] = scores(j + 1, 0)
                if between is not None and h == N_HEADS // 2:
                    extra = between()
                if masked:
                    key_i = lax.broadcasted_iota(i32, s.shape, 0)
                    qry_i = lax.broadcasted_iota(i32, s.shape, 1) & (tq - 1)
                    s = jnp.where(key_i <= qry_i, s, NEG)
                m_prev = m_sc[h]
                m_new = jnp.maximum(m_prev, jnp.max(s, axis=0, keepdims=True))
                alpha = jnp.exp2(m_prev - m_new)
                p = jnp.exp2(s - m_new)
                pv = jnp.dot(vt_sc[h, j], p.astype(bf16), preferred_element_type=f32)
                acc_sc[h] = acc_sc[h] * alpha + pv
                m_sc[h] = m_new
            return extra

        def full_block(j, cc):
            kv_block(j, False)
            return cc

        def full_pair(jj, tt):
            slot_begin(tt)
            s_past = kv_block(2 * jj, False, lambda: slot_scores(tt))
            kv_block(2 * jj + 1, False, lambda: slot_values(tt, s_past))
            slot_end(tt)
            return tt + 1

        pairs = qi >> 1
        t = lax.fori_loop(0, pairs, full_pair, t)
        lax.fori_loop(2 * pairs, qi, full_block, 0)
        if slot_in_diag:
            slot_begin(t)
            s_past = slot_scores(t)
            kv_block(qi, True, lambda: slot_values(t, s_past))
        else:
            kv_block(qi, True)
        for h in heads:
            acc = acc_sc[h]
            inv_l = 1.0 / acc[DV:DV + 1, :]
            o_t = acc[:DV, :tq] * inv_l[:, :tq] - lam * (acc[:DV, tq:] * inv_l[:, tq:])
            o_ref[0, qsl, h * DV:(h + 1) * DV] = _rms(o_t.T, g_ref[h:h + 1, :] * (1.0 - lam_init)).astype(bf16)
        if slot_in_diag:
            slot_end(t)
            t = t + 1
        return t

    pair_slots = sum(qi >> 1 for qi in range(n_qb))
    diag_slots = slots - pair_slots
    assert 0 <= diag_slots <= n_qb, (slots, pair_slots)
    t = lax.fori_loop(0, diag_slots, lambda qi, tt: q_block(qi, tt, True), 0)
    lax.fori_loop(diag_slots, n_qb, lambda qi, tt: q_block(qi, tt, False), t)

    @pl.when(b == n_steps - 1)
    def _():
        wait_pages(slots & 1)


def _attention(page_table, lam_p, g_attn, q, k, v, q_s, k_new, v_new, cache_kt, cache_vr, lam_init):
    n, seq, w = q.shape
    nseq, ds, _ = q_s.shape
    assert nseq % n == 0 and ds & (ds - 1) == 0 and page_table.shape[1] % PAGES_PER_STEP == 0
    sps = nseq // n
    rows = N_HEADS * 2 * ds
    blk = pl.BlockSpec((1, seq, w), lambda b, pt: (b, 0, 0))
    blk_in = pl.BlockSpec((1, seq, w), lambda b, pt: (b, 0, 0), pipeline_mode=pl.Buffered(1))
    sblk = pl.BlockSpec((sps, ds, w), lambda b, pt: (b, 0, 0))
    full = lambda a: pl.BlockSpec(a.shape, lambda b, pt: (0,) * a.ndim)
    any_spec = pl.BlockSpec(memory_space=pl.ANY)
    grid_spec = pltpu.PrefetchScalarGridSpec(
        num_scalar_prefetch=1,
        grid=(n,),
        in_specs=[full(lam_p), full(g_attn), blk_in, blk_in, blk_in, sblk, sblk, sblk, any_spec, any_spec],
        out_specs=[blk, sblk],
        scratch_shapes=[pltpu.VMEM((N_HEADS, seq // ATTN_TK, DV + ONES_ROWS, ATTN_TK), bf16),
                        pltpu.VMEM((N_HEADS, 2 * ATTN_TQ, DV), bf16),
                        pltpu.VMEM((N_HEADS, DV + ONES_ROWS, 2 * ATTN_TQ), f32),
                        pltpu.VMEM((N_HEADS, 1, 2 * ATTN_TQ), f32),
                        pltpu.VMEM((ATTN_TK, 2 * ATTN_TQ), f32),
                        pltpu.VMEM((rows, w), bf16), pltpu.VMEM((rows, 1), f32), pltpu.VMEM((rows, 1), f32),
                        pltpu.VMEM((rows, DV), f32),
                        pltpu.VMEM((2, PAGES_PER_STEP) + cache_kt.shape[1:], f32),
                        pltpu.VMEM((2, PAGES_PER_STEP) + cache_vr.shape[1:], f32),
                        pltpu.SemaphoreType.DMA((2,))],
    )
    return pl.pallas_call(
        functools.partial(_attn_kernel, lam_init=lam_init),
        grid_spec=grid_spec,
        out_shape=[SDS((n, seq, w), bf16), SDS((nseq, ds, w), bf16)],
        compiler_params=pltpu.CompilerParams(dimension_semantics=("arbitrary",), vmem_limit_bytes=ATTN_VMEM_LIMIT),
        name="attention",
    )(page_table, lam_p, g_attn, q, k, v, q_s, k_new, v_new, cache_kt, cache_vr)


def _out_route_kernel(oa_ref, ob_ref, x_ref, g1_ref, sh_ref, sc_ref, wa_ref, wb_ref, g2_ref, wr_ref, br_ref,
                      x1_ref, h2_ref, ids_ref, gates_ref, cnt_ref):
    mix = (jnp.dot(oa_ref[0], wa_ref[...], preferred_element_type=f32)
           + jnp.dot(ob_ref[0], wb_ref[...], preferred_element_type=f32))
    x1 = x_ref[0] + g1_ref[0] * mix
    x1_ref[0] = x1
    h2 = _rms(x1, g2_ref[...]) * (1.0 + sc_ref[0]) + sh_ref[0]
    h2_ref[0] = h2.astype(bf16)
    tm = h2.shape[0]
    wr = wr_ref[...]
    wr_hi = wr.astype(bf16)
    wr_lo = (wr - wr_hi.astype(f32)).astype(bf16)
    h_hi = h2.astype(bf16)
    h_lo = (h2 - h_hi.astype(f32)).astype(bf16)
    nt = (((1,), (1,)), ((), ()))
    r1 = lax.dot_general(jnp.concatenate([wr_hi, wr_lo], axis=0), h_hi, nt, preferred_element_type=f32)
    r2 = lax.dot_general(wr_hi, h_lo, nt, preferred_element_type=f32)
    lg = r1[:ROUTER_ROWS] + r1[ROUTER_ROWS:] + r2 + br_ref[:, 0:1]
    row = lax.broadcasted_iota(i32, (SUBLANES, tm), 0).astype(f32)
    big = float(SUBLANES)
    gl = jnp.where(row < N_EXPERT_GROUPS, lg[0:SUBLANES], NEG)
    gmax = jnp.max(gl, axis=0, keepdims=True)
    g_p = 1.0 / jnp.sum(jnp.exp(gl - gmax), axis=0, keepdims=True)
    gidx = jnp.min(jnp.where(gl == gmax, row, big), axis=0, keepdims=True)
    esel = jnp.zeros((SUBLANES, tm), f32)
    for g in range(N_EXPERT_GROUPS):
        esel = jnp.where(gidx == float(g), lg[SUBLANES * (g + 1):SUBLANES * (g + 2)], esel)
    e1 = jnp.max(esel, axis=0, keepdims=True)
    i1 = jnp.min(jnp.where(esel == e1, row, big), axis=0, keepdims=True)
    esel2 = jnp.where(row == i1, -jnp.inf, esel)
    e2 = jnp.max(esel2, axis=0, keepdims=True)
    i2 = jnp.min(jnp.where(esel2 == e2, row, big), axis=0, keepdims=True)
    t = jnp.exp(e2 - e1)
    w1 = g_p / (1.0 + t)
    ids = jnp.concatenate([gidx * EXPERTS_PER_GROUP + i1, gidx * EXPERTS_PER_GROUP + i2], axis=0).astype(i32)
    ids_ref[0] = ids
    gates_ref[0] = jnp.concatenate([w1, w1 * t], axis=0)
    e_iota = lax.broadcasted_iota(i32, (N_EXPERTS, tm), 0)
    onehot = jnp.where((e_iota == ids[0:1]) | (e_iota == ids[1:2]), 1.0, 0.0).astype(bf16)
    cnt_ref[0] = jnp.dot(onehot, jnp.ones((tm, LANES), bf16), preferred_element_type=f32)


def _out_route(oa, ob, x, gate1, shift2, scale2, w_a, w_b, g2, wr_t, br_b, *, tm):
    n, t, d = x.shape
    w_half = oa.shape[2]
    mod_rows = gate1.shape[1]
    tpn = t // tm

    def mod_spec():
        if mod_rows == 1:
            return pl.BlockSpec((1, 1, d), lambda b, i: (b, 0, 0))
        return pl.BlockSpec((1, tm, d), lambda b, i: (b, i, 0))

    full = lambda shp: pl.BlockSpec(shp, lambda b, i: (0,) * len(shp))
    tok = lambda w: pl.BlockSpec((1, tm, w), lambda b, i: (b, i, 0))
    return pl.pallas_call(
        _out_route_kernel,
        grid=(n, tpn),
        in_specs=[tok(w_half), tok(w_half), tok(d), mod_spec(), mod_spec(), mod_spec(),
                  full((w_half, d)), full((w_half, d)), full((1, d)),
                  full((ROUTER_ROWS, d)), full((ROUTER_ROWS, LANES))],
        out_specs=[tok(d), tok(d),
                   pl.BlockSpec((1, 2, tm), lambda b, i: (b * tpn + i, 0, 0)),
                   pl.BlockSpec((1, 2, tm), lambda b, i: (b * tpn + i, 0, 0)),
                   pl.BlockSpec((1, N_EXPERTS, LANES), lambda b, i: (b * tpn + i, 0, 0))],
        out_shape=[SDS((n, t, d), f32), SDS((n, t, d), bf16), SDS((n * tpn, 2, tm), i32),
                   SDS((n * tpn, 2, tm), f32), SDS((n * tpn, N_EXPERTS, LANES), f32)],
        compiler_params=_cparams(("arbitrary", "arbitrary")),
        name="out_route",
    )(oa, ob, x, gate1, shift2, scale2, w_a, w_b, g2, wr_t, br_b)


def _chunk_list(counts, src0, dst0, rows, kmax):
    cin = jnp.cumsum(counts, axis=1)
    cex = cin - counts
    k = jnp.arange(kmax, dtype=i32)
    owner = jnp.sum((cin[:, None, :] <= k[None, :, None]).astype(i32), axis=2)
    owner = jnp.minimum(owner, N_EXPERTS - 1)
    pick = (owner[:, :, None] == jnp.arange(N_EXPERTS, dtype=i32)).astype(i32)
    at_owner = lambda v: jnp.sum(pick * v[:, None, :], axis=2)
    within = (k[None, :] - at_owner(cex)) * rows
    src = at_owner(src0) + within
    dst = at_owner(dst0) + within
    return src.reshape(-1).astype(i32), dst.reshape(-1).astype(i32), cin[:, -1].astype(i32)


def _route_plan(cnt, nb_max, sorted_rows):
    pc = _round_up(cnt, RUN_ALIGN)
    toff = jnp.cumsum(pc, axis=1) - pc
    tot = jnp.sum(pc, axis=0)
    seg = _round_up(tot, EXPERT_BLOCK)
    seg_end = jnp.cumsum(seg)
    seg_start = seg_end - seg
    run_start = seg_start[None, :] + jnp.cumsum(pc, axis=0) - pc
    nb = seg_end[-1] // EXPERT_BLOCK
    lists = []
    done = jnp.zeros_like(pc)
    for rows in CHUNK_ROWS:
        n = (pc - done) // rows
        kmax = sorted_rows // rows if rows == CHUNK_ROWS[0] else N_EXPERTS
        lists.extend(_chunk_list(n, toff + done, run_start + done, rows, kmax))
        done = done + n * rows
    return dict(
        lists=lists,
        tail_start=(seg_start + tot).astype(i32), tail8=((seg - tot) // RUN_ALIGN).astype(i32),
        tail_tot8=(jnp.sum(seg - tot) // RUN_ALIGN).reshape(1).astype(i32),
        seg_start=seg_start.astype(i32), seg_blocks=(seg // EXPERT_BLOCK).astype(i32),
        nb=nb.reshape(1).astype(i32),
        toff_v=jnp.broadcast_to(toff.astype(f32)[:, :, None], toff.shape + (LANES,)))


def _sorted_positions(ids_ref, toff_ref, upper_ref, tm):
    ids = ids_ref[0]
    idc = jnp.concatenate([ids[0:1], ids[1:2]], axis=1)
    e_iota = lax.broadcasted_iota(i32, (N_EXPERTS, 2 * tm), 0)
    onehot = jnp.where(e_iota == idc, 1.0, 0.0)
    before = jnp.dot(onehot.astype(bf16), upper_ref[...], preferred_element_type=f32)
    pos = jnp.sum(onehot * (before + toff_ref[0][:, 0:1]), axis=0, keepdims=True)
    return jnp.where(idc >= 0, pos, -1.0)


def _fill_upper(upper_ref):
    n = upper_ref.shape[0]
    a_src = lax.broadcasted_iota(i32, (n, n), 0)
    a_dst = lax.broadcasted_iota(i32, (n, n), 1)
    upper_ref[...] = jnp.where(a_src < a_dst, 1.0, 0.0).astype(bf16)


def _aligned(row):
    return row if isinstance(row, int) else pl.multiple_of(row, RUN_ALIGN)


def _run_chunk_copy(vmem_buf, hbm_buf, sem, vrow, hrow, to_hbm, rows=RUN_ALIGN):
    v = vmem_buf.at[pl.ds(_aligned(vrow), rows)]
    h = hbm_buf.at[pl.ds(_aligned(hrow), rows)]
    return pltpu.make_async_copy(v, h, sem) if to_hbm else pltpu.make_async_copy(h, v, sem)


def _tile_chunks(i, list_refs, vmem_buf, hbm_buf, sems, to_hbm, wait):
    for c, rows in enumerate(CHUNK_ROWS):
        src_ref, dst_ref, n_ref = list_refs[3 * c:3 * c + 3]
        stride = src_ref.shape[0] // n_ref.shape[0]

        def one(k, carry, rows=rows, src_ref=src_ref, dst_ref=dst_ref, stride=stride, sem=sems.at[c]):
            if wait:
                _run_chunk_copy(vmem_buf, hbm_buf, sem, 0, 0, to_hbm, rows).wait()
            else:
                _run_chunk_copy(vmem_buf, hbm_buf, sem, src_ref[i * stride + k], dst_ref[i * stride + k],
                                to_hbm, rows).start()
            return carry

        lax.fori_loop(0, n_ref[i], one, 0)


def _dispatch_kernel(*refs):
    lists = refs[:N_LIST]
    tstart_ref, tail8_ref, ttot_ref, nb_ref = refs[N_LIST:N_LIST + 4]
    hm_ref, ht_ref, ids_ref, toff_ref, xs_ref, upper_sc, xbuf_sc, zero_sc, sem = refs[N_LIST + 4:]
    i = pl.program_id(0)
    last = pl.num_programs(0) - 1
    tm = hm_ref.shape[0]
    rt = xbuf_sc.shape[1]
    slot = i & 1
    sem_tail, sem_dead = sem.at[len(CHUNK_ROWS)], sem.at[len(CHUNK_ROWS) + 1]

    @pl.when(i == 0)
    def _():
        _fill_upper(upper_sc)
        zero_sc[...] = jnp.zeros(zero_sc.shape, f32)

    h = jnp.where(i == last, ht_ref[...], hm_ref[...])
    pos = _sorted_positions(ids_ref, toff_ref, upper_sc, tm)
    r_iota = lax.broadcasted_iota(i32, (rt, tm), 0).astype(f32)
    perm = jnp.where((r_iota == pos[:, :tm]) | (r_iota == pos[:, tm:]), 1.0, 0.0).astype(bf16)
    xbuf_sc[slot] = jnp.dot(perm, h, preferred_element_type=f32)

    @pl.when(i > 0)
    def _():
        _tile_chunks(i - 1, lists, xbuf_sc.at[1 - slot], xs_ref, sem, True, wait=True)

    _tile_chunks(i, lists, xbuf_sc.at[slot], xs_ref, sem, True, wait=False)

    @pl.when(i == last)
    def _():
        _tile_chunks(i, lists, xbuf_sc.at[slot], xs_ref, sem, True, wait=True)

        def per_expert(e, c):
            def per_chunk(k, cc):
                _run_chunk_copy(zero_sc, xs_ref, sem_tail, 0, tstart_ref[e] + k * RUN_ALIGN, True).start()
                return cc
            lax.fori_loop(0, tail8_ref[e], per_chunk, 0)
            return c
        lax.fori_loop(0, N_EXPERTS, per_expert, 0)

        def wait_one(k, c):
            _run_chunk_copy(zero_sc, xs_ref, sem_tail, 0, 0, True).wait()
            return c
        lax.fori_loop(0, ttot_ref[0], wait_one, 0)

        def dead_block(row):
            return pltpu.make_async_copy(zero_sc, xs_ref.at[pl.ds(row, EXPERT_BLOCK)], sem_dead)

        n_dead = xs_ref.shape[0] // EXPERT_BLOCK - nb_ref[0]

        def start_dead(k, c):
            dead_block(pl.multiple_of((nb_ref[0] + k) * EXPERT_BLOCK, EXPERT_BLOCK)).start()
            return c
        lax.fori_loop(0, n_dead, start_dead, 0)

        def wait_dead(k, c):
            dead_block(0).wait()
            return c
        lax.fori_loop(0, n_dead, wait_dead, 0)


def _sorted_rows(tm):
    return _round_up(2 * tm + N_EXPERTS * (RUN_ALIGN - 1), LANES)


def _dispatch(plan, h_main, h_tail, ids, toff_v, rows_total):
    tiles, _, tm = ids.shape
    d = h_main.shape[1]
    tiles_main = h_main.shape[0] // tm
    assert tiles == tiles_main + 1 and h_tail.shape[0] == tm
    rt = _sorted_rows(tm)
    grid_spec = pltpu.PrefetchScalarGridSpec(
        num_scalar_prefetch=N_LIST + 4,
        grid=(tiles,),
        in_specs=[pl.BlockSpec((tm, d), lambda i, *_: (jnp.minimum(i, tiles_main - 1), 0)),
                  pl.BlockSpec((tm, d), lambda i, *_: (0, 0)),
                  pl.BlockSpec((1, 2, tm), lambda i, *_: (i, 0, 0)),
                  pl.BlockSpec((1, N_EXPERTS, LANES), lambda i, *_: (i, 0, 0))],
        out_specs=pl.BlockSpec(memory_space=pl.ANY),
        scratch_shapes=[pltpu.VMEM((2 * tm, 2 * tm), bf16), pltpu.VMEM((2, rt, d), f32),
                        pltpu.VMEM((EXPERT_BLOCK, d), f32), pltpu.SemaphoreType.DMA((len(CHUNK_ROWS) + 2,))],
    )
    return pl.pallas_call(
        _dispatch_kernel,
        grid_spec=grid_spec,
        out_shape=SDS((rows_total, d), f32),
        compiler_params=_cparams(("arbitrary",)),
        name="dispatch",
    )(*plan["lists"], plan["tail_start"], plan["tail8"], plan["tail_tot8"], plan["nb"],
      h_main, h_tail, ids, toff_v)


def _experts_kernel(seg_ref, nblk_ref, nb_ref, x_ref, wg_ref, wu_ref, wd_ref, y_ref,
                    wg_sc, wu_sc, wd_sc, xbuf, ybuf, sem_in, sem_out):
    e = pl.program_id(0)
    eb = EXPERT_BLOCK
    nblk = nblk_ref[e]
    g0 = seg_ref[e] // eb
    n_all = nb_ref[0]

    def rows(g):
        return pl.ds(pl.multiple_of(g * eb, eb), eb)

    def x_copy(g):
        slot = lax.rem(g, X_AHEAD + 1)
        return pltpu.make_async_copy(x_ref.at[rows(g)], xbuf.at[slot], sem_in.at[slot])

    def y_copy(g):
        return pltpu.make_async_copy(ybuf.at[g & 1], y_ref.at[rows(g)], sem_out.at[g & 1])

    @pl.when(e == 0)
    def _():
        for g in range(X_AHEAD):
            @pl.when(g < n_all)
            def _():
                x_copy(g).start()

    @pl.when(nblk > 0)
    def _():
        wg_sc[...] = wg_ref[0].astype(bf16)
        wu_sc[...] = wu_ref[0].astype(bf16)
        wd_sc[...] = wd_ref[0].astype(bf16)

        def block(b, c):
            g = g0 + b
            x_copy(g).wait()

            @pl.when(g + X_AHEAD < n_all)
            def _():
                x_copy(g + X_AHEAD).start()

            xb = xbuf[lax.rem(g, X_AHEAD + 1)].astype(bf16)
            a = jnp.dot(xb, wg_sc[...], preferred_element_type=f32)
            u = jnp.dot(xb, wu_sc[...], preferred_element_type=f32)
            hid = (jax.nn.silu(a) * u).astype(bf16)
            y = jnp.dot(hid, wd_sc[...], preferred_element_type=f32)

            @pl.when(g >= 2)
            def _():
                y_copy(g - 2).wait()

            ybuf[g & 1] = y
            y_copy(g).start()
            return c

        lax.fori_loop(0, nblk, block, 0)

    @pl.when(e == pl.num_programs(0) - 1)
    def _():
        @pl.when(n_all >= 2)
        def _():
            y_copy(n_all - 2).wait()

        @pl.when(n_all >= 1)
        def _():
            y_copy(n_all - 1).wait()

        def dead_block(b):
            return pltpu.make_async_copy(ybuf.at[0], y_ref.at[pl.ds(pl.multiple_of(b * eb, eb), eb)], sem_out.at[0])

        n_dead = y_ref.shape[0] // eb - nb_ref[0]
        ybuf[0] = jnp.zeros(ybuf.shape[1:], f32)

        def start_dead(k, c):
            dead_block(nb_ref[0] + k).start()
            return c
        lax.fori_loop(0, n_dead, start_dead, 0)

        def wait_dead(k, c):
            dead_block(0).wait()
            return c
        lax.fori_loop(0, n_dead, wait_dead, 0)


def _experts(plan, xs, w_gate, w_up, w_down):
    rows_total, d = xs.shape
    n_exp, _, de = w_gate.shape
    eb = EXPERT_BLOCK
    grid_spec = pltpu.PrefetchScalarGridSpec(
        num_scalar_prefetch=3,
        grid=(n_exp,),
        in_specs=[pl.BlockSpec(memory_space=pl.ANY),
                  pl.BlockSpec((1, d, de), lambda e, *_: (e, 0, 0)),
                  pl.BlockSpec((1, d, de), lambda e, *_: (e, 0, 0)),
                  pl.BlockSpec((1, de, d), lambda e, *_: (e, 0, 0))],
        out_specs=pl.BlockSpec(memory_space=pl.ANY),
        scratch_shapes=[pltpu.VMEM((d, de), bf16), pltpu.VMEM((d, de), bf16), pltpu.VMEM((de, d), bf16),
                        pltpu.VMEM((X_AHEAD + 1, eb, d), f32), pltpu.VMEM((2, eb, d), f32),
                        pltpu.SemaphoreType.DMA((X_AHEAD + 1,)), pltpu.SemaphoreType.DMA((2,))],
    )
    return pl.pallas_call(
        _experts_kernel,
        grid_spec=grid_spec,
        out_shape=SDS((rows_total, d), f32),
        compiler_params=_cparams(("arbitrary",)),
        name="experts",
    )(plan["seg_start"], plan["seg_blocks"], plan["nb"], xs, w_gate, w_up, w_down)


def _combine_kernel(*refs, final):
    lists = refs[:N_LIST]
    (xm_ref, xt_ref, gm_ref, gt_ref, gf_ref, ids_ref, gates_ref, toff_ref, y_ref,
     om_ref, ot_ref, upper_sc, ybuf_sc, sem) = refs[N_LIST:]
    i = pl.program_id(0)
    last = pl.num_programs(0) - 1
    tm = xm_ref.shape[0]
    rt = ybuf_sc.shape[1]
    slot = i & 1

    @pl.when(i == 0)
    def _():
        _fill_upper(upper_sc)
        ybuf_sc[...] = jnp.zeros(ybuf_sc.shape, f32)
        _tile_chunks(0, lists, ybuf_sc.at[0], y_ref, sem, False, wait=False)

    _tile_chunks(i, lists, ybuf_sc.at[slot], y_ref, sem, False, wait=True)

    @pl.when(i < last)
    def _():
        _tile_chunks(i + 1, lists, ybuf_sc.at[1 - slot], y_ref, sem, False, wait=False)

    pos = _sorted_positions(ids_ref, toff_ref, upper_sc, tm)
    gates = gates_ref[0]
    r_iota = lax.broadcasted_iota(i32, (rt, tm), 0).astype(f32)
    weights = (jnp.where(r_iota == pos[:, :tm], gates[0:1], 0.0)
               + jnp.where(r_iota == pos[:, tm:], gates[1:2], 0.0)).astype(bf16)
    ff = lax.dot_general(weights, ybuf_sc[slot].astype(bf16), (((0,), (0,)), ((), ())),
                         preferred_element_type=f32)

    def finish(x1, gate):
        x2 = x1 + gate * ff
        return _rms(x2, gf_ref[...]) if final else x2

    @pl.when(i < last)
    def _():
        om_ref[...] = finish(xm_ref[...], gm_ref[0])

    @pl.when(i == last)
    def _():
        ot_ref[...] = finish(xt_ref[...], gt_ref[...])


def _combine(plan, x_main, x_tail, gate_main, gate_tail, g_final, ids, gates, toff_v, y, *, final):
    tiles, _, tm = ids.shape
    d = x_main.shape[1]
    tiles_main = x_main.shape[0] // tm
    tpr = tiles_main // gate_main.shape[0]
    rt = _sorted_rows(tm)
    main_idx = lambda i, *_: (jnp.minimum(i, tiles_main - 1), 0)
    tail_spec = pl.BlockSpec((tm, d), lambda i, *_: (0, 0))
    slot_spec = pl.BlockSpec((1, 2, tm), lambda i, *_: (i, 0, 0))
    grid_spec = pltpu.PrefetchScalarGridSpec(
        num_scalar_prefetch=N_LIST,
        grid=(tiles,),
        in_specs=[pl.BlockSpec((tm, d), main_idx), tail_spec,
                  pl.BlockSpec((1, 1, d), lambda i, *_: (jnp.minimum(i, tiles_main - 1) // tpr, 0, 0)), tail_spec,
                  pl.BlockSpec((1, d), lambda i, *_: (0, 0)), slot_spec, slot_spec,
                  pl.BlockSpec((1, N_EXPERTS, LANES), lambda i, *_: (i, 0, 0)),
                  pl.BlockSpec(memory_space=pl.ANY)],
        out_specs=[pl.BlockSpec((tm, d), main_idx), tail_spec],
        scratch_shapes=[pltpu.VMEM((2 * tm, 2 * tm), bf16), pltpu.VMEM((2, rt, d), f32),
                        pltpu.SemaphoreType.DMA((len(CHUNK_ROWS),))],
    )
    return pl.pallas_call(
        functools.partial(_combine_kernel, final=final),
        grid_spec=grid_spec,
        out_shape=[SDS(x_main.shape, f32), SDS((tm, d), f32)],
        compiler_params=_cparams(("arbitrary",)),
        name="combine",
    )(*plan["lists"], x_main, x_tail, gate_main, gate_tail, g_final, ids, gates, toff_v, y)


def _moe(main, tail, g_final, w_gate, w_up, w_down, *, tm, final):
    n, t, d = main["x1"].shape
    t_tail = tail["x1"].shape[1]
    pad = tm - t_tail
    tiles = n * t // tm + 1
    rows_max = 2 * (n * t + t_tail) + tiles * N_EXPERTS * (RUN_ALIGN - 1) + N_EXPERTS * (EXPERT_BLOCK - 1)
    nb_max = -(-rows_max // EXPERT_BLOCK)
    rows2 = lambda a: jnp.pad(a.reshape(t_tail, d), ((0, pad), (0, 0)))
    ids = jnp.concatenate([main["ids"], jnp.pad(tail["ids"], ((0, 0), (0, 0), (0, pad)), constant_values=-1)])
    gates = jnp.concatenate([main["gates"], jnp.pad(tail["gates"], ((0, 0), (0, 0), (0, pad)))])
    cnt = jnp.concatenate([main["cnt"], tail["cnt"]])[:, :, 0].astype(i32)
    plan = _route_plan(cnt, nb_max, _sorted_rows(tm))
    xs = _dispatch(plan, main["h2"].reshape(n * t, d), rows2(tail["h2"]), ids, plan["toff_v"],
                   nb_max * EXPERT_BLOCK)
    y = _experts(plan, xs, w_gate, w_up, w_down)
    out_m, out_t = _combine(plan, main["x1"].reshape(n * t, d), rows2(tail["x1"]), main["gate"],
                            rows2(tail["gate"]), g_final, ids, gates, plan["toff_v"], y, final=final)
    return out_m.reshape(n, t, d), out_t[:t_tail].reshape(1, t_tail, d)


def _before_attention(x, mods, tables, p, *, tm, mix, emit_vn):
    ws_t, mask, bs_b = mix
    return _in_proj(x, mods[0], mods[1], p["g1"], p["w_in"], tables, ws_t, mask, bs_b, p["g_v"], p["g_mlp"],
                    tm=tm, emit_vn=emit_vn)


def _after_attention(x, oa, ob, mods, p, *, tm):
    x1, h2, ids, gates, cnt = _out_route(oa, ob, x, mods[2], mods[3], mods[4], p["w_out_a"], p["w_out_b"],
                                         p["g2"], p["wr_t"], p["br_b"], tm=tm)
    return dict(x1=x1, h2=h2, ids=ids, gates=gates, cnt=cnt, gate=mods[5])


def kernel(x_prompt, x_sample, cache_k, cache_v, page_table, c_prompt, c_sample, w_ada, b_ada, g_norm1, w_in,
           lam_p, g_attn, g_v, w_s, b_s, g_mlp, w_out, g_norm2, w_router_g, b_router_g, w_router_e, b_router_e,
           w_exp_gate, w_exp_up, w_exp_down, g_final):
    nb, seq, d = x_prompt.shape
    ndec, dseq, _ = x_sample.shape
    depth = w_ada.shape[0]
    past_len = page_table.shape[1] * PAGE
    n_pool = cache_k.shape[1]
    a_w = N_HEADS * DV
    tm_s = ndec * dseq

    tables_p, tables_s8 = _rope_tables(seq, dseq, past_len)
    tables_s = tuple(jnp.tile(t, (ndec, 1)) for t in tables_s8)
    c_all = jnp.concatenate([c_prompt, c_sample], axis=0)
    c_pad = _round_up(c_all.shape[0], SUBLANES) - c_all.shape[0]
    c_all = jnp.pad(c_all, ((0, c_pad), (0, 0)))

    tri = jnp.tril(jnp.ones((CHUNK, CHUNK), f32))
    idx = jnp.arange(tm_s)
    mask_s = ((idx[:, None] // dseq == idx[None, :] // dseq) & (idx[None, :] % dseq <= idx[:, None] % dseq)).astype(f32)
    sel_s = (idx[:, None] % dseq == jnp.arange(dseq)[None, :]).astype(f32)

    xp, xs = x_prompt, x_sample.reshape(1, tm_s, d)
    kp_l, vp_l, ks_l, vs_l, cv_l = [], [], [], [], []
    for l in range(depth):
        lam_init = 0.8 - 0.6 * math.exp(-0.3 * l)
        mod = _adaln(c_all, w_ada[l], b_ada[l])
        mods_p = [mod[:nb, None, j * d:(j + 1) * d] for j in range(6)]
        mods_s = [jnp.repeat(mod[nb:nb + ndec, j * d:(j + 1) * d], dseq, axis=0)[None] for j in range(6)]
        wr_t = jnp.concatenate([w_router_g[l].T, jnp.zeros((SUBLANES - N_EXPERT_GROUPS, d), f32),
                                w_router_e[l].T], axis=0)
        br = jnp.concatenate([b_router_g[l], jnp.zeros((SUBLANES - N_EXPERT_GROUPS,), f32), b_router_e[l]])
        p = dict(
            g1=g_norm1[l].reshape(1, d), w_in=w_in[l].astype(bf16), g_v=g_v[l].reshape(1, -1),
            g_mlp=g_mlp[l].reshape(1, -1), w_out_a=w_out[l][:a_w].astype(bf16), w_out_b=w_out[l][a_w:].astype(bf16),
            g2=g_norm2[l].reshape(1, d), wr_t=wr_t, br_b=jnp.broadcast_to(br[:, None], (ROUTER_ROWS, LANES)),
            w_gate=w_exp_gate[l], w_up=w_exp_up[l], w_down=w_exp_down[l])
        gf = g_final.reshape(1, d)
        mix_p = (w_s[l], tri, jnp.broadcast_to(b_s[l][:, :, None], (N_GROUPS_B, CHUNK, C_B)))
        rep = lambda eq, *ops: jnp.einsum(eq, *ops, precision=lax.Precision.HIGHEST)
        mix_s = (rep("ia,gab,jb->gij", sel_s, w_s[l][:, :dseq, :dseq], sel_s), mask_s,
                 jnp.broadcast_to(rep("ia,ga->gi", sel_s, b_s[l][:, :dseq])[:, :, None], (N_GROUPS_B, tm_s, C_B)))

        ck = jnp.transpose(cache_k[l], (0, 2, 3, 4, 1)).reshape(n_pool, -1, PAGE)
        cv = cache_v[l].reshape(n_pool, PAGE * N_HEADS, DV)

        q_p, k_p, v_p, kb_p, vb_p, ob_p = _before_attention(xp, mods_p, tables_p, p, tm=TOKEN_TILE, mix=mix_p,
                                                            emit_vn=False)
        q_s, k_s, v_s, _, _, ob_s, cv_s = _before_attention(xs, mods_s, tables_s, p, tm=tm_s, mix=mix_s,
                                                            emit_vn=True)
        per_seq = lambda a: a.reshape(ndec, dseq, -1)
        oa_p, oa_s = _attention(page_table, lam_p[l], g_attn[l], q_p, kb_p, vb_p, per_seq(q_s), per_seq(k_s),
                                per_seq(v_s), ck, cv, lam_init)
        routed_p = _after_attention(xp, oa_p, ob_p, mods_p, p, tm=TOKEN_TILE)
        routed_s = _after_attention(xs, oa_s.reshape(1, tm_s, -1), ob_s, mods_s, p, tm=tm_s)
        xp, xs = _moe(routed_p, routed_s, gf, p["w_gate"], p["w_up"], p["w_down"], tm=TOKEN_TILE,
                      final=l == depth - 1)
        kp_l.append(k_p.reshape(nb, seq, N_HEADS, 2, DK))
        vp_l.append(v_p.reshape(nb, seq, N_HEADS, DV))
        ks_l.append(k_s.reshape(ndec, dseq, N_HEADS, 2, DK))
        vs_l.append(v_s.reshape(ndec, dseq, N_HEADS, DV))
        cv_l.append(cv_s.reshape(ndec, dseq, -1))
    return (xp, xs.reshape(ndec, dseq, d), jnp.stack(kp_l), jnp.stack(vp_l), jnp.stack(ks_l), jnp.stack(vs_l),
            jnp.stack(cv_l))
```

```python
import functools
import math

import jax
import jax.numpy as jnp
from jax import lax
from jax.experimental import pallas as pl
from jax.experimental.pallas import tpu as pltpu

f32 = jnp.float32
bf16 = jnp.bfloat16
i32 = jnp.int32
SDS = jax.ShapeDtypeStruct

N_HEADS = 4
DK = 64
DV = 2 * DK
ROT = DK // 4
ROPE_THETA = 500000.0
N_GROUPS_B = 4
C_B = 128
CHUNK = 128
N_EXPERT_GROUPS = 4
EXPERTS_PER_GROUP = 8
N_EXPERTS = N_EXPERT_GROUPS * EXPERTS_PER_GROUP
PAGE = 128
EPS = 1e-6
NEG = -1e30
LOG2E = math.log2(math.e)

LANES = 128
SUBLANES = 8
VMEM_LIMIT = 56 * 1024 * 1024
ATTN_VMEM_LIMIT = 60 * 1024 * 1024

TOKEN_TILE = 512
ATTN_TQ = 256
ATTN_TK = 256
ONES_ROWS = 16
PAGES_PER_STEP = 16
PAGE_SETS = 3
EXPERT_BLOCK = 256
X_AHEAD = 2
RUN_ALIGN = 8
CHUNK_ROWS = (32, 16, 8)
N_LIST = 3 * len(CHUNK_ROWS)
ROUTER_ROWS = 40


def _cparams(sem=None):
    return pltpu.CompilerParams(dimension_semantics=sem, vmem_limit_bytes=VMEM_LIMIT)


def _round_up(x, m):
    return (x + m - 1) // m * m


def _adaln_kernel(c_ref, w_ref, b_ref, o_ref):
    a = jax.nn.silu(c_ref[...])
    o_ref[...] = jnp.dot(a, w_ref[...], preferred_element_type=f32,
                         precision=lax.Precision.HIGHEST) + b_ref[...]


def _adaln(c_all, w_ada, b_ada):
    n, d = c_all.shape
    m = w_ada.shape[1]
    tn = 1536
    return pl.pallas_call(
        _adaln_kernel,
        grid=(m // tn,),
        in_specs=[pl.BlockSpec((n, d), lambda j: (0, 0)),
                  pl.BlockSpec((d, tn), lambda j: (0, j)),
                  pl.BlockSpec((1, tn), lambda j: (0, j))],
        out_specs=pl.BlockSpec((n, tn), lambda j: (0, j)),
        out_shape=SDS((n, m), f32),
        compiler_params=_cparams(("arbitrary",)),
        name="adaln",
    )(c_all, w_ada, b_ada.reshape(1, m))


def _rope_kernel(inv_ref, cos_ref, sin_ref, *, rows_prompt, past_len):
    shape = cos_ref.shape
    r = lax.broadcasted_iota(i32, shape, 0)
    l = lax.broadcasted_iota(i32, shape, 1)
    base = jnp.where(r < rows_prompt, r * 16, past_len + (r - rows_prompt) * 16)
    pos = base + (l >> 3)
    ang = pos.astype(f32) * inv_ref[...]
    cos_ref[...] = jnp.cos(ang)
    sin_ref[...] = jnp.sin(ang)


def _rope_tables(seq, dec_seq, past_len):
    inv = ROPE_THETA ** (-jnp.arange(0, ROT, 2, dtype=f32) / ROT)
    inv_lane = jnp.tile(inv, LANES // (ROT // 2)).reshape(1, LANES)
    rp = seq // 16
    rt = rp + SUBLANES
    cos_c, sin_c = pl.pallas_call(
        functools.partial(_rope_kernel, rows_prompt=rp, past_len=past_len),
        out_shape=(SDS((rt, LANES), f32), SDS((rt, LANES), f32)),
        name="rope_table",
    )(inv_lane)

    def expand(c8, s8):
        n = c8.shape[0]
        z8 = jnp.zeros((n, ROT // 2), f32)
        rest = DK - ROT
        cos_t = jnp.concatenate([c8, c8, jnp.ones((n, rest), f32)], axis=1)
        sin_a = jnp.concatenate([-s8, z8, jnp.zeros((n, rest), f32)], axis=1)
        sin_b = jnp.concatenate([z8, s8, jnp.zeros((n, rest), f32)], axis=1)
        return tuple(jnp.tile(t, (1, LANES // DK)) for t in (cos_t, sin_a, sin_b))

    half = ROT // 2
    prompt = expand(cos_c[:rp].reshape(seq, half), sin_c[:rp].reshape(seq, half))
    sample = expand(cos_c[rp].reshape(16, half)[:dec_seq], sin_c[rp].reshape(16, half)[:dec_seq])
    return prompt, sample


def _rms(x, g):
    return x * lax.rsqrt(jnp.mean(x * x, axis=-1, keepdims=True) + EPS) * g


def _in_proj_kernel(x_ref, sh_ref, sc_ref, g1_ref, w_ref, cos_ref, sa_ref, sb_ref,
                    ws_ref, msk_ref, bs_ref, gv_ref, gm_ref,
                    q_ref, k_ref, v_ref, kb_ref, vb_ref, ob_ref, *vn_refs, mix_rows):
    x = x_ref[0]
    tm = x.shape[0]
    h = _rms(x, g1_ref[...]) * (1.0 + sc_ref[0]) + sh_ref[0]
    z = jnp.dot(h.astype(bf16), w_ref[...], preferred_element_type=f32)
    qk_w = N_HEADS * 2 * DK
    a_w = N_HEADS * DV
    b_w = N_GROUPS_B * C_B
    cos_t, sin_a, sin_b = cos_ref[...], sa_ref[...], sb_ref[...]
    for s in range(2 * qk_w // LANES):
        zs = z[:, s * LANES:(s + 1) * LANES]
        rot = zs * cos_t + pltpu.roll(zs, LANES - ROT // 2, 1) * sin_a + pltpu.roll(zs, ROT // 2, 1) * sin_b
        if s < qk_w // LANES:
            q_ref[0, :, s * LANES:(s + 1) * LANES] = (rot * (DK ** -0.5 * LOG2E)).astype(bf16)
        else:
            k_ref[0, :, s * LANES - qk_w:(s + 1) * LANES - qk_w] = rot
            kb_ref[0, :, s * LANES - qk_w:(s + 1) * LANES - qk_w] = rot.astype(bf16)
    v = z[:, 2 * qk_w:2 * qk_w + a_w]
    for h in range(N_HEADS):
        v_ref[0, pl.ds(h, tm, stride=N_HEADS), :] = v[:, h * DV:(h + 1) * DV]
    vb_ref[0] = v.astype(bf16)
    uv = jax.nn.gelu(z[:, 2 * qk_w + a_w:])
    nblk = tm // mix_rows
    for g in range(N_GROUPS_B):
        u = uv[:, g * C_B:(g + 1) * C_B]
        vn = _rms(uv[:, b_w + g * C_B:b_w + (g + 1) * C_B], gv_ref[:, g * C_B:(g + 1) * C_B])
        if vn_refs:
            vn_refs[0][0, :, g * C_B:(g + 1) * C_B] = vn
        wm = (ws_ref[g] * msk_ref[...]).astype(bf16)
        vcat = jnp.concatenate([vn[j * mix_rows:(j + 1) * mix_rows] for j in range(nblk)], axis=1)
        mixed = jnp.dot(wm, vcat.astype(bf16), preferred_element_type=f32)
        for j in range(nblk):
            t = u[j * mix_rows:(j + 1) * mix_rows] * (mixed[:, j * C_B:(j + 1) * C_B] + bs_ref[g])
            ob_ref[0, j * mix_rows:(j + 1) * mix_rows, g * C_B:(g + 1) * C_B] = _rms(
                t, gm_ref[:, g * C_B:(g + 1) * C_B]).astype(bf16)


def _in_proj(x, shift, scale, g1, w_in_bf, tables, ws_t, mask, bs_b, g_v, g_mlp, *, tm, emit_vn):
    n, t, d = x.shape
    in_w = w_in_bf.shape[1]
    r = ws_t.shape[1]
    mod_rows = shift.shape[1]
    tab_rows = tables[0].shape[0]
    w_half = N_HEADS * DV

    def mod_spec():
        if mod_rows == 1:
            return pl.BlockSpec((1, 1, d), lambda b, i: (b, 0, 0))
        return pl.BlockSpec((1, tm, d), lambda b, i: (b, i, 0))

    def tab_spec():
        if tab_rows == tm:
            return pl.BlockSpec((tm, LANES), lambda b, i: (0, 0))
        return pl.BlockSpec((tm, LANES), lambda b, i: (i, 0))

    full = lambda shp: pl.BlockSpec(shp, lambda b, i: (0,) * len(shp))
    tok = lambda w: pl.BlockSpec((1, tm, w), lambda b, i: (b, i, 0))
    out_shape = [SDS((n, t, w_half), bf16), SDS((n, t, w_half), f32), SDS((n, t * N_HEADS, DV), f32),
                 SDS((n, t, w_half), bf16), SDS((n, t, w_half), bf16), SDS((n, t, w_half), bf16)]
    out_specs = [tok(w_half)] * 6
    out_specs[2] = pl.BlockSpec((1, tm * N_HEADS, DV), lambda b, i: (b, i, 0))
    if emit_vn:
        out_shape.append(SDS((n, t, w_half), f32))
        out_specs.append(tok(w_half))
    return pl.pallas_call(
        functools.partial(_in_proj_kernel, mix_rows=r),
        grid=(n, t // tm),
        in_specs=[tok(d), mod_spec(), mod_spec(), full((1, d)), full((d, in_w)),
                  tab_spec(), tab_spec(), tab_spec(),
                  full((N_GROUPS_B, r, r)), full((r, r)), full((N_GROUPS_B, r, C_B)),
                  full((1, w_half)), full((1, w_half))],
        out_specs=out_specs,
        out_shape=out_shape,
        compiler_params=_cparams(("arbitrary", "arbitrary")),
        name="in_proj",
    )(x, shift, scale, g1, w_in_bf, *tables, ws_t, mask, bs_b, g_v, g_mlp)


def _diff_lambda(lam_ref, lam_init):
    lp = lam_ref[...]
    s1 = jnp.sum(lp[0:1] * lp[1:2], axis=1, keepdims=True)
    s2 = jnp.sum(lp[2:3] * lp[3:4], axis=1, keepdims=True)
    return jnp.exp(s1) - jnp.exp(s2) + lam_init


def _attn_kernel(pt_ref, lam_ref, g_ref, q_ref, k_ref, v_ref, qs_ref, kn_ref, vn_ref, ck_ref, cv_ref,
                 o_ref, os_ref, vt_sc, qq_sc, acc_sc, m_sc, s_sc, qbd_sc, sm_sc, sl_sc, sacc_sc, kbuf, vbuf, sem,
                 *, lam_init, n_steps):
    b = pl.program_id(0)
    seq = q_ref.shape[1]
    tq, tk = ATTN_TQ, ATTN_TK
    n_qb = seq // tq
    lam = _diff_lambda(lam_ref, lam_init)
    heads = range(N_HEADS)
    nt = (((1,), (1,)), ((), ()))

    seq_per_step, ds, width = qs_ref.shape
    pps = kbuf.shape[1]
    groups = pt_ref.shape[1] // pps
    slots = seq_per_step * groups
    total_slots = n_steps * slots
    rows_h = 2 * ds
    rows = N_HEADS * rows_h

    assert groups & (groups - 1) == 0
    g_shift = groups.bit_length() - 1
    n_sets = kbuf.shape[0]
    g_first = b * slots

    def set_of(g):
        return lax.rem(g, n_sets)

    def page_copies(g, buf):
        sq = g >> g_shift
        first_page = (g & (groups - 1)) * pps
        out = []
        for i in range(pps):
            page = pt_ref[sq, first_page + i]
            out.append(pltpu.make_async_copy(ck_ref.at[page], kbuf.at[buf, i], sem.at[buf]))
            out.append(pltpu.make_async_copy(cv_ref.at[page], vbuf.at[buf, i], sem.at[buf]))
        return out

    def wait_pages(buf):
        for i in range(pps):
            pltpu.make_async_copy(ck_ref.at[0], kbuf.at[buf, i], sem.at[buf]).wait()
            pltpu.make_async_copy(cv_ref.at[0], vbuf.at[buf, i], sem.at[buf]).wait()

    def sample_update(s, value):
        m_prev = sm_sc[...]
        m_new = jnp.maximum(m_prev, jnp.max(s, axis=1, keepdims=True))
        alpha = jnp.exp2(m_prev - m_new)
        p = jnp.exp2(s - m_new)
        sl_sc[...] = alpha * sl_sc[...] + jnp.sum(p, axis=1, keepdims=True)
        pb = p.astype(bf16)
        n_blk = s.shape[1] // PAGE
        parts = [jnp.dot(pb[h * rows_h:(h + 1) * rows_h],
                         jnp.concatenate([value(i, h) for i in range(n_blk)], axis=0), preferred_element_type=f32)
                 for h in heads]
        sacc_sc[...] = sacc_sc[...] * alpha + jnp.concatenate(parts, axis=0)
        sm_sc[...] = m_new

    def slot_begin(t):
        g = g_first + t
        wait_pages(set_of(g))
        g_ahead = g + n_sets - 1
        for cp in page_copies(jnp.minimum(g_ahead, total_slots - 1), set_of(g_ahead)):
            cp.start()

        @pl.when(t & (groups - 1) == 0)
        def _():
            row_i = lax.broadcasted_iota(i32, (rows, width), 0)
            col_i = lax.broadcasted_iota(i32, (rows, width), 1)
            qt = jnp.concatenate([qs_ref[t >> g_shift].astype(f32)] * (N_HEADS * 2), axis=0)
            same_map = (col_i >> (DK.bit_length() - 1)) == (row_i >> (ds.bit_length() - 1))
            qbd_sc[...] = jnp.where(same_map, qt, 0.0).astype(bf16)
            sm_sc[...] = jnp.full(sm_sc.shape, NEG, f32)
            sl_sc[...] = jnp.zeros(sl_sc.shape, f32)
            sacc_sc[...] = jnp.zeros(sacc_sc.shape, f32)

    def slot_hooks(t, n_parts):
        buf = set_of(g_first + t)
        ppp = pps // n_parts
        carried = {}

        def scores_of(part):
            def run():
                k_all = jnp.concatenate([kbuf[buf, part * ppp + i].astype(bf16) for i in range(ppp)], axis=1)
                carried[part] = jnp.dot(qbd_sc[...], k_all, preferred_element_type=f32)
            return run

        def values_of(part):
            def run():
                sample_update(carried.pop(part), lambda i, h: vbuf[
                    buf, part * ppp + i, pl.ds(h, PAGE, stride=N_HEADS), :].astype(bf16))
            return run

        return [f(part) for part in range(n_parts) for f in (scores_of, values_of)]

    def slot_end(t):
        @pl.when(t & (groups - 1) == groups - 1)
        def _():
            sq = t >> g_shift
            qbd = qbd_sc[...]
            pad = jnp.zeros((PAGE - ds, width), f32)
            kn = jnp.concatenate([kn_ref[sq], pad], axis=0).astype(bf16)
            vn = jnp.concatenate([vn_ref[sq], pad], axis=0).astype(bf16)
            s_new = lax.dot_general(qbd, kn, nt, preferred_element_type=f32)
            key = lax.broadcasted_iota(i32, s_new.shape, 1)
            qry = lax.broadcasted_iota(i32, s_new.shape, 0) & (ds - 1)
            sample_update(jnp.where(key <= qry, s_new, NEG), lambda i, h: vn[:, h * DV:(h + 1) * DV])
            out = sacc_sc[...] * (1.0 / sl_sc[...])
            for h in heads:
                o = out[h * rows_h:h * rows_h + ds] - lam * out[h * rows_h + ds:(h + 1) * rows_h]
                os_ref[sq, :, h * DV:(h + 1) * DV] = _rms(o, g_ref[h:h + 1, :] * (1.0 - lam_init)).astype(bf16)

    @pl.when(b == 0)
    def _():
        for g in range(n_sets - 1):
            for cp in page_copies(min(g, total_slots - 1), g):
                cp.start()

    def prep(j, c):
        sl = pl.ds(pl.multiple_of(j * tk, tk), tk)
        for h in heads:
            vt_sc[h, j, :DV] = v_ref[0, sl, h * DV:(h + 1) * DV].T
            vt_sc[h, j, DV:] = jnp.ones((ONES_ROWS, tk), bf16)
        return c

    lax.fori_loop(0, seq // tk, prep, 0)
    lane = lax.broadcasted_iota(i32, (tq, DV), 1)

    def q_block(qi, t, slot_in_diag):
        qsl = pl.ds(pl.multiple_of(qi * tq, tq), tq)
        for h in heads:
            qb = q_ref[0, qsl, h * DV:(h + 1) * DV]
            zero = jnp.zeros_like(qb)
            qq_sc[h, :tq] = jnp.where(lane < DK, qb, zero)
            qq_sc[h, tq:] = jnp.where(lane >= DK, qb, zero)
            m_sc[h] = jnp.full(m_sc.shape[1:], NEG, f32)
            acc_sc[h] = jnp.zeros(acc_sc.shape[1:], f32)

        def scores(j, h):
            ksl = pl.ds(pl.multiple_of(j * tk, tk), tk)
            return lax.dot_general(k_ref[0, ksl, h * DV:(h + 1) * DV], qq_sc[h], nt, preferred_element_type=f32)

        s_sc[...] = scores(0, 0)

        def kv_block(j, masked, hooks=()):
            s_next = None
            for h in heads:
                s = s_sc[...] if h == 0 else s_next
                if h + 1 < N_HEADS:
                    s_next = scores(j, h + 1)
                elif not masked:
                    s_sc[...] = scores(j + 1, 0)
                if h < len(hooks):
                    hooks[h]()
                if masked:
                    key_i = lax.broadcasted_iota(i32, s.shape, 0)
                    qry_i = lax.broadcasted_iota(i32, s.shape, 1) & (tq - 1)
                    s = jnp.where(key_i <= qry_i, s, NEG)
                m_prev = m_sc[h]
                m_new = jnp.maximum(m_prev, jnp.max(s, axis=0, keepdims=True))
                alpha = jnp.exp2(m_prev - m_new)
                p = jnp.exp2(s - m_new)
                pv = jnp.dot(vt_sc[h, j], p.astype(bf16), preferred_element_type=f32)
                acc_sc[h] = acc_sc[h] * alpha + pv
                m_sc[h] = m_new

        def full_block(j, cc):
            kv_block(j, False)
            return cc

        idle = lambda: None

        def full_pair(jj, tt):
            slot_begin(tt)
            slot_scores, slot_values = slot_hooks(tt, 1)
            kv_block(2 * jj, False, [idle, idle, slot_scores])
            kv_block(2 * jj + 1, False, [idle, idle, slot_values])
            slot_end(tt)
            return tt + 1

        pairs = qi >> 1
        t = lax.fori_loop(0, pairs, full_pair, t)
        lax.fori_loop(2 * pairs, qi, full_block, 0)
        if slot_in_diag:
            slot_begin(t)
            slot_scores, slot_values = slot_hooks(t, 1)
            slot_scores()
            kv_block(qi, True, [idle, idle, slot_values])
        else:
            kv_block(qi, True)
        for h in heads:
            acc = acc_sc[h]
            inv_l = 1.0 / acc[DV:DV + 1, :]
            o_t = acc[:DV, :tq] * inv_l[:, :tq] - lam * (acc[:DV, tq:] * inv_l[:, tq:])
            o_ref[0, qsl, h * DV:(h + 1) * DV] = _rms(o_t.T, g_ref[h:h + 1, :] * (1.0 - lam_init)).astype(bf16)
        if slot_in_diag:
            slot_end(t)
            t = t + 1
        return t

    pair_slots = sum(qi >> 1 for qi in range(n_qb))
    diag_slots = slots - pair_slots
    assert 0 <= diag_slots <= n_qb, (slots, pair_slots)
    t = lax.fori_loop(0, diag_slots, lambda qi, tt: q_block(qi, tt, True), 0)
    lax.fori_loop(diag_slots, n_qb, lambda qi, tt: q_block(qi, tt, False), t)

    @pl.when(b == n_steps - 1)
    def _():
        for g in range(total_slots, total_slots + n_sets - 1):
            wait_pages(g % n_sets)


def _attention(page_table, lam_p, g_attn, q, k, v, q_s, k_new, v_new, cache_kt, cache_vr, lam_init):
    n, seq, w = q.shape
    nseq, ds, _ = q_s.shape
    assert nseq % n == 0 and ds & (ds - 1) == 0 and page_table.shape[1] % PAGES_PER_STEP == 0
    sps = nseq // n
    rows = N_HEADS * 2 * ds
    blk = pl.BlockSpec((1, seq, w), lambda b, pt: (b, 0, 0))
    blk_in = pl.BlockSpec((1, seq, w), lambda b, pt: (b, 0, 0), pipeline_mode=pl.Buffered(1))
    sblk = pl.BlockSpec((sps, ds, w), lambda b, pt: (b, 0, 0))
    full = lambda a: pl.BlockSpec(a.shape, lambda b, pt: (0,) * a.ndim)
    any_spec = pl.BlockSpec(memory_space=pl.ANY)
    grid_spec = pltpu.PrefetchScalarGridSpec(
        num_scalar_prefetch=1,
        grid=(n,),
        in_specs=[full(lam_p), full(g_attn), blk_in, blk_in, blk_in, sblk, sblk, sblk, any_spec, any_spec],
        out_specs=[blk, sblk],
        scratch_shapes=[pltpu.VMEM((N_HEADS, seq // ATTN_TK, DV + ONES_ROWS, ATTN_TK), bf16),
                        pltpu.VMEM((N_HEADS, 2 * ATTN_TQ, DV), bf16),
                        pltpu.VMEM((N_HEADS, DV + ONES_ROWS, 2 * ATTN_TQ), f32),
                        pltpu.VMEM((N_HEADS, 1, 2 * ATTN_TQ), f32),
                        pltpu.VMEM((ATTN_TK, 2 * ATTN_TQ), f32),
                        pltpu.VMEM((rows, w), bf16), pltpu.VMEM((rows, 1), f32), pltpu.VMEM((rows, 1), f32),
                        pltpu.VMEM((rows, DV), f32),
                        pltpu.VMEM((PAGE_SETS, PAGES_PER_STEP) + cache_kt.shape[1:], f32),
                        pltpu.VMEM((PAGE_SETS, PAGES_PER_STEP) + cache_vr.shape[1:], f32),
                        pltpu.SemaphoreType.DMA((PAGE_SETS,))],
    )
    return pl.pallas_call(
        functools.partial(_attn_kernel, lam_init=lam_init, n_steps=n),
        grid_spec=grid_spec,
        out_shape=[SDS((n, seq, w), bf16), SDS((nseq, ds, w), bf16)],
        compiler_params=pltpu.CompilerParams(dimension_semantics=("arbitrary",), vmem_limit_bytes=ATTN_VMEM_LIMIT),
        name="attention",
    )(page_table, lam_p, g_attn, q, k, v, q_s, k_new, v_new, cache_kt, cache_vr)


def _out_route_kernel(oa_ref, ob_ref, x_ref, g1_ref, sh_ref, sc_ref, wa_ref, wb_ref, g2_ref, wr_ref, br_ref,
                      x1_ref, h2_ref, ids_ref, gates_ref, cnt_ref):
    mix = (jnp.dot(oa_ref[0], wa_ref[...], preferred_element_type=f32)
           + jnp.dot(ob_ref[0], wb_ref[...], preferred_element_type=f32))
    x1 = x_ref[0] + g1_ref[0] * mix
    x1_ref[0] = x1
    h2 = _rms(x1, g2_ref[...]) * (1.0 + sc_ref[0]) + sh_ref[0]
    h2_ref[0] = h2.astype(bf16)
    tm = h2.shape[0]
    wr = wr_ref[...]
    wr_hi = wr.astype(bf16)
    wr_lo = (wr - wr_hi.astype(f32)).astype(bf16)
    h_hi = h2.astype(bf16)
    h_lo = (h2 - h_hi.astype(f32)).astype(bf16)
    nt = (((1,), (1,)), ((), ()))
    r1 = lax.dot_general(jnp.concatenate([wr_hi, wr_lo], axis=0), h_hi, nt, preferred_element_type=f32)
    r2 = lax.dot_general(wr_hi, h_lo, nt, preferred_element_type=f32)
    lg = r1[:ROUTER_ROWS] + r1[ROUTER_ROWS:] + r2 + br_ref[:, 0:1]
    row = lax.broadcasted_iota(i32, (SUBLANES, tm), 0).astype(f32)
    big = float(SUBLANES)
    gl = jnp.where(row < N_EXPERT_GROUPS, lg[0:SUBLANES], NEG)
    gmax = jnp.max(gl, axis=0, keepdims=True)
    g_p = 1.0 / jnp.sum(jnp.exp(gl - gmax), axis=0, keepdims=True)
    gidx = jnp.min(jnp.where(gl == gmax, row, big), axis=0, keepdims=True)
    esel = jnp.zeros((SUBLANES, tm), f32)
    for g in range(N_EXPERT_GROUPS):
        esel = jnp.where(gidx == float(g), lg[SUBLANES * (g + 1):SUBLANES * (g + 2)], esel)
    e1 = jnp.max(esel, axis=0, keepdims=True)
    i1 = jnp.min(jnp.where(esel == e1, row, big), axis=0, keepdims=True)
    esel2 = jnp.where(row == i1, -jnp.inf, esel)
    e2 = jnp.max(esel2, axis=0, keepdims=True)
    i2 = jnp.min(jnp.where(esel2 == e2, row, big), axis=0, keepdims=True)
    t = jnp.exp(e2 - e1)
    w1 = g_p / (1.0 + t)
    ids = jnp.concatenate([gidx * EXPERTS_PER_GROUP + i1, gidx * EXPERTS_PER_GROUP + i2], axis=0).astype(i32)
    ids_ref[0] = ids
    gates_ref[0] = jnp.concatenate([w1, w1 * t], axis=0)
    e_iota = lax.broadcasted_iota(i32, (N_EXPERTS, tm), 0)
    onehot = jnp.where((e_iota == ids[0:1]) | (e_iota == ids[1:2]), 1.0, 0.0).astype(bf16)
    cnt_ref[0] = jnp.dot(onehot, jnp.ones((tm, LANES), bf16), preferred_element_type=f32)


def _out_route(oa, ob, x, gate1, shift2, scale2, w_a, w_b, g2, wr_t, br_b, *, tm):
    n, t, d = x.shape
    w_half = oa.shape[2]
    mod_rows = gate1.shape[1]
    tpn = t // tm

    def mod_spec():
        if mod_rows == 1:
            return pl.BlockSpec((1, 1, d), lambda b, i: (b, 0, 0))
        return pl.BlockSpec((1, tm, d), lambda b, i: (b, i, 0))

    full = lambda shp: pl.BlockSpec(shp, lambda b, i: (0,) * len(shp))
    tok = lambda w: pl.BlockSpec((1, tm, w), lambda b, i: (b, i, 0))
    return pl.pallas_call(
        _out_route_kernel,
        grid=(n, tpn),
        in_specs=[tok(w_half), tok(w_half), tok(d), mod_spec(), mod_spec(), mod_spec(),
                  full((w_half, d)), full((w_half, d)), full((1, d)),
                  full((ROUTER_ROWS, d)), full((ROUTER_ROWS, LANES))],
        out_specs=[tok(d), tok(d),
                   pl.BlockSpec((1, 2, tm), lambda b, i: (b * tpn + i, 0, 0)),
                   pl.BlockSpec((1, 2, tm), lambda b, i: (b * tpn + i, 0, 0)),
                   pl.BlockSpec((1, N_EXPERTS, LANES), lambda b, i: (b * tpn + i, 0, 0))],
        out_shape=[SDS((n, t, d), f32), SDS((n, t, d), bf16), SDS((n * tpn, 2, tm), i32),
                   SDS((n * tpn, 2, tm), f32), SDS((n * tpn, N_EXPERTS, LANES), f32)],
        compiler_params=_cparams(("arbitrary", "arbitrary")),
        name="out_route",
    )(oa, ob, x, gate1, shift2, scale2, w_a, w_b, g2, wr_t, br_b)


def _chunk_list(counts, src0, dst0, rows, kmax):
    cin = jnp.cumsum(counts, axis=1)
    cex = cin - counts
    k = jnp.arange(kmax, dtype=i32)
    owner = jnp.sum((cin[:, None, :] <= k[None, :, None]).astype(i32), axis=2)
    owner = jnp.minimum(owner, N_EXPERTS - 1)
    pick = (owner[:, :, None] == jnp.arange(N_EXPERTS, dtype=i32)).astype(i32)
    at_owner = lambda v: jnp.sum(pick * v[:, None, :], axis=2)
    within = (k[None, :] - at_owner(cex)) * rows
    src = at_owner(src0) + within
    dst = at_owner(dst0) + within
    return src.reshape(-1).astype(i32), dst.reshape(-1).astype(i32), cin[:, -1].astype(i32)


def _route_plan(cnt, nb_max, sorted_rows):
    pc = _round_up(cnt, RUN_ALIGN)
    toff = jnp.cumsum(pc, axis=1) - pc
    tot = jnp.sum(pc, axis=0)
    seg = _round_up(tot, EXPERT_BLOCK)
    seg_end = jnp.cumsum(seg)
    seg_start = seg_end - seg
    run_start = seg_start[None, :] + jnp.cumsum(pc, axis=0) - pc
    nb = seg_end[-1] // EXPERT_BLOCK
    lists = []
    done = jnp.zeros_like(pc)
    for rows in CHUNK_ROWS:
        n = (pc - done) // rows
        kmax = sorted_rows // rows if rows == CHUNK_ROWS[0] else N_EXPERTS
        lists.extend(_chunk_list(n, toff + done, run_start + done, rows, kmax))
        done = done + n * rows
    return dict(
        lists=lists,
        tail_start=(seg_start + tot).astype(i32), tail8=((seg - tot) // RUN_ALIGN).astype(i32),
        tail_tot8=(jnp.sum(seg - tot) // RUN_ALIGN).reshape(1).astype(i32),
        seg_start=seg_start.astype(i32), seg_blocks=(seg // EXPERT_BLOCK).astype(i32),
        nb=nb.reshape(1).astype(i32),
        toff_v=jnp.broadcast_to(toff.astype(f32)[:, :, None], toff.shape + (LANES,)))


def _sorted_positions(ids_ref, toff_ref, upper_ref, tm):
    ids = ids_ref[0]
    idc = jnp.concatenate([ids[0:1], ids[1:2]], axis=1)
    e_iota = lax.broadcasted_iota(i32, (N_EXPERTS, 2 * tm), 0)
    onehot = jnp.where(e_iota == idc, 1.0, 0.0)
    before = jnp.dot(onehot.astype(bf16), upper_ref[...], preferred_element_type=f32)
    pos = jnp.sum(onehot * (before + toff_ref[0][:, 0:1]), axis=0, keepdims=True)
    return jnp.where(idc >= 0, pos, -1.0)


def _fill_upper(upper_ref):
    n = upper_ref.shape[0]
    a_src = lax.broadcasted_iota(i32, (n, n), 0)
    a_dst = lax.broadcasted_iota(i32, (n, n), 1)
    upper_ref[...] = jnp.where(a_src < a_dst, 1.0, 0.0).astype(bf16)


def _aligned(row):
    return row if isinstance(row, int) else pl.multiple_of(row, RUN_ALIGN)


def _run_chunk_copy(vmem_buf, hbm_buf, sem, vrow, hrow, to_hbm, rows=RUN_ALIGN):
    v = vmem_buf.at[pl.ds(_aligned(vrow), rows)]
    h = hbm_buf.at[pl.ds(_aligned(hrow), rows)]
    return pltpu.make_async_copy(v, h, sem) if to_hbm else pltpu.make_async_copy(h, v, sem)


def _tile_chunks(i, list_refs, vmem_buf, hbm_buf, sems, to_hbm, wait):
    for c, rows in enumerate(CHUNK_ROWS):
        src_ref, dst_ref, n_ref = list_refs[3 * c:3 * c + 3]
        stride = src_ref.shape[0] // n_ref.shape[0]

        def one(k, carry, rows=rows, src_ref=src_ref, dst_ref=dst_ref, stride=stride, sem=sems.at[c]):
            if wait:
                _run_chunk_copy(vmem_buf, hbm_buf, sem, 0, 0, to_hbm, rows).wait()
            else:
                _run_chunk_copy(vmem_buf, hbm_buf, sem, src_ref[i * stride + k], dst_ref[i * stride + k],
                                to_hbm, rows).start()
            return carry

        lax.fori_loop(0, n_ref[i], one, 0)


def _dispatch_kernel(*refs):
    lists = refs[:N_LIST]
    tstart_ref, tail8_ref, ttot_ref, nb_ref = refs[N_LIST:N_LIST + 4]
    hm_ref, ht_ref, ids_ref, toff_ref, xs_ref, upper_sc, xbuf_sc, zero_sc, sem = refs[N_LIST + 4:]
    i = pl.program_id(0)
    last = pl.num_programs(0) - 1
    tm = hm_ref.shape[0]
    rt = xbuf_sc.shape[1]
    slot = i & 1
    sem_tail, sem_dead = sem.at[len(CHUNK_ROWS)], sem.at[len(CHUNK_ROWS) + 1]

    @pl.when(i == 0)
    def _():
        _fill_upper(upper_sc)
        zero_sc[...] = jnp.zeros(zero_sc.shape, f32)

    h = jnp.where(i == last, ht_ref[...], hm_ref[...])
    pos = _sorted_positions(ids_ref, toff_ref, upper_sc, tm)
    r_iota = lax.broadcasted_iota(i32, (rt, tm), 0).astype(f32)
    perm = jnp.where((r_iota == pos[:, :tm]) | (r_iota == pos[:, tm:]), 1.0, 0.0).astype(bf16)
    xbuf_sc[slot] = jnp.dot(perm, h, preferred_element_type=f32)

    @pl.when(i > 0)
    def _():
        _tile_chunks(i - 1, lists, xbuf_sc.at[1 - slot], xs_ref, sem, True, wait=True)

    _tile_chunks(i, lists, xbuf_sc.at[slot], xs_ref, sem, True, wait=False)

    @pl.when(i == last)
    def _():
        _tile_chunks(i, lists, xbuf_sc.at[slot], xs_ref, sem, True, wait=True)

        def per_expert(e, c):
            def per_chunk(k, cc):
                _run_chunk_copy(zero_sc, xs_ref, sem_tail, 0, tstart_ref[e] + k * RUN_ALIGN, True).start()
                return cc
            lax.fori_loop(0, tail8_ref[e], per_chunk, 0)
            return c
        lax.fori_loop(0, N_EXPERTS, per_expert, 0)

        def wait_one(k, c):
            _run_chunk_copy(zero_sc, xs_ref, sem_tail, 0, 0, True).wait()
            return c
        lax.fori_loop(0, ttot_ref[0], wait_one, 0)

        def dead_block(row):
            return pltpu.make_async_copy(zero_sc, xs_ref.at[pl.ds(row, EXPERT_BLOCK)], sem_dead)

        n_dead = xs_ref.shape[0] // EXPERT_BLOCK - nb_ref[0]

        def start_dead(k, c):
            dead_block(pl.multiple_of((nb_ref[0] + k) * EXPERT_BLOCK, EXPERT_BLOCK)).start()
            return c
        lax.fori_loop(0, n_dead, start_dead, 0)

        def wait_dead(k, c):
            dead_block(0).wait()
            return c
        lax.fori_loop(0, n_dead, wait_dead, 0)


def _sorted_rows(tm):
    return _round_up(2 * tm + N_EXPERTS * (RUN_ALIGN - 1), LANES)


def _dispatch(plan, h_main, h_tail, ids, toff_v, rows_total):
    tiles, _, tm = ids.shape
    d = h_main.shape[1]
    tiles_main = h_main.shape[0] // tm
    assert tiles == tiles_main + 1 and h_tail.shape[0] == tm
    rt = _sorted_rows(tm)
    grid_spec = pltpu.PrefetchScalarGridSpec(
        num_scalar_prefetch=N_LIST + 4,
        grid=(tiles,),
        in_specs=[pl.BlockSpec((tm, d), lambda i, *_: (jnp.minimum(i, tiles_main - 1), 0)),
                  pl.BlockSpec((tm, d), lambda i, *_: (0, 0)),
                  pl.BlockSpec((1, 2, tm), lambda i, *_: (i, 0, 0)),
                  pl.BlockSpec((1, N_EXPERTS, LANES), lambda i, *_: (i, 0, 0))],
        out_specs=pl.BlockSpec(memory_space=pl.ANY),
        scratch_shapes=[pltpu.VMEM((2 * tm, 2 * tm), bf16), pltpu.VMEM((2, rt, d), f32),
                        pltpu.VMEM((EXPERT_BLOCK, d), f32), pltpu.SemaphoreType.DMA((len(CHUNK_ROWS) + 2,))],
    )
    return pl.pallas_call(
        _dispatch_kernel,
        grid_spec=grid_spec,
        out_shape=SDS((rows_total, d), f32),
        compiler_params=_cparams(("arbitrary",)),
        name="dispatch",
    )(*plan["lists"], plan["tail_start"], plan["tail8"], plan["tail_tot8"], plan["nb"],
      h_main, h_tail, ids, toff_v)


def _experts_kernel(seg_ref, nblk_ref, nb_ref, x_ref, wg_ref, wu_ref, wd_ref, y_ref,
                    wg_sc, wu_sc, wd_sc, xbuf, ybuf, sem_in, sem_out):
    e = pl.program_id(0)
    eb = EXPERT_BLOCK
    nblk = nblk_ref[e]
    g0 = seg_ref[e] // eb
    n_all = nb_ref[0]

    def rows(g):
        return pl.ds(pl.multiple_of(g * eb, eb), eb)

    def x_copy(g):
        slot = lax.rem(g, X_AHEAD + 1)
        return pltpu.make_async_copy(x_ref.at[rows(g)], xbuf.at[slot], sem_in.at[slot])

    def y_copy(g):
        return pltpu.make_async_copy(ybuf.at[g & 1], y_ref.at[rows(g)], sem_out.at[g & 1])

    @pl.when(e == 0)
    def _():
        for g in range(X_AHEAD):
            @pl.when(g < n_all)
            def _():
                x_copy(g).start()

    @pl.when(nblk > 0)
    def _():
        wg_sc[...] = wg_ref[0].astype(bf16)
        wu_sc[...] = wu_ref[0].astype(bf16)
        wd_sc[...] = wd_ref[0].astype(bf16)

        def block(b, c):
            g = g0 + b
            x_copy(g).wait()

            @pl.when(g + X_AHEAD < n_all)
            def _():
                x_copy(g + X_AHEAD).start()

            xb = xbuf[lax.rem(g, X_AHEAD + 1)].astype(bf16)
            a = jnp.dot(xb, wg_sc[...], preferred_element_type=f32)
            u = jnp.dot(xb, wu_sc[...], preferred_element_type=f32)
            hid = (jax.nn.silu(a) * u).astype(bf16)
            y = jnp.dot(hid, wd_sc[...], preferred_element_type=f32)

            @pl.when(g >= 2)
            def _():
                y_copy(g - 2).wait()

            ybuf[g & 1] = y
            y_copy(g).start()
            return c

        lax.fori_loop(0, nblk, block, 0)

    @pl.when(e == pl.num_programs(0) - 1)
    def _():
        @pl.when(n_all >= 2)
        def _():
            y_copy(n_all - 2).wait()

        @pl.when(n_all >= 1)
        def _():
            y_copy(n_all - 1).wait()

        def dead_block(b):
            return pltpu.make_async_copy(ybuf.at[0], y_ref.at[pl.ds(pl.multiple_of(b * eb, eb), eb)], sem_out.at[0])

        n_dead = y_ref.shape[0] // eb - nb_ref[0]
        ybuf[0] = jnp.zeros(ybuf.shape[1:], f32)

        def start_dead(k, c):
            dead_block(nb_ref[0] + k).start()
            return c
        lax.fori_loop(0, n_dead, start_dead, 0)

        def wait_dead(k, c):
            dead_block(0).wait()
            return c
        lax.fori_loop(0, n_dead, wait_dead, 0)


def _experts(plan, xs, w_gate, w_up, w_down):
    rows_total, d = xs.shape
    n_exp, _, de = w_gate.shape
    eb = EXPERT_BLOCK
    grid_spec = pltpu.PrefetchScalarGridSpec(
        num_scalar_prefetch=3,
        grid=(n_exp,),
        in_specs=[pl.BlockSpec(memory_space=pl.ANY),
                  pl.BlockSpec((1, d, de), lambda e, *_: (e, 0, 0)),
                  pl.BlockSpec((1, d, de), lambda e, *_: (e, 0, 0)),
                  pl.BlockSpec((1, de, d), lambda e, *_: (e, 0, 0))],
        out_specs=pl.BlockSpec(memory_space=pl.ANY),
        scratch_shapes=[pltpu.VMEM((d, de), bf16), pltpu.VMEM((d, de), bf16), pltpu.VMEM((de, d), bf16),
                        pltpu.VMEM((X_AHEAD + 1, eb, d), f32), pltpu.VMEM((2, eb, d), f32),
                        pltpu.SemaphoreType.DMA((X_AHEAD + 1,)), pltpu.SemaphoreType.DMA((2,))],
    )
    return pl.pallas_call(
        _experts_kernel,
        grid_spec=grid_spec,
        out_shape=SDS((rows_total, d), f32),
        compiler_params=_cparams(("arbitrary",)),
        name="experts",
    )(plan["seg_start"], plan["seg_blocks"], plan["nb"], xs, w_gate, w_up, w_down)


def _combine_kernel(*refs, final):
    lists = refs[:N_LIST]
    (xm_ref, xt_ref, gm_ref, gt_ref, gf_ref, ids_ref, gates_ref, toff_ref, y_ref,
     om_ref, ot_ref, upper_sc, ybuf_sc, sem) = refs[N_LIST:]
    i = pl.program_id(0)
    last = pl.num_programs(0) - 1
    tm = xm_ref.shape[0]
    rt = ybuf_sc.shape[1]
    slot = i & 1

    @pl.when(i == 0)
    def _():
        _fill_upper(upper_sc)
        ybuf_sc[...] = jnp.zeros(ybuf_sc.shape, f32)
        _tile_chunks(0, lists, ybuf_sc.at[0], y_ref, sem, False, wait=False)

    _tile_chunks(i, lists, ybuf_sc.at[slot], y_ref, sem, False, wait=True)

    @pl.when(i < last)
    def _():
        _tile_chunks(i + 1, lists, ybuf_sc.at[1 - slot], y_ref, sem, False, wait=False)

    pos = _sorted_positions(ids_ref, toff_ref, upper_sc, tm)
    gates = gates_ref[0]
    r_iota = lax.broadcasted_iota(i32, (rt, tm), 0).astype(f32)
    weights = (jnp.where(r_iota == pos[:, :tm], gates[0:1], 0.0)
               + jnp.where(r_iota == pos[:, tm:], gates[1:2], 0.0)).astype(bf16)
    ff = lax.dot_general(weights, ybuf_sc[slot].astype(bf16), (((0,), (0,)), ((), ())),
                         preferred_element_type=f32)

    def finish(x1, gate):
        x2 = x1 + gate * ff
        return _rms(x2, gf_ref[...]) if final else x2

    @pl.when(i < last)
    def _():
        om_ref[...] = finish(xm_ref[...], gm_ref[0])

    @pl.when(i == last)
    def _():
        ot_ref[...] = finish(xt_ref[...], gt_ref[...])


def _combine(plan, x_main, x_tail, gate_main, gate_tail, g_final, ids, gates, toff_v, y, *, final):
    tiles, _, tm = ids.shape
    d = x_main.shape[1]
    tiles_main = x_main.shape[0] // tm
    tpr = tiles_main // gate_main.shape[0]
    rt = _sorted_rows(tm)
    main_idx = lambda i, *_: (jnp.minimum(i, tiles_main - 1), 0)
    tail_spec = pl.BlockSpec((tm, d), lambda i, *_: (0, 0))
    slot_spec = pl.BlockSpec((1, 2, tm), lambda i, *_: (i, 0, 0))
    grid_spec = pltpu.PrefetchScalarGridSpec(
        num_scalar_prefetch=N_LIST,
        grid=(tiles,),
        in_specs=[pl.BlockSpec((tm, d), main_idx), tail_spec,
                  pl.BlockSpec((1, 1, d), lambda i, *_: (jnp.minimum(i, tiles_main - 1) // tpr, 0, 0)), tail_spec,
                  pl.BlockSpec((1, d), lambda i, *_: (0, 0)), slot_spec, slot_spec,
                  pl.BlockSpec((1, N_EXPERTS, LANES), lambda i, *_: (i, 0, 0)),
                  pl.BlockSpec(memory_space=pl.ANY)],
        out_specs=[pl.BlockSpec((tm, d), main_idx), tail_spec],
        scratch_shapes=[pltpu.VMEM((2 * tm, 2 * tm), bf16), pltpu.VMEM((2, rt, d), f32),
                        pltpu.SemaphoreType.DMA((len(CHUNK_ROWS),))],
    )
    return pl.pallas_call(
        functools.partial(_combine_kernel, final=final),
        grid_spec=grid_spec,
        out_shape=[SDS(x_main.shape, f32), SDS((tm, d), f32)],
        compiler_params=_cparams(("arbitrary",)),
        name="combine",
    )(*plan["lists"], x_main, x_tail, gate_main, gate_tail, g_final, ids, gates, toff_v, y)


def _moe(main, tail, g_final, w_gate, w_up, w_down, *, tm, final):
    n, t, d = main["x1"].shape
    t_tail = tail["x1"].shape[1]
    pad = tm - t_tail
    tiles = n * t // tm + 1
    rows_max = 2 * (n * t + t_tail) + tiles * N_EXPERTS * (RUN_ALIGN - 1) + N_EXPERTS * (EXPERT_BLOCK - 1)
    nb_max = -(-rows_max // EXPERT_BLOCK)
    rows2 = lambda a: jnp.pad(a.reshape(t_tail, d), ((0, pad), (0, 0)))
    ids = jnp.concatenate([main["ids"], jnp.pad(tail["ids"], ((0, 0), (0, 0), (0, pad)), constant_values=-1)])
    gates = jnp.concatenate([main["gates"], jnp.pad(tail["gates"], ((0, 0), (0, 0), (0, pad)))])
    cnt = jnp.concatenate([main["cnt"], tail["cnt"]])[:, :, 0].astype(i32)
    plan = _route_plan(cnt, nb_max, _sorted_rows(tm))
    xs = _dispatch(plan, main["h2"].reshape(n * t, d), rows2(tail["h2"]), ids, plan["toff_v"],
                   nb_max * EXPERT_BLOCK)
    y = _experts(plan, xs, w_gate, w_up, w_down)
    out_m, out_t = _combine(plan, main["x1"].reshape(n * t, d), rows2(tail["x1"]), main["gate"],
                            rows2(tail["gate"]), g_final, ids, gates, plan["toff_v"], y, final=final)
    return out_m.reshape(n, t, d), out_t[:t_tail].reshape(1, t_tail, d)


def _before_attention(x, mods, tables, p, *, tm, mix, emit_vn):
    ws_t, mask, bs_b = mix
    return _in_proj(x, mods[0], mods[1], p["g1"], p["w_in"], tables, ws_t, mask, bs_b, p["g_v"], p["g_mlp"],
                    tm=tm, emit_vn=emit_vn)


def _after_attention(x, oa, ob, mods, p, *, tm):
    x1, h2, ids, gates, cnt = _out_route(oa, ob, x, mods[2], mods[3], mods[4], p["w_out_a"], p["w_out_b"],
                                         p["g2"], p["wr_t"], p["br_b"], tm=tm)
    return dict(x1=x1, h2=h2, ids=ids, gates=gates, cnt=cnt, gate=mods[5])


def kernel(x_prompt, x_sample, cache_k, cache_v, page_table, c_prompt, c_sample, w_ada, b_ada, g_norm1, w_in,
           lam_p, g_attn, g_v, w_s, b_s, g_mlp, w_out, g_norm2, w_router_g, b_router_g, w_router_e, b_router_e,
           w_exp_gate, w_exp_up, w_exp_down, g_final):
    nb, seq, d = x_prompt.shape
    ndec, dseq, _ = x_sample.shape
    depth = w_ada.shape[0]
    past_len = page_table.shape[1] * PAGE
    n_pool = cache_k.shape[1]
    a_w = N_HEADS * DV
    tm_s = ndec * dseq

    tables_p, tables_s8 = _rope_tables(seq, dseq, past_len)
    tables_s = tuple(jnp.tile(t, (ndec, 1)) for t in tables_s8)
    c_all = jnp.concatenate([c_prompt, c_sample], axis=0)
    c_pad = _round_up(c_all.shape[0], SUBLANES) - c_all.shape[0]
    c_all = jnp.pad(c_all, ((0, c_pad), (0, 0)))

    tri = jnp.tril(jnp.ones((CHUNK, CHUNK), f32))
    idx = jnp.arange(tm_s)
    mask_s = ((idx[:, None] // dseq == idx[None, :] // dseq) & (idx[None, :] % dseq <= idx[:, None] % dseq)).astype(f32)
    sel_s = (idx[:, None] % dseq == jnp.arange(dseq)[None, :]).astype(f32)

    xp, xs = x_prompt, x_sample.reshape(1, tm_s, d)
    kp_l, vp_l, ks_l, vs_l, cv_l = [], [], [], [], []
    for l in range(depth):
        lam_init = 0.8 - 0.6 * math.exp(-0.3 * l)
        mod = _adaln(c_all, w_ada[l], b_ada[l])
        mods_p = [mod[:nb, None, j * d:(j + 1) * d] for j in range(6)]
        mods_s = [jnp.repeat(mod[nb:nb + ndec, j * d:(j + 1) * d], dseq, axis=0)[None] for j in range(6)]
        wr_t = jnp.concatenate([w_router_g[l].T, jnp.zeros((SUBLANES - N_EXPERT_GROUPS, d), f32),
                                w_router_e[l].T], axis=0)
        br = jnp.concatenate([b_router_g[l], jnp.zeros((SUBLANES - N_EXPERT_GROUPS,), f32), b_router_e[l]])
        p = dict(
            g1=g_norm1[l].reshape(1, d), w_in=w_in[l].astype(bf16), g_v=g_v[l].reshape(1, -1),
            g_mlp=g_mlp[l].reshape(1, -1), w_out_a=w_out[l][:a_w].astype(bf16), w_out_b=w_out[l][a_w:].astype(bf16),
            g2=g_norm2[l].reshape(1, d), wr_t=wr_t, br_b=jnp.broadcast_to(br[:, None], (ROUTER_ROWS, LANES)),
            w_gate=w_exp_gate[l], w_up=w_exp_up[l], w_down=w_exp_down[l])
        gf = g_final.reshape(1, d)
        mix_p = (w_s[l], tri, jnp.broadcast_to(b_s[l][:, :, None], (N_GROUPS_B, CHUNK, C_B)))
        rep = lambda eq, *ops: jnp.einsum(eq, *ops, precision=lax.Precision.HIGHEST)
        mix_s = (rep("ia,gab,jb->gij", sel_s, w_s[l][:, :dseq, :dseq], sel_s), mask_s,
                 jnp.broadcast_to(rep("ia,ga->gi", sel_s, b_s[l][:, :dseq])[:, :, None], (N_GROUPS_B, tm_s, C_B)))

        ck = jnp.transpose(cache_k[l], (0, 2, 3, 4, 1)).reshape(n_pool, -1, PAGE)
        cv = cache_v[l].reshape(n_pool, PAGE * N_HEADS, DV)

        q_p, k_p, v_p, kb_p, vb_p, ob_p = _before_attention(xp, mods_p, tables_p, p, tm=TOKEN_TILE, mix=mix_p,
                                                            emit_vn=False)
        q_s, k_s, v_s, _, _, ob_s, cv_s = _before_attention(xs, mods_s, tables_s, p, tm=tm_s, mix=mix_s,
                                                            emit_vn=True)
        per_seq = lambda a: a.reshape(ndec, dseq, -1)
        oa_p, oa_s = _attention(page_table, lam_p[l], g_attn[l], q_p, kb_p, vb_p, per_seq(q_s), per_seq(k_s),
                                per_seq(v_s), ck, cv, lam_init)
        routed_p = _after_attention(xp, oa_p, ob_p, mods_p, p, tm=TOKEN_TILE)
        routed_s = _after_attention(xs, oa_s.reshape(1, tm_s, -1), ob_s, mods_s, p, tm=tm_s)
        xp, xs = _moe(routed_p, routed_s, gf, p["w_gate"], p["w_up"], p["w_down"], tm=TOKEN_TILE,
                      final=l == depth - 1)
        kp_l.append(k_p.reshape(nb, seq, N_HEADS, 2, DK))
        vp_l.append(v_p.reshape(nb, seq, N_HEADS, DV))
        ks_l.append(k_s.reshape(ndec, dseq, N_HEADS, 2, DK))
        vs_l.append(v_s.reshape(ndec, dseq, N_HEADS, DV))
        cv_l.append(cv_s.reshape(ndec, dseq, -1))
    return (xp, xs.reshape(ndec, dseq, d), jnp.stack(kp_l), jnp.stack(vp_l), jnp.stack(ks_l), jnp.stack(vs_l),
            jnp.stack(cv_l))
```

```python
import functools
import math

import jax
import jax.numpy as jnp
from jax import lax
from jax.experimental import pallas as pl
from jax.experimental.pallas import tpu as pltpu

f32 = jnp.float32
bf16 = jnp.bfloat16
i32 = jnp.int32
SDS = jax.ShapeDtypeStruct

N_HEADS = 4
DK = 64
DV = 2 * DK
ROT = DK // 4
ROPE_THETA = 500000.0
N_GROUPS_B = 4
C_B = 128
CHUNK = 128
N_EXPERT_GROUPS = 4
EXPERTS_PER_GROUP = 8
N_EXPERTS = N_EXPERT_GROUPS * EXPERTS_PER_GROUP
PAGE = 128
EPS = 1e-6
NEG = -1e30
LOG2E = math.log2(math.e)

LANES = 128
SUBLANES = 8
VMEM_LIMIT = 56 * 1024 * 1024
ATTN_VMEM_LIMIT = 60 * 1024 * 1024

TOKEN_TILE = 512
ATTN_TQ = 256
ATTN_TK = 256
ONES_ROWS = 16
PAGES_PER_STEP = 16
PAGE_SETS = 3
EXPERT_BLOCK = 256
X_AHEAD = 2
RUN_ALIGN = 8
CHUNK_ROWS = (32, 16, 8)
N_LIST = 3 * len(CHUNK_ROWS)
ROUTER_ROWS = 40


def _cparams(sem=None):
    return pltpu.CompilerParams(dimension_semantics=sem, vmem_limit_bytes=VMEM_LIMIT)


def _round_up(x, m):
    return (x + m - 1) // m * m


def _adaln_kernel(c_ref, w_ref, b_ref, o_ref):
    a = jax.nn.silu(c_ref[...])
    w = w_ref[...]
    a_hi = a.astype(bf16)
    a_lo = (a - a_hi.astype(f32)).astype(bf16)
    w_hi = w.astype(bf16)
    w_lo = (w - w_hi.astype(f32)).astype(bf16)
    n = a.shape[0]
    r = jnp.dot(jnp.concatenate([a_hi, a_lo], axis=0), w_hi, preferred_element_type=f32)
    o_ref[...] = r[:n] + r[n:] + jnp.dot(a_hi, w_lo, preferred_element_type=f32) + b_ref[...]


def _adaln(c_all, w_ada, b_ada):
    n, d = c_all.shape
    m = w_ada.shape[1]
    tn = 1536
    return pl.pallas_call(
        _adaln_kernel,
        grid=(m // tn,),
        in_specs=[pl.BlockSpec((n, d), lambda j: (0, 0)),
                  pl.BlockSpec((d, tn), lambda j: (0, j)),
                  pl.BlockSpec((1, tn), lambda j: (0, j))],
        out_specs=pl.BlockSpec((n, tn), lambda j: (0, j)),
        out_shape=SDS((n, m), f32),
        compiler_params=_cparams(("arbitrary",)),
        name="adaln",
    )(c_all, w_ada, b_ada.reshape(1, m))


def _rope_kernel(inv_ref, cos_ref, sin_ref, *, rows_prompt, past_len):
    shape = cos_ref.shape
    r = lax.broadcasted_iota(i32, shape, 0)
    l = lax.broadcasted_iota(i32, shape, 1)
    base = jnp.where(r < rows_prompt, r * 16, past_len + (r - rows_prompt) * 16)
    pos = base + (l >> 3)
    ang = pos.astype(f32) * inv_ref[...]
    cos_ref[...] = jnp.cos(ang)
    sin_ref[...] = jnp.sin(ang)


def _rope_tables(seq, dec_seq, past_len):
    inv = ROPE_THETA ** (-jnp.arange(0, ROT, 2, dtype=f32) / ROT)
    inv_lane = jnp.tile(inv, LANES // (ROT // 2)).reshape(1, LANES)
    rp = seq // 16
    rt = rp + SUBLANES
    cos_c, sin_c = pl.pallas_call(
        functools.partial(_rope_kernel, rows_prompt=rp, past_len=past_len),
        out_shape=(SDS((rt, LANES), f32), SDS((rt, LANES), f32)),
        name="rope_table",
    )(inv_lane)

    def expand(c8, s8):
        n = c8.shape[0]
        z8 = jnp.zeros((n, ROT // 2), f32)
        rest = DK - ROT
        cos_t = jnp.concatenate([c8, c8, jnp.ones((n, rest), f32)], axis=1)
        sin_a = jnp.concatenate([-s8, z8, jnp.zeros((n, rest), f32)], axis=1)
        sin_b = jnp.concatenate([z8, s8, jnp.zeros((n, rest), f32)], axis=1)
        return tuple(jnp.tile(t, (1, LANES // DK)) for t in (cos_t, sin_a, sin_b))

    half = ROT // 2
    prompt = expand(cos_c[:rp].reshape(seq, half), sin_c[:rp].reshape(seq, half))
    sample = expand(cos_c[rp].reshape(16, half)[:dec_seq], sin_c[rp].reshape(16, half)[:dec_seq])
    return prompt, sample


def _rms(x, g):
    return x * lax.rsqrt(jnp.mean(x * x, axis=-1, keepdims=True) + EPS) * g


def _in_proj_kernel(x_ref, sh_ref, sc_ref, g1_ref, w_ref, cos_ref, sa_ref, sb_ref,
                    ws_ref, msk_ref, bs_ref, gv_ref, gm_ref,
                    q_ref, k_ref, v_ref, kb_ref, vb_ref, ob_ref, *vn_refs, mix_rows):
    x = x_ref[0]
    tm = x.shape[0]
    h = _rms(x, g1_ref[...]) * (1.0 + sc_ref[0]) + sh_ref[0]
    z = jnp.dot(h.astype(bf16), w_ref[...], preferred_element_type=f32)
    qk_w = N_HEADS * 2 * DK
    a_w = N_HEADS * DV
    b_w = N_GROUPS_B * C_B
    cos_t, sin_a, sin_b = cos_ref[...], sa_ref[...], sb_ref[...]
    for s in range(2 * qk_w // LANES):
        zs = z[:, s * LANES:(s + 1) * LANES]
        rot = zs * cos_t + pltpu.roll(zs, LANES - ROT // 2, 1) * sin_a + pltpu.roll(zs, ROT // 2, 1) * sin_b
        if s < qk_w // LANES:
            q_ref[0, :, s * LANES:(s + 1) * LANES] = (rot * (DK ** -0.5 * LOG2E)).astype(bf16)
        else:
            k_ref[0, :, s * LANES - qk_w:(s + 1) * LANES - qk_w] = rot
            kb_ref[0, :, s * LANES - qk_w:(s + 1) * LANES - qk_w] = rot.astype(bf16)
    v = z[:, 2 * qk_w:2 * qk_w + a_w]
    for h in range(N_HEADS):
        v_ref[0, pl.ds(h, tm, stride=N_HEADS), :] = v[:, h * DV:(h + 1) * DV]
    vb_ref[0] = v.astype(bf16)
    uv = jax.nn.gelu(z[:, 2 * qk_w + a_w:])
    nblk = tm // mix_rows
    for g in range(N_GROUPS_B):
        u = uv[:, g * C_B:(g + 1) * C_B]
        vn = _rms(uv[:, b_w + g * C_B:b_w + (g + 1) * C_B], gv_ref[:, g * C_B:(g + 1) * C_B])
        if vn_refs:
            vn_refs[0][0, :, g * C_B:(g + 1) * C_B] = vn
        wm = (ws_ref[g] * msk_ref[...]).astype(bf16)
        vcat = jnp.concatenate([vn[j * mix_rows:(j + 1) * mix_rows] for j in range(nblk)], axis=1)
        mixed = jnp.dot(wm, vcat.astype(bf16), preferred_element_type=f32)
        for j in range(nblk):
            t = u[j * mix_rows:(j + 1) * mix_rows] * (mixed[:, j * C_B:(j + 1) * C_B] + bs_ref[g])
            ob_ref[0, j * mix_rows:(j + 1) * mix_rows, g * C_B:(g + 1) * C_B] = _rms(
                t, gm_ref[:, g * C_B:(g + 1) * C_B]).astype(bf16)


def _in_proj(x, shift, scale, g1, w_in_bf, tables, ws_t, mask, bs_b, g_v, g_mlp, *, tm, emit_vn):
    n, t, d = x.shape
    in_w = w_in_bf.shape[1]
    r = ws_t.shape[1]
    mod_rows = shift.shape[1]
    tab_rows = tables[0].shape[0]
    w_half = N_HEADS * DV

    def mod_spec():
        if mod_rows == 1:
            return pl.BlockSpec((1, 1, d), lambda b, i: (b, 0, 0))
        return pl.BlockSpec((1, tm, d), lambda b, i: (b, i, 0))

    def tab_spec():
        if tab_rows == tm:
            return pl.BlockSpec((tm, LANES), lambda b, i: (0, 0))
        return pl.BlockSpec((tm, LANES), lambda b, i: (i, 0))

    full = lambda shp: pl.BlockSpec(shp, lambda b, i: (0,) * len(shp))
    tok = lambda w: pl.BlockSpec((1, tm, w), lambda b, i: (b, i, 0))
    out_shape = [SDS((n, t, w_half), bf16), SDS((n, t, w_half), f32), SDS((n, t * N_HEADS, DV), f32),
                 SDS((n, t, w_half), bf16), SDS((n, t, w_half), bf16), SDS((n, t, w_half), bf16)]
    out_specs = [tok(w_half)] * 6
    out_specs[2] = pl.BlockSpec((1, tm * N_HEADS, DV), lambda b, i: (b, i, 0))
    if emit_vn:
        out_shape.append(SDS((n, t, w_half), f32))
        out_specs.append(tok(w_half))
    return pl.pallas_call(
        functools.partial(_in_proj_kernel, mix_rows=r),
        grid=(n, t // tm),
        in_specs=[tok(d), mod_spec(), mod_spec(), full((1, d)), full((d, in_w)),
                  tab_spec(), tab_spec(), tab_spec(),
                  full((N_GROUPS_B, r, r)), full((r, r)), full((N_GROUPS_B, r, C_B)),
                  full((1, w_half)), full((1, w_half))],
        out_specs=out_specs,
        out_shape=out_shape,
        compiler_params=_cparams(("arbitrary", "arbitrary")),
        name="in_proj",
    )(x, shift, scale, g1, w_in_bf, *tables, ws_t, mask, bs_b, g_v, g_mlp)


def _diff_lambda(lam_ref, lam_init):
    lp = lam_ref[...]
    s1 = jnp.sum(lp[0:1] * lp[1:2], axis=1, keepdims=True)
    s2 = jnp.sum(lp[2:3] * lp[3:4], axis=1, keepdims=True)
    return jnp.exp(s1) - jnp.exp(s2) + lam_init


def _attn_kernel(pt_ref, lam_ref, g_ref, q_ref, k_ref, v_ref, qs_ref, kn_ref, vn_ref, ck_ref, cv_ref,
                 o_ref, os_ref, vt_sc, qq_sc, acc_sc, m_sc, s_sc, qbd_sc, sm_sc, sl_sc, sacc_sc, kbuf, vbuf, sem,
                 *, lam_init, n_steps):
    b = pl.program_id(0)
    seq = q_ref.shape[1]
    tq, tk = ATTN_TQ, ATTN_TK
    n_qb = seq // tq
    lam = _diff_lambda(lam_ref, lam_init)
    heads = range(N_HEADS)
    nt = (((1,), (1,)), ((), ()))

    seq_per_step, ds, width = qs_ref.shape
    pps = kbuf.shape[1]
    groups = pt_ref.shape[1] // pps
    slots = seq_per_step * groups
    total_slots = n_steps * slots
    rows_h = 2 * ds
    rows = N_HEADS * rows_h

    assert groups & (groups - 1) == 0
    g_shift = groups.bit_length() - 1
    n_sets = kbuf.shape[0]
    g_first = b * slots

    def set_of(g):
        return lax.rem(g, n_sets)

    def page_copies(g, buf):
        sq = g >> g_shift
        first_page = (g & (groups - 1)) * pps
        out = []
        for i in range(pps):
            page = pt_ref[sq, first_page + i]
            out.append(pltpu.make_async_copy(ck_ref.at[page], kbuf.at[buf, i], sem.at[buf]))
            out.append(pltpu.make_async_copy(cv_ref.at[page], vbuf.at[buf, i], sem.at[buf]))
        return out

    def wait_pages(buf):
        for i in range(pps):
            pltpu.make_async_copy(ck_ref.at[0], kbuf.at[buf, i], sem.at[buf]).wait()
            pltpu.make_async_copy(cv_ref.at[0], vbuf.at[buf, i], sem.at[buf]).wait()

    def sample_update(s, value):
        m_prev = sm_sc[...]
        m_new = jnp.maximum(m_prev, jnp.max(s, axis=1, keepdims=True))
        alpha = jnp.exp2(m_prev - m_new)
        p = jnp.exp2(s - m_new)
        sl_sc[...] = alpha * sl_sc[...] + jnp.sum(p, axis=1, keepdims=True)
        pb = p.astype(bf16)
        n_blk = s.shape[1] // PAGE
        parts = [jnp.dot(pb[h * rows_h:(h + 1) * rows_h],
                         jnp.concatenate([value(i, h) for i in range(n_blk)], axis=0), preferred_element_type=f32)
                 for h in heads]
        sacc_sc[...] = sacc_sc[...] * alpha + jnp.concatenate(parts, axis=0)
        sm_sc[...] = m_new

    def slot_begin(t):
        g = g_first + t
        wait_pages(set_of(g))
        g_ahead = g + n_sets - 1
        for cp in page_copies(jnp.minimum(g_ahead, total_slots - 1), set_of(g_ahead)):
            cp.start()

        @pl.when(t & (groups - 1) == 0)
        def _():
            row_i = lax.broadcasted_iota(i32, (rows, width), 0)
            col_i = lax.broadcasted_iota(i32, (rows, width), 1)
            qt = jnp.concatenate([qs_ref[t >> g_shift].astype(f32)] * (N_HEADS * 2), axis=0)
            same_map = (col_i >> (DK.bit_length() - 1)) == (row_i >> (ds.bit_length() - 1))
            qbd_sc[...] = jnp.where(same_map, qt, 0.0).astype(bf16)
            sm_sc[...] = jnp.full(sm_sc.shape, NEG, f32)
            sl_sc[...] = jnp.zeros(sl_sc.shape, f32)
            sacc_sc[...] = jnp.zeros(sacc_sc.shape, f32)

    def slot_hooks(t, n_parts):
        buf = set_of(g_first + t)
        ppp = pps // n_parts
        carried = {}

        def scores_of(part):
            def run():
                k_all = jnp.concatenate([kbuf[buf, part * ppp + i].astype(bf16) for i in range(ppp)], axis=1)
                carried[part] = jnp.dot(qbd_sc[...], k_all, preferred_element_type=f32)
            return run

        def values_of(part):
            def run():
                sample_update(carried.pop(part), lambda i, h: vbuf[
                    buf, part * ppp + i, pl.ds(h, PAGE, stride=N_HEADS), :].astype(bf16))
            return run

        return [f(part) for part in range(n_parts) for f in (scores_of, values_of)]

    def slot_end(t):
        @pl.when(t & (groups - 1) == groups - 1)
        def _():
            sq = t >> g_shift
            qbd = qbd_sc[...]
            pad = jnp.zeros((PAGE - ds, width), f32)
            kn = jnp.concatenate([kn_ref[sq], pad], axis=0).astype(bf16)
            vn = jnp.concatenate([vn_ref[sq], pad], axis=0).astype(bf16)
            s_new = lax.dot_general(qbd, kn, nt, preferred_element_type=f32)
            key = lax.broadcasted_iota(i32, s_new.shape, 1)
            qry = lax.broadcasted_iota(i32, s_new.shape, 0) & (ds - 1)
            sample_update(jnp.where(key <= qry, s_new, NEG), lambda i, h: vn[:, h * DV:(h + 1) * DV])
            out = sacc_sc[...] * (1.0 / sl_sc[...])
            for h in heads:
                o = out[h * rows_h:h * rows_h + ds] - lam * out[h * rows_h + ds:(h + 1) * rows_h]
                os_ref[sq, :, h * DV:(h + 1) * DV] = _rms(o, g_ref[h:h + 1, :] * (1.0 - lam_init)).astype(bf16)

    @pl.when(b == 0)
    def _():
        for g in range(n_sets - 1):
            for cp in page_copies(min(g, total_slots - 1), g):
                cp.start()

    def prep(j, c):
        sl = pl.ds(pl.multiple_of(j * tk, tk), tk)
        for h in heads:
            vt_sc[h, j, :DV] = v_ref[0, sl, h * DV:(h + 1) * DV].T
            vt_sc[h, j, DV:] = jnp.ones((ONES_ROWS, tk), bf16)
        return c

    lax.fori_loop(0, seq // tk, prep, 0)
    lane = lax.broadcasted_iota(i32, (tq, DV), 1)

    def q_block(qi, t, slot_in_diag):
        qsl = pl.ds(pl.multiple_of(qi * tq, tq), tq)
        for h in heads:
            qb = q_ref[0, qsl, h * DV:(h + 1) * DV]
            zero = jnp.zeros_like(qb)
            qq_sc[h, :tq] = jnp.where(lane < DK, qb, zero)
            qq_sc[h, tq:] = jnp.where(lane >= DK, qb, zero)
            m_sc[h] = jnp.full(m_sc.shape[1:], NEG, f32)
            acc_sc[h] = jnp.zeros(acc_sc.shape[1:], f32)

        def scores(j, h):
            ksl = pl.ds(pl.multiple_of(j * tk, tk), tk)
            return lax.dot_general(k_ref[0, ksl, h * DV:(h + 1) * DV], qq_sc[h], nt, preferred_element_type=f32)

        s_sc[...] = scores(0, 0)

        def kv_block(j, masked, hooks=()):
            s_next = None
            for h in heads:
                s = s_sc[...] if h == 0 else s_next
                if h + 1 < N_HEADS:
                    s_next = scores(j, h + 1)
                elif not masked:
                    s_sc[...] = scores(j + 1, 0)
                if h < len(hooks):
                    hooks[h]()
                if masked:
                    key_i = lax.broadcasted_iota(i32, s.shape, 0)
                    qry_i = lax.broadcasted_iota(i32, s.shape, 1) & (tq - 1)
                    s = jnp.where(key_i <= qry_i, s, NEG)
                m_prev = m_sc[h]
                m_new = jnp.maximum(m_prev, jnp.max(s, axis=0, keepdims=True))
                alpha = jnp.exp2(m_prev - m_new)
                p = jnp.exp2(s - m_new)
                pv = jnp.dot(vt_sc[h, j], p.astype(bf16), preferred_element_type=f32)
                acc_sc[h] = acc_sc[h] * alpha + pv
                m_sc[h] = m_new

        def full_block(j, cc):
            kv_block(j, False)
            return cc

        idle = lambda: None

        def full_pair(jj, tt):
            slot_begin(tt)
            slot_scores, slot_values = slot_hooks(tt, 1)
            kv_block(2 * jj, False, [idle, idle, slot_scores])
            kv_block(2 * jj + 1, False)
            slot_values()
            slot_end(tt)
            return tt + 1

        pairs = qi >> 1
        t = lax.fori_loop(0, pairs, full_pair, t)
        lax.fori_loop(2 * pairs, qi, full_block, 0)
        if slot_in_diag:
            slot_begin(t)
            slot_scores, slot_values = slot_hooks(t, 1)
            slot_scores()
            kv_block(qi, True, [idle, idle, slot_values])
        else:
            kv_block(qi, True)
        for h in heads:
            acc = acc_sc[h]
            inv_l = 1.0 / acc[DV:DV + 1, :]
            o_t = acc[:DV, :tq] * inv_l[:, :tq] - lam * (acc[:DV, tq:] * inv_l[:, tq:])
            o_ref[0, qsl, h * DV:(h + 1) * DV] = _rms(o_t.T, g_ref[h:h + 1, :] * (1.0 - lam_init)).astype(bf16)
        if slot_in_diag:
            slot_end(t)
            t = t + 1
        return t

    pair_slots = sum(qi >> 1 for qi in range(n_qb))
    diag_slots = slots - pair_slots
    assert 0 <= diag_slots <= n_qb, (slots, pair_slots)
    t = lax.fori_loop(0, diag_slots, lambda qi, tt: q_block(qi, tt, True), 0)
    lax.fori_loop(diag_slots, n_qb, lambda qi, tt: q_block(qi, tt, False), t)

    @pl.when(b == n_steps - 1)
    def _():
        for g in range(total_slots, total_slots + n_sets - 1):
            wait_pages(g % n_sets)


def _attention(page_table, lam_p, g_attn, q, k, v, q_s, k_new, v_new, cache_kt, cache_vr, lam_init):
    n, seq, w = q.shape
    nseq, ds, _ = q_s.shape
    assert nseq % n == 0 and ds & (ds - 1) == 0 and page_table.shape[1] % PAGES_PER_STEP == 0
    sps = nseq // n
    rows = N_HEADS * 2 * ds
    blk = pl.BlockSpec((1, seq, w), lambda b, pt: (b, 0, 0))
    blk_in = pl.BlockSpec((1, seq, w), lambda b, pt: (b, 0, 0), pipeline_mode=pl.Buffered(1))
    sblk = pl.BlockSpec((sps, ds, w), lambda b, pt: (b, 0, 0))
    full = lambda a: pl.BlockSpec(a.shape, lambda b, pt: (0,) * a.ndim)
    any_spec = pl.BlockSpec(memory_space=pl.ANY)
    grid_spec = pltpu.PrefetchScalarGridSpec(
        num_scalar_prefetch=1,
        grid=(n,),
        in_specs=[full(lam_p), full(g_attn), blk_in, blk_in, blk_in, sblk, sblk, sblk, any_spec, any_spec],
        out_specs=[blk, sblk],
        scratch_shapes=[pltpu.VMEM((N_HEADS, seq // ATTN_TK, DV + ONES_ROWS, ATTN_TK), bf16),
                        pltpu.VMEM((N_HEADS, 2 * ATTN_TQ, DV), bf16),
                        pltpu.VMEM((N_HEADS, DV + ONES_ROWS, 2 * ATTN_TQ), f32),
                        pltpu.VMEM((N_HEADS, 1, 2 * ATTN_TQ), f32),
                        pltpu.VMEM((ATTN_TK, 2 * ATTN_TQ), f32),
                        pltpu.VMEM((rows, w), bf16), pltpu.VMEM((rows, 1), f32), pltpu.VMEM((rows, 1), f32),
                        pltpu.VMEM((rows, DV), f32),
                        pltpu.VMEM((PAGE_SETS, PAGES_PER_STEP) + cache_kt.shape[1:], f32),
                        pltpu.VMEM((PAGE_SETS, PAGES_PER_STEP) + cache_vr.shape[1:], f32),
                        pltpu.SemaphoreType.DMA((PAGE_SETS,))],
    )
    return pl.pallas_call(
        functools.partial(_attn_kernel, lam_init=lam_init, n_steps=n),
        grid_spec=grid_spec,
        out_shape=[SDS((n, seq, w), bf16), SDS((nseq, ds, w), bf16)],
        compiler_params=pltpu.CompilerParams(dimension_semantics=("arbitrary",), vmem_limit_bytes=ATTN_VMEM_LIMIT),
        name="attention",
    )(page_table, lam_p, g_attn, q, k, v, q_s, k_new, v_new, cache_kt, cache_vr)


def _out_route_kernel(oa_ref, ob_ref, x_ref, g1_ref, sh_ref, sc_ref, wa_ref, wb_ref, g2_ref, wr_ref, br_ref,
                      x1_ref, h2_ref, ids_ref, gates_ref, cnt_ref):
    mix = (jnp.dot(oa_ref[0], wa_ref[...], preferred_element_type=f32)
           + jnp.dot(ob_ref[0], wb_ref[...], preferred_element_type=f32))
    x1 = x_ref[0] + g1_ref[0] * mix
    x1_ref[0] = x1
    h2 = _rms(x1, g2_ref[...]) * (1.0 + sc_ref[0]) + sh_ref[0]
    h2_ref[0] = h2.astype(bf16)
    tm = h2.shape[0]
    wr = wr_ref[...]
    wr_hi = wr.astype(bf16)
    wr_lo = (wr - wr_hi.astype(f32)).astype(bf16)
    h_hi = h2.astype(bf16)
    h_lo = (h2 - h_hi.astype(f32)).astype(bf16)
    nt = (((1,), (1,)), ((), ()))
    r1 = lax.dot_general(jnp.concatenate([wr_hi, wr_lo], axis=0), h_hi, nt, preferred_element_type=f32)
    r2 = lax.dot_general(wr_hi, h_lo, nt, preferred_element_type=f32)
    lg = r1[:ROUTER_ROWS] + r1[ROUTER_ROWS:] + r2 + br_ref[:, 0:1]
    row = lax.broadcasted_iota(i32, (SUBLANES, tm), 0).astype(f32)
    big = float(SUBLANES)
    gl = jnp.where(row < N_EXPERT_GROUPS, lg[0:SUBLANES], NEG)
    gmax = jnp.max(gl, axis=0, keepdims=True)
    g_p = 1.0 / jnp.sum(jnp.exp(gl - gmax), axis=0, keepdims=True)
    gidx = jnp.min(jnp.where(gl == gmax, row, big), axis=0, keepdims=True)
    esel = jnp.zeros((SUBLANES, tm), f32)
    for g in range(N_EXPERT_GROUPS):
        esel = jnp.where(gidx == float(g), lg[SUBLANES * (g + 1):SUBLANES * (g + 2)], esel)
    e1 = jnp.max(esel, axis=0, keepdims=True)
    i1 = jnp.min(jnp.where(esel == e1, row, big), axis=0, keepdims=True)
    esel2 = jnp.where(row == i1, -jnp.inf, esel)
    e2 = jnp.max(esel2, axis=0, keepdims=True)
    i2 = jnp.min(jnp.where(esel2 == e2, row, big), axis=0, keepdims=True)
    t = jnp.exp(e2 - e1)
    w1 = g_p / (1.0 + t)
    ids = jnp.concatenate([gidx * EXPERTS_PER_GROUP + i1, gidx * EXPERTS_PER_GROUP + i2], axis=0).astype(i32)
    ids_ref[0] = ids
    gates_ref[0] = jnp.concatenate([w1, w1 * t], axis=0)
    e_iota = lax.broadcasted_iota(i32, (N_EXPERTS, tm), 0)
    onehot = jnp.where((e_iota == ids[0:1]) | (e_iota == ids[1:2]), 1.0, 0.0).astype(bf16)
    cnt_ref[0] = jnp.dot(onehot, jnp.ones((tm, LANES), bf16), preferred_element_type=f32)


def _out_route(oa, ob, x, gate1, shift2, scale2, w_a, w_b, g2, wr_t, br_b, *, tm):
    n, t, d = x.shape
    w_half = oa.shape[2]
    mod_rows = gate1.shape[1]
    tpn = t // tm

    def mod_spec():
        if mod_rows == 1:
            return pl.BlockSpec((1, 1, d), lambda b, i: (b, 0, 0))
        return pl.BlockSpec((1, tm, d), lambda b, i: (b, i, 0))

    full = lambda shp: pl.BlockSpec(shp, lambda b, i: (0,) * len(shp))
    tok = lambda w: pl.BlockSpec((1, tm, w), lambda b, i: (b, i, 0))
    return pl.pallas_call(
        _out_route_kernel,
        grid=(n, tpn),
        in_specs=[tok(w_half), tok(w_half), tok(d), mod_spec(), mod_spec(), mod_spec(),
                  full((w_half, d)), full((w_half, d)), full((1, d)),
                  full((ROUTER_ROWS, d)), full((ROUTER_ROWS, LANES))],
        out_specs=[tok(d), tok(d),
                   pl.BlockSpec((1, 2, tm), lambda b, i: (b * tpn + i, 0, 0)),
                   pl.BlockSpec((1, 2, tm), lambda b, i: (b * tpn + i, 0, 0)),
                   pl.BlockSpec((1, N_EXPERTS, LANES), lambda b, i: (b * tpn + i, 0, 0))],
        out_shape=[SDS((n, t, d), f32), SDS((n, t, d), bf16), SDS((n * tpn, 2, tm), i32),
                   SDS((n * tpn, 2, tm), f32), SDS((n * tpn, N_EXPERTS, LANES), f32)],
        compiler_params=_cparams(("arbitrary", "arbitrary")),
        name="out_route",
    )(oa, ob, x, gate1, shift2, scale2, w_a, w_b, g2, wr_t, br_b)


def _chunk_list(counts, src0, dst0, rows, kmax):
    cin = jnp.cumsum(counts, axis=1)
    cex = cin - counts
    k = jnp.arange(kmax, dtype=i32)
    owner = jnp.sum((cin[:, None, :] <= k[None, :, None]).astype(i32), axis=2)
    owner = jnp.minimum(owner, N_EXPERTS - 1)
    pick = (owner[:, :, None] == jnp.arange(N_EXPERTS, dtype=i32)).astype(i32)
    at_owner = lambda v: jnp.sum(pick * v[:, None, :], axis=2)
    within = (k[None, :] - at_owner(cex)) * rows
    src = at_owner(src0) + within
    dst = at_owner(dst0) + within
    return src.reshape(-1).astype(i32), dst.reshape(-1).astype(i32), cin[:, -1].astype(i32)


def _route_plan(cnt, nb_max, sorted_rows):
    pc = _round_up(cnt, RUN_ALIGN)
    toff = jnp.cumsum(pc, axis=1) - pc
    tot = jnp.sum(pc, axis=0)
    seg = _round_up(tot, EXPERT_BLOCK)
    seg_end = jnp.cumsum(seg)
    seg_start = seg_end - seg
    run_start = seg_start[None, :] + jnp.cumsum(pc, axis=0) - pc
    nb = seg_end[-1] // EXPERT_BLOCK
    lists = []
    done = jnp.zeros_like(pc)
    for rows in CHUNK_ROWS:
        n = (pc - done) // rows
        kmax = sorted_rows // rows if rows == CHUNK_ROWS[0] else N_EXPERTS
        lists.extend(_chunk_list(n, toff + done, run_start + done, rows, kmax))
        done = done + n * rows
    return dict(
        lists=lists,
        tail_start=(seg_start + tot).astype(i32), tail8=((seg - tot) // RUN_ALIGN).astype(i32),
        tail_tot8=(jnp.sum(seg - tot) // RUN_ALIGN).reshape(1).astype(i32),
        seg_start=seg_start.astype(i32), seg_blocks=(seg // EXPERT_BLOCK).astype(i32),
        nb=nb.reshape(1).astype(i32),
        toff_v=jnp.broadcast_to(toff.astype(f32)[:, :, None], toff.shape + (LANES,)))


def _sorted_positions(ids_ref, toff_ref, tm):
    ids = ids_ref[0]
    idc = jnp.concatenate([ids[0:1], ids[1:2]], axis=1)
    n_blk = 2 * tm // LANES
    e_iota = lax.broadcasted_iota(i32, (N_EXPERTS, 2 * tm), 0)
    onehot = jnp.where(e_iota == idc, 1.0, 0.0)
    stacked = jnp.concatenate([onehot[:, k * LANES:(k + 1) * LANES] for k in range(n_blk)], axis=0).astype(bf16)
    src = lax.broadcasted_iota(i32, (LANES, LANES), 0)
    dst = lax.broadcasted_iota(i32, (LANES, LANES), 1)
    within = jnp.dot(stacked, jnp.where(src < dst, 1.0, 0.0).astype(bf16), preferred_element_type=f32)
    totals = jnp.dot(stacked, jnp.ones((LANES, LANES), bf16), preferred_element_type=f32)
    r = lax.broadcasted_iota(i32, (n_blk * N_EXPERTS, n_blk * N_EXPERTS), 0)
    c = lax.broadcasted_iota(i32, (n_blk * N_EXPERTS, n_blk * N_EXPERTS), 1)
    same_expert_earlier = ((r & (N_EXPERTS - 1)) == (c & (N_EXPERTS - 1))) & (c < r)
    block_off = jnp.dot(jnp.where(same_expert_earlier, 1.0, 0.0).astype(bf16), totals.astype(bf16),
                        preferred_element_type=f32)
    before2 = within + block_off
    before = jnp.concatenate([before2[k * N_EXPERTS:(k + 1) * N_EXPERTS] for k in range(n_blk)], axis=1)
    pos = jnp.sum(onehot * (before + toff_ref[0][:, 0:1]), axis=0, keepdims=True)
    return jnp.where(idc >= 0, pos, -1.0)


def _aligned(row):
    return row if isinstance(row, int) else pl.multiple_of(row, RUN_ALIGN)


def _run_chunk_copy(vmem_buf, hbm_buf, sem, vrow, hrow, to_hbm, rows=RUN_ALIGN):
    v = vmem_buf.at[pl.ds(_aligned(vrow), rows)]
    h = hbm_buf.at[pl.ds(_aligned(hrow), rows)]
    return pltpu.make_async_copy(v, h, sem) if to_hbm else pltpu.make_async_copy(h, v, sem)


def _tile_chunks(i, list_refs, vmem_buf, hbm_buf, sems, to_hbm, wait):
    for c, rows in enumerate(CHUNK_ROWS):
        src_ref, dst_ref, n_ref = list_refs[3 * c:3 * c + 3]
        stride = src_ref.shape[0] // n_ref.shape[0]

        def one(k, carry, rows=rows, src_ref=src_ref, dst_ref=dst_ref, stride=stride, sem=sems.at[c]):
            if wait:
                _run_chunk_copy(vmem_buf, hbm_buf, sem, 0, 0, to_hbm, rows).wait()
            else:
                _run_chunk_copy(vmem_buf, hbm_buf, sem, src_ref[i * stride + k], dst_ref[i * stride + k],
                                to_hbm, rows).start()
            return carry

        lax.fori_loop(0, n_ref[i], one, 0)


def _dispatch_kernel(*refs):
    lists = refs[:N_LIST]
    tstart_ref, tail8_ref, ttot_ref, nb_ref = refs[N_LIST:N_LIST + 4]
    hm_ref, ht_ref, ids_ref, toff_ref, xs_ref, xbuf_sc, zero_sc, sem = refs[N_LIST + 4:]
    i = pl.program_id(0)
    last = pl.num_programs(0) - 1
    tm = hm_ref.shape[0]
    rt = xbuf_sc.shape[1]
    slot = i & 1
    sem_tail, sem_dead = sem.at[len(CHUNK_ROWS)], sem.at[len(CHUNK_ROWS) + 1]

    @pl.when(i == 0)
    def _():
        zero_sc[...] = jnp.zeros(zero_sc.shape, f32)

    h = jnp.where(i == last, ht_ref[...], hm_ref[...])
    pos = _sorted_positions(ids_ref, toff_ref, tm)
    r_iota = lax.broadcasted_iota(i32, (rt, tm), 0).astype(f32)
    perm = jnp.where((r_iota == pos[:, :tm]) | (r_iota == pos[:, tm:]), 1.0, 0.0).astype(bf16)
    xbuf_sc[slot] = jnp.dot(perm, h, preferred_element_type=f32)

    @pl.when(i > 0)
    def _():
        _tile_chunks(i - 1, lists, xbuf_sc.at[1 - slot], xs_ref, sem, True, wait=True)

    _tile_chunks(i, lists, xbuf_sc.at[slot], xs_ref, sem, True, wait=False)

    @pl.when(i == last)
    def _():
        _tile_chunks(i, lists, xbuf_sc.at[slot], xs_ref, sem, True, wait=True)

        def per_expert(e, c):
            def per_chunk(k, cc):
                _run_chunk_copy(zero_sc, xs_ref, sem_tail, 0, tstart_ref[e] + k * RUN_ALIGN, True).start()
                return cc
            lax.fori_loop(0, tail8_ref[e], per_chunk, 0)
            return c
        lax.fori_loop(0, N_EXPERTS, per_expert, 0)

        def wait_one(k, c):
            _run_chunk_copy(zero_sc, xs_ref, sem_tail, 0, 0, True).wait()
            return c
        lax.fori_loop(0, ttot_ref[0], wait_one, 0)

        def dead_block(row):
            return pltpu.make_async_copy(zero_sc, xs_ref.at[pl.ds(row, EXPERT_BLOCK)], sem_dead)

        n_dead = xs_ref.shape[0] // EXPERT_BLOCK - nb_ref[0]

        def start_dead(k, c):
            dead_block(pl.multiple_of((nb_ref[0] + k) * EXPERT_BLOCK, EXPERT_BLOCK)).start()
            return c
        lax.fori_loop(0, n_dead, start_dead, 0)

        def wait_dead(k, c):
            dead_block(0).wait()
            return c
        lax.fori_loop(0, n_dead, wait_dead, 0)


def _sorted_rows(tm):
    return _round_up(2 * tm + N_EXPERTS * (RUN_ALIGN - 1), LANES)


def _dispatch(plan, h_main, h_tail, ids, toff_v, rows_total):
    tiles, _, tm = ids.shape
    d = h_main.shape[1]
    tiles_main = h_main.shape[0] // tm
    assert tiles == tiles_main + 1 and h_tail.shape[0] == tm
    rt = _sorted_rows(tm)
    grid_spec = pltpu.PrefetchScalarGridSpec(
        num_scalar_prefetch=N_LIST + 4,
        grid=(tiles,),
        in_specs=[pl.BlockSpec((tm, d), lambda i, *_: (jnp.minimum(i, tiles_main - 1), 0)),
                  pl.BlockSpec((tm, d), lambda i, *_: (0, 0)),
                  pl.BlockSpec((1, 2, tm), lambda i, *_: (i, 0, 0)),
                  pl.BlockSpec((1, N_EXPERTS, LANES), lambda i, *_: (i, 0, 0))],
        out_specs=pl.BlockSpec(memory_space=pl.ANY),
        scratch_shapes=[pltpu.VMEM((2, rt, d), f32),
                        pltpu.VMEM((EXPERT_BLOCK, d), f32), pltpu.SemaphoreType.DMA((len(CHUNK_ROWS) + 2,))],
    )
    return pl.pallas_call(
        _dispatch_kernel,
        grid_spec=grid_spec,
        out_shape=SDS((rows_total, d), f32),
        compiler_params=_cparams(("arbitrary",)),
        name="dispatch",
    )(*plan["lists"], plan["tail_start"], plan["tail8"], plan["tail_tot8"], plan["nb"],
      h_main, h_tail, ids, toff_v)


def _experts_kernel(seg_ref, nblk_ref, nb_ref, x_ref, wg_ref, wu_ref, wd_ref, y_ref,
                    wg_sc, wu_sc, wd_sc, xbuf, ybuf, sem_in, sem_out):
    e = pl.program_id(0)
    eb = EXPERT_BLOCK
    nblk = nblk_ref[e]
    g0 = seg_ref[e] // eb
    n_all = nb_ref[0]

    def rows(g):
        return pl.ds(pl.multiple_of(g * eb, eb), eb)

    def x_copy(g):
        slot = lax.rem(g, X_AHEAD + 1)
        return pltpu.make_async_copy(x_ref.at[rows(g)], xbuf.at[slot], sem_in.at[slot])

    def y_copy(g):
        return pltpu.make_async_copy(ybuf.at[g & 1], y_ref.at[rows(g)], sem_out.at[g & 1])

    @pl.when(e == 0)
    def _():
        for g in range(X_AHEAD):
            @pl.when(g < n_all)
            def _():
                x_copy(g).start()

    @pl.when(nblk > 0)
    def _():
        wg_sc[...] = wg_ref[0].astype(bf16)
        wu_sc[...] = wu_ref[0].astype(bf16)
        wd_sc[...] = wd_ref[0].astype(bf16)

        def block(b, c):
            g = g0 + b
            x_copy(g).wait()

            @pl.when(g + X_AHEAD < n_all)
            def _():
                x_copy(g + X_AHEAD).start()

            xb = xbuf[lax.rem(g, X_AHEAD + 1)].astype(bf16)
            a = jnp.dot(xb, wg_sc[...], preferred_element_type=f32)
            u = jnp.dot(xb, wu_sc[...], preferred_element_type=f32)
            hid = (jax.nn.silu(a) * u).astype(bf16)
            y = jnp.dot(hid, wd_sc[...], preferred_element_type=f32)

            @pl.when(g >= 2)
            def _():
                y_copy(g - 2).wait()

            ybuf[g & 1] = y
            y_copy(g).start()
            return c

        lax.fori_loop(0, nblk, block, 0)

    @pl.when(e == pl.num_programs(0) - 1)
    def _():
        @pl.when(n_all >= 2)
        def _():
            y_copy(n_all - 2).wait()

        @pl.when(n_all >= 1)
        def _():
            y_copy(n_all - 1).wait()

        def dead_block(b):
            return pltpu.make_async_copy(ybuf.at[0], y_ref.at[pl.ds(pl.multiple_of(b * eb, eb), eb)], sem_out.at[0])

        n_dead = y_ref.shape[0] // eb - nb_ref[0]
        ybuf[0] = jnp.zeros(ybuf.shape[1:], f32)

        def start_dead(k, c):
            dead_block(nb_ref[0] + k).start()
            return c
        lax.fori_loop(0, n_dead, start_dead, 0)

        def wait_dead(k, c):
            dead_block(0).wait()
            return c
        lax.fori_loop(0, n_dead, wait_dead, 0)


def _experts(plan, xs, w_gate, w_up, w_down):
    rows_total, d = xs.shape
    n_exp, _, de = w_gate.shape
    eb = EXPERT_BLOCK
    grid_spec = pltpu.PrefetchScalarGridSpec(
        num_scalar_prefetch=3,
        grid=(n_exp,),
        in_specs=[pl.BlockSpec(memory_space=pl.ANY),
                  pl.BlockSpec((1, d, de), lambda e, *_: (e, 0, 0)),
                  pl.BlockSpec((1, d, de), lambda e, *_: (e, 0, 0)),
                  pl.BlockSpec((1, de, d), lambda e, *_: (e, 0, 0))],
        out_specs=pl.BlockSpec(memory_space=pl.ANY),
        scratch_shapes=[pltpu.VMEM((d, de), bf16), pltpu.VMEM((d, de), bf16), pltpu.VMEM((de, d), bf16),
                        pltpu.VMEM((X_AHEAD + 1, eb, d), f32), pltpu.VMEM((2, eb, d), f32),
                        pltpu.SemaphoreType.DMA((X_AHEAD + 1,)), pltpu.SemaphoreType.DMA((2,))],
    )
    return pl.pallas_call(
        _experts_kernel,
        grid_spec=grid_spec,
        out_shape=SDS((rows_total, d), f32),
        compiler_params=_cparams(("arbitrary",)),
        name="experts",
    )(plan["seg_start"], plan["seg_blocks"], plan["nb"], xs, w_gate, w_up, w_down)


def _combine_kernel(*refs, final):
    lists = refs[:N_LIST]
    (xm_ref, xt_ref, gm_ref, gt_ref, gf_ref, ids_ref, gates_ref, toff_ref, y_ref,
     om_ref, ot_ref, ybuf_sc, sem) = refs[N_LIST:]
    i = pl.program_id(0)
    last = pl.num_programs(0) - 1
    tm = xm_ref.shape[0]
    rt = ybuf_sc.shape[1]
    slot = i & 1

    @pl.when(i == 0)
    def _():
        ybuf_sc[...] = jnp.zeros(ybuf_sc.shape, f32)
        _tile_chunks(0, lists, ybuf_sc.at[0], y_ref, sem, False, wait=False)

    _tile_chunks(i, lists, ybuf_sc.at[slot], y_ref, sem, False, wait=True)

    @pl.when(i < last)
    def _():
        _tile_chunks(i + 1, lists, ybuf_sc.at[1 - slot], y_ref, sem, False, wait=False)

    pos = _sorted_positions(ids_ref, toff_ref, tm)
    gates = gates_ref[0]
    r_iota = lax.broadcasted_iota(i32, (rt, tm), 0).astype(f32)
    weights = (jnp.where(r_iota == pos[:, :tm], gates[0:1], 0.0)
               + jnp.where(r_iota == pos[:, tm:], gates[1:2], 0.0)).astype(bf16)
    ff = lax.dot_general(weights, ybuf_sc[slot].astype(bf16), (((0,), (0,)), ((), ())),
                         preferred_element_type=f32)

    def finish(x1, gate):
        x2 = x1 + gate * ff
        return _rms(x2, gf_ref[...]) if final else x2

    @pl.when(i < last)
    def _():
        om_ref[...] = finish(xm_ref[...], gm_ref[0])

    @pl.when(i == last)
    def _():
        ot_ref[...] = finish(xt_ref[...], gt_ref[...])


def _combine(plan, x_main, x_tail, gate_main, gate_tail, g_final, ids, gates, toff_v, y, *, final):
    tiles, _, tm = ids.shape
    d = x_main.shape[1]
    tiles_main = x_main.shape[0] // tm
    tpr = tiles_main // gate_main.shape[0]
    rt = _sorted_rows(tm)
    main_idx = lambda i, *_: (jnp.minimum(i, tiles_main - 1), 0)
    tail_spec = pl.BlockSpec((tm, d), lambda i, *_: (0, 0))
    slot_spec = pl.BlockSpec((1, 2, tm), lambda i, *_: (i, 0, 0))
    grid_spec = pltpu.PrefetchScalarGridSpec(
        num_scalar_prefetch=N_LIST,
        grid=(tiles,),
        in_specs=[pl.BlockSpec((tm, d), main_idx), tail_spec,
                  pl.BlockSpec((1, 1, d), lambda i, *_: (jnp.minimum(i, tiles_main - 1) // tpr, 0, 0)), tail_spec,
                  pl.BlockSpec((1, d), lambda i, *_: (0, 0)), slot_spec, slot_spec,
                  pl.BlockSpec((1, N_EXPERTS, LANES), lambda i, *_: (i, 0, 0)),
                  pl.BlockSpec(memory_space=pl.ANY)],
        out_specs=[pl.BlockSpec((tm, d), main_idx), tail_spec],
        scratch_shapes=[pltpu.VMEM((2, rt, d), f32), pltpu.SemaphoreType.DMA((len(CHUNK_ROWS),))],
    )
    return pl.pallas_call(
        functools.partial(_combine_kernel, final=final),
        grid_spec=grid_spec,
        out_shape=[SDS(x_main.shape, f32), SDS((tm, d), f32)],
        compiler_params=_cparams(("arbitrary",)),
        name="combine",
    )(*plan["lists"], x_main, x_tail, gate_main, gate_tail, g_final, ids, gates, toff_v, y)


def _moe(main, tail, g_final, w_gate, w_up, w_down, *, tm, final):
    n, t, d = main["x1"].shape
    t_tail = tail["x1"].shape[1]
    pad = tm - t_tail
    tiles = n * t // tm + 1
    rows_max = 2 * (n * t + t_tail) + tiles * N_EXPERTS * (RUN_ALIGN - 1) + N_EXPERTS * (EXPERT_BLOCK - 1)
    nb_max = -(-rows_max // EXPERT_BLOCK)
    rows2 = lambda a: jnp.pad(a.reshape(t_tail, d), ((0, pad), (0, 0)))
    ids = jnp.concatenate([main["ids"], jnp.pad(tail["ids"], ((0, 0), (0, 0), (0, pad)), constant_values=-1)])
    gates = jnp.concatenate([main["gates"], jnp.pad(tail["gates"], ((0, 0), (0, 0), (0, pad)))])
    cnt = jnp.concatenate([main["cnt"], tail["cnt"]])[:, :, 0].astype(i32)
    plan = _route_plan(cnt, nb_max, _sorted_rows(tm))
    xs = _dispatch(plan, main["h2"].reshape(n * t, d), rows2(tail["h2"]), ids, plan["toff_v"],
                   nb_max * EXPERT_BLOCK)
    y = _experts(plan, xs, w_gate, w_up, w_down)
    out_m, out_t = _combine(plan, main["x1"].reshape(n * t, d), rows2(tail["x1"]), main["gate"],
                            rows2(tail["gate"]), g_final, ids, gates, plan["toff_v"], y, final=final)
    return out_m.reshape(n, t, d), out_t[:t_tail].reshape(1, t_tail, d)


def _before_attention(x, mods, tables, p, *, tm, mix, emit_vn):
    ws_t, mask, bs_b = mix
    return _in_proj(x, mods[0], mods[1], p["g1"], p["w_in"], tables, ws_t, mask, bs_b, p["g_v"], p["g_mlp"],
                    tm=tm, emit_vn=emit_vn)


def _after_attention(x, oa, ob, mods, p, *, tm):
    x1, h2, ids, gates, cnt = _out_route(oa, ob, x, mods[2], mods[3], mods[4], p["w_out_a"], p["w_out_b"],
                                         p["g2"], p["wr_t"], p["br_b"], tm=tm)
    return dict(x1=x1, h2=h2, ids=ids, gates=gates, cnt=cnt, gate=mods[5])


def kernel(x_prompt, x_sample, cache_k, cache_v, page_table, c_prompt, c_sample, w_ada, b_ada, g_norm1, w_in,
           lam_p, g_attn, g_v, w_s, b_s, g_mlp, w_out, g_norm2, w_router_g, b_router_g, w_router_e, b_router_e,
           w_exp_gate, w_exp_up, w_exp_down, g_final):
    nb, seq, d = x_prompt.shape
    ndec, dseq, _ = x_sample.shape
    depth = w_ada.shape[0]
    past_len = page_table.shape[1] * PAGE
    n_pool = cache_k.shape[1]
    a_w = N_HEADS * DV
    tm_s = ndec * dseq

    tables_p, tables_s8 = _rope_tables(seq, dseq, past_len)
    tables_s = tuple(jnp.tile(t, (ndec, 1)) for t in tables_s8)
    c_all = jnp.concatenate([c_prompt, c_sample], axis=0)
    c_pad = _round_up(c_all.shape[0], SUBLANES) - c_all.shape[0]
    c_all = jnp.pad(c_all, ((0, c_pad), (0, 0)))

    tri = jnp.tril(jnp.ones((CHUNK, CHUNK), f32))
    idx = jnp.arange(tm_s)
    mask_s = ((idx[:, None] // dseq == idx[None, :] // dseq) & (idx[None, :] % dseq <= idx[:, None] % dseq)).astype(f32)
    sel_s = (idx[:, None] % dseq == jnp.arange(dseq)[None, :]).astype(f32)

    xp, xs = x_prompt, x_sample.reshape(1, tm_s, d)
    kp_l, vp_l, ks_l, vs_l, cv_l = [], [], [], [], []
    for l in range(depth):
        lam_init = 0.8 - 0.6 * math.exp(-0.3 * l)
        mod = _adaln(c_all, w_ada[l], b_ada[l])
        mods_p = [mod[:nb, None, j * d:(j + 1) * d] for j in range(6)]
        mods_s = [jnp.repeat(mod[nb:nb + ndec, j * d:(j + 1) * d], dseq, axis=0)[None] for j in range(6)]
        wr_t = jnp.concatenate([w_router_g[l].T, jnp.zeros((SUBLANES - N_EXPERT_GROUPS, d), f32),
                                w_router_e[l].T], axis=0)
        br = jnp.concatenate([b_router_g[l], jnp.zeros((SUBLANES - N_EXPERT_GROUPS,), f32), b_router_e[l]])
        p = dict(
            g1=g_norm1[l].reshape(1, d), w_in=w_in[l].astype(bf16), g_v=g_v[l].reshape(1, -1),
            g_mlp=g_mlp[l].reshape(1, -1), w_out_a=w_out[l][:a_w].astype(bf16), w_out_b=w_out[l][a_w:].astype(bf16),
            g2=g_norm2[l].reshape(1, d), wr_t=wr_t, br_b=jnp.broadcast_to(br[:, None], (ROUTER_ROWS, LANES)),
            w_gate=w_exp_gate[l], w_up=w_exp_up[l], w_down=w_exp_down[l])
        gf = g_final.reshape(1, d)
        mix_p = (w_s[l], tri, jnp.broadcast_to(b_s[l][:, :, None], (N_GROUPS_B, CHUNK, C_B)))
        rep = lambda eq, *ops: jnp.einsum(eq, *ops, precision=lax.Precision.HIGHEST)
        mix_s = (rep("ia,gab,jb->gij", sel_s, w_s[l][:, :dseq, :dseq], sel_s), mask_s,
                 jnp.broadcast_to(rep("ia,ga->gi", sel_s, b_s[l][:, :dseq])[:, :, None], (N_GROUPS_B, tm_s, C_B)))

        ck = jnp.transpose(cache_k[l], (0, 2, 3, 4, 1)).reshape(n_pool, -1, PAGE)
        cv = cache_v[l].reshape(n_pool, PAGE * N_HEADS, DV)

        q_p, k_p, v_p, kb_p, vb_p, ob_p = _before_attention(xp, mods_p, tables_p, p, tm=TOKEN_TILE, mix=mix_p,
                                                            emit_vn=False)
        q_s, k_s, v_s, _, _, ob_s, cv_s = _before_attention(xs, mods_s, tables_s, p, tm=tm_s, mix=mix_s,
                                                            emit_vn=True)
        per_seq = lambda a: a.reshape(ndec, dseq, -1)
        oa_p, oa_s = _attention(page_table, lam_p[l], g_attn[l], q_p, kb_p, vb_p, per_seq(q_s), per_seq(k_s),
                                per_seq(v_s), ck, cv, lam_init)
        routed_p = _after_attention(xp, oa_p, ob_p, mods_p, p, tm=TOKEN_TILE)
        routed_s = _after_attention(xs, oa_s.reshape(1, tm_s, -1), ob_s, mods_s, p, tm=tm_s)
        xp, xs = _moe(routed_p, routed_s, gf, p["w_gate"], p["w_up"], p["w_down"], tm=TOKEN_TILE,
                      final=l == depth - 1)
        kp_l.append(k_p.reshape(nb, seq, N_HEADS, 2, DK))
        vp_l.append(v_p.reshape(nb, seq, N_HEADS, DV))
        ks_l.append(k_s.reshape(ndec, dseq, N_HEADS, 2, DK))
        vs_l.append(v_s.reshape(ndec, dseq, N_HEADS, DV))
        cv_l.append(cv_s.reshape(ndec, dseq, -1))
    return (xp, xs.reshape(ndec, dseq, d), jnp.stack(kp_l), jnp.stack(vp_l), jnp.stack(ks_l), jnp.stack(vs_l),
            jnp.stack(cv_l))
```

```python
import functools
import math

import jax
import jax.numpy as jnp
from jax import lax
from jax.experimental import pallas as pl
from jax.experimental.pallas import tpu as pltpu

f32 = jnp.float32
bf16 = jnp.bfloat16
i32 = jnp.int32
SDS = jax.ShapeDtypeStruct

N_HEADS = 4
DK = 64
DV = 2 * DK
ROT = DK // 4
ROPE_THETA = 500000.0
N_GROUPS_B = 4
C_B = 128
CHUNK = 128
N_EXPERT_GROUPS = 4
EXPERTS_PER_GROUP = 8
N_EXPERTS = N_EXPERT_GROUPS * EXPERTS_PER_GROUP
PAGE = 128
EPS = 1e-6
NEG = -1e30
LOG2E = math.log2(math.e)

LANES = 128
SUBLANES = 8
VMEM_LIMIT = 56 * 1024 * 1024
ATTN_VMEM_LIMIT = 60 * 1024 * 1024

TOKEN_TILE = 512
ATTN_TQ = 256
ATTN_TK = 256
ONES_ROWS = 16
PAGES_PER_STEP = 16
PAGE_SETS = 3
EXPERT_BLOCK = 256
X_AHEAD = 2
RUN_ALIGN = 16
CHUNK_ROWS = (32, 16)
ROW_DTYPE = bf16
N_LIST = 3 * len(CHUNK_ROWS)
ROUTER_ROWS = 40


def _cparams(sem=None):
    return pltpu.CompilerParams(dimension_semantics=sem, vmem_limit_bytes=VMEM_LIMIT)


def _round_up(x, m):
    return (x + m - 1) // m * m


def _adaln_kernel(c_ref, w_ref, b_ref, o_ref):
    a = jax.nn.silu(c_ref[...])
    w = w_ref[...]
    a_hi = a.astype(bf16)
    a_lo = (a - a_hi.astype(f32)).astype(bf16)
    w_hi = w.astype(bf16)
    w_lo = (w - w_hi.astype(f32)).astype(bf16)
    n = a.shape[0]
    r = jnp.dot(jnp.concatenate([a_hi, a_lo], axis=0), w_hi, preferred_element_type=f32)
    o_ref[...] = r[:n] + r[n:] + jnp.dot(a_hi, w_lo, preferred_element_type=f32) + b_ref[...]


def _adaln(c_all, w_ada, b_ada):
    n, d = c_all.shape
    m = w_ada.shape[1]
    tn = 1536
    return pl.pallas_call(
        _adaln_kernel,
        grid=(m // tn,),
        in_specs=[pl.BlockSpec((n, d), lambda j: (0, 0)),
                  pl.BlockSpec((d, tn), lambda j: (0, j)),
                  pl.BlockSpec((1, tn), lambda j: (0, j))],
        out_specs=pl.BlockSpec((n, tn), lambda j: (0, j)),
        out_shape=SDS((n, m), f32),
        compiler_params=_cparams(("arbitrary",)),
        name="adaln",
    )(c_all, w_ada, b_ada.reshape(1, m))


def _rope_kernel(inv_ref, cos_ref, sin_ref, *, rows_prompt, past_len):
    shape = cos_ref.shape
    r = lax.broadcasted_iota(i32, shape, 0)
    l = lax.broadcasted_iota(i32, shape, 1)
    base = jnp.where(r < rows_prompt, r * 16, past_len + (r - rows_prompt) * 16)
    pos = base + (l >> 3)
    ang = pos.astype(f32) * inv_ref[...]
    cos_ref[...] = jnp.cos(ang)
    sin_ref[...] = jnp.sin(ang)


def _rope_tables(seq, dec_seq, past_len):
    inv = ROPE_THETA ** (-jnp.arange(0, ROT, 2, dtype=f32) / ROT)
    inv_lane = jnp.tile(inv, LANES // (ROT // 2)).reshape(1, LANES)
    rp = seq // 16
    rt = rp + SUBLANES
    cos_c, sin_c = pl.pallas_call(
        functools.partial(_rope_kernel, rows_prompt=rp, past_len=past_len),
        out_shape=(SDS((rt, LANES), f32), SDS((rt, LANES), f32)),
        name="rope_table",
    )(inv_lane)

    def expand(c8, s8):
        n = c8.shape[0]
        z8 = jnp.zeros((n, ROT // 2), f32)
        rest = DK - ROT
        cos_t = jnp.concatenate([c8, c8, jnp.ones((n, rest), f32)], axis=1)
        sin_a = jnp.concatenate([-s8, z8, jnp.zeros((n, rest), f32)], axis=1)
        sin_b = jnp.concatenate([z8, s8, jnp.zeros((n, rest), f32)], axis=1)
        return tuple(jnp.tile(t, (1, LANES // DK)) for t in (cos_t, sin_a, sin_b))

    half = ROT // 2
    prompt = expand(cos_c[:rp].reshape(seq, half), sin_c[:rp].reshape(seq, half))
    sample = expand(cos_c[rp].reshape(16, half)[:dec_seq], sin_c[rp].reshape(16, half)[:dec_seq])
    return prompt, sample


def _rms(x, g):
    return x * lax.rsqrt(jnp.mean(x * x, axis=-1, keepdims=True) + EPS) * g


def _in_proj_kernel(x_ref, sh_ref, sc_ref, g1_ref, w_ref, cos_ref, sa_ref, sb_ref,
                    ws_ref, msk_ref, bs_ref, gv_ref, gm_ref,
                    q_ref, k_ref, v_ref, kb_ref, vb_ref, ob_ref, *vn_refs, mix_rows):
    x = x_ref[0]
    tm = x.shape[0]
    h = _rms(x, g1_ref[...]) * (1.0 + sc_ref[0]) + sh_ref[0]
    z = jnp.dot(h.astype(bf16), w_ref[...], preferred_element_type=f32)
    qk_w = N_HEADS * 2 * DK
    a_w = N_HEADS * DV
    b_w = N_GROUPS_B * C_B
    cos_t, sin_a, sin_b = cos_ref[...], sa_ref[...], sb_ref[...]
    for s in range(2 * qk_w // LANES):
        zs = z[:, s * LANES:(s + 1) * LANES]
        rot = zs * cos_t + pltpu.roll(zs, LANES - ROT // 2, 1) * sin_a + pltpu.roll(zs, ROT // 2, 1) * sin_b
        if s < qk_w // LANES:
            q_ref[0, :, s * LANES:(s + 1) * LANES] = (rot * (DK ** -0.5 * LOG2E)).astype(bf16)
        else:
            k_ref[0, :, s * LANES - qk_w:(s + 1) * LANES - qk_w] = rot
            kb_ref[0, :, s * LANES - qk_w:(s + 1) * LANES - qk_w] = rot.astype(bf16)
    v = z[:, 2 * qk_w:2 * qk_w + a_w]
    for h in range(N_HEADS):
        v_ref[0, pl.ds(h, tm, stride=N_HEADS), :] = v[:, h * DV:(h + 1) * DV]
    vb_ref[0] = v.astype(bf16)
    uv = jax.nn.gelu(z[:, 2 * qk_w + a_w:])
    nblk = tm // mix_rows
    for g in range(N_GROUPS_B):
        u = uv[:, g * C_B:(g + 1) * C_B]
        vn = _rms(uv[:, b_w + g * C_B:b_w + (g + 1) * C_B], gv_ref[:, g * C_B:(g + 1) * C_B])
        if vn_refs:
            vn_refs[0][0, :, g * C_B:(g + 1) * C_B] = vn
        wm = (ws_ref[g] * msk_ref[...]).astype(bf16)
        vcat = jnp.concatenate([vn[j * mix_rows:(j + 1) * mix_rows] for j in range(nblk)], axis=1)
        mixed = jnp.dot(wm, vcat.astype(bf16), preferred_element_type=f32)
        for j in range(nblk):
            t = u[j * mix_rows:(j + 1) * mix_rows] * (mixed[:, j * C_B:(j + 1) * C_B] + bs_ref[g])
            ob_ref[0, j * mix_rows:(j + 1) * mix_rows, g * C_B:(g + 1) * C_B] = _rms(
                t, gm_ref[:, g * C_B:(g + 1) * C_B]).astype(bf16)


def _in_proj(x, shift, scale, g1, w_in_bf, tables, ws_t, mask, bs_b, g_v, g_mlp, *, tm, emit_vn):
    n, t, d = x.shape
    in_w = w_in_bf.shape[1]
    r = ws_t.shape[1]
    mod_rows = shift.shape[1]
    tab_rows = tables[0].shape[0]
    w_half = N_HEADS * DV

    def mod_spec():
        if mod_rows == 1:
            return pl.BlockSpec((1, 1, d), lambda b, i: (b, 0, 0))
        return pl.BlockSpec((1, tm, d), lambda b, i: (b, i, 0))

    def tab_spec():
        if tab_rows == tm:
            return pl.BlockSpec((tm, LANES), lambda b, i: (0, 0))
        return pl.BlockSpec((tm, LANES), lambda b, i: (i, 0))

    full = lambda shp: pl.BlockSpec(shp, lambda b, i: (0,) * len(shp))
    tok = lambda w: pl.BlockSpec((1, tm, w), lambda b, i: (b, i, 0))
    out_shape = [SDS((n, t, w_half), bf16), SDS((n, t, w_half), f32), SDS((n, t * N_HEADS, DV), f32),
                 SDS((n, t, w_half), bf16), SDS((n, t, w_half), bf16), SDS((n, t, w_half), bf16)]
    out_specs = [tok(w_half)] * 6
    out_specs[2] = pl.BlockSpec((1, tm * N_HEADS, DV), lambda b, i: (b, i, 0))
    if emit_vn:
        out_shape.append(SDS((n, t, w_half), f32))
        out_specs.append(tok(w_half))
    return pl.pallas_call(
        functools.partial(_in_proj_kernel, mix_rows=r),
        grid=(n, t // tm),
        in_specs=[tok(d), mod_spec(), mod_spec(), full((1, d)), full((d, in_w)),
                  tab_spec(), tab_spec(), tab_spec(),
                  full((N_GROUPS_B, r, r)), full((r, r)), full((N_GROUPS_B, r, C_B)),
                  full((1, w_half)), full((1, w_half))],
        out_specs=out_specs,
        out_shape=out_shape,
        compiler_params=_cparams(("arbitrary", "arbitrary")),
        name="in_proj",
    )(x, shift, scale, g1, w_in_bf, *tables, ws_t, mask, bs_b, g_v, g_mlp)


def _diff_lambda(lam_ref, lam_init):
    lp = lam_ref[...]
    s1 = jnp.sum(lp[0:1] * lp[1:2], axis=1, keepdims=True)
    s2 = jnp.sum(lp[2:3] * lp[3:4], axis=1, keepdims=True)
    return jnp.exp(s1) - jnp.exp(s2) + lam_init


def _attn_kernel(pt_ref, lam_ref, g_ref, q_ref, k_ref, v_ref, qs_ref, kn_ref, vn_ref, ck_ref, cv_ref,
                 o_ref, os_ref, vt_sc, qq_sc, acc_sc, m_sc, s_sc, qbd_sc, sm_sc, sl_sc, sacc_sc, kbuf, vbuf, sem,
                 *, lam_init, n_steps):
    b = pl.program_id(0)
    seq = q_ref.shape[1]
    tq, tk = ATTN_TQ, ATTN_TK
    n_qb = seq // tq
    lam = _diff_lambda(lam_ref, lam_init)
    heads = range(N_HEADS)
    nt = (((1,), (1,)), ((), ()))

    seq_per_step, ds, width = qs_ref.shape
    pps = kbuf.shape[1]
    groups = pt_ref.shape[1] // pps
    slots = seq_per_step * groups
    total_slots = n_steps * slots
    rows_h = 2 * ds
    rows = N_HEADS * rows_h

    assert groups & (groups - 1) == 0
    g_shift = groups.bit_length() - 1
    n_sets = kbuf.shape[0]
    g_first = b * slots

    def set_of(g):
        return lax.rem(g, n_sets)

    def page_copies(g, buf):
        sq = g >> g_shift
        first_page = (g & (groups - 1)) * pps
        out = []
        for i in range(pps):
            page = pt_ref[sq, first_page + i]
            out.append(pltpu.make_async_copy(ck_ref.at[page], kbuf.at[buf, i], sem.at[buf]))
            out.append(pltpu.make_async_copy(cv_ref.at[page], vbuf.at[buf, i], sem.at[buf]))
        return out

    def wait_pages(buf):
        for i in range(pps):
            pltpu.make_async_copy(ck_ref.at[0], kbuf.at[buf, i], sem.at[buf]).wait()
            pltpu.make_async_copy(cv_ref.at[0], vbuf.at[buf, i], sem.at[buf]).wait()

    def sample_update(s, value):
        m_prev = sm_sc[...]
        m_new = jnp.maximum(m_prev, jnp.max(s, axis=1, keepdims=True))
        alpha = jnp.exp2(m_prev - m_new)
        p = jnp.exp2(s - m_new)
        sl_sc[...] = alpha * sl_sc[...] + jnp.sum(p, axis=1, keepdims=True)
        pb = p.astype(bf16)
        n_blk = s.shape[1] // PAGE
        parts = [jnp.dot(pb[h * rows_h:(h + 1) * rows_h],
                         jnp.concatenate([value(i, h) for i in range(n_blk)], axis=0), preferred_element_type=f32)
                 for h in heads]
        sacc_sc[...] = sacc_sc[...] * alpha + jnp.concatenate(parts, axis=0)
        sm_sc[...] = m_new

    def slot_begin(t):
        g = g_first + t
        wait_pages(set_of(g))
        g_ahead = g + n_sets - 1
        for cp in page_copies(jnp.minimum(g_ahead, total_slots - 1), set_of(g_ahead)):
            cp.start()

        @pl.when(t & (groups - 1) == 0)
        def _():
            row_i = lax.broadcasted_iota(i32, (rows, width), 0)
            col_i = lax.broadcasted_iota(i32, (rows, width), 1)
            qt = jnp.concatenate([qs_ref[t >> g_shift].astype(f32)] * (N_HEADS * 2), axis=0)
            same_map = (col_i >> (DK.bit_length() - 1)) == (row_i >> (ds.bit_length() - 1))
            qbd_sc[...] = jnp.where(same_map, qt, 0.0).astype(bf16)
            sm_sc[...] = jnp.full(sm_sc.shape, NEG, f32)
            sl_sc[...] = jnp.zeros(sl_sc.shape, f32)
            sacc_sc[...] = jnp.zeros(sacc_sc.shape, f32)

    def slot_hooks(t, n_parts):
        buf = set_of(g_first + t)
        ppp = pps // n_parts
        carried = {}

        def scores_of(part):
            def run():
                k_all = jnp.concatenate([kbuf[buf, part * ppp + i].astype(bf16) for i in range(ppp)], axis=1)
                carried[part] = jnp.dot(qbd_sc[...], k_all, preferred_element_type=f32)
            return run

        def values_of(part):
            def run():
                sample_update(carried.pop(part), lambda i, h: vbuf[
                    buf, part * ppp + i, pl.ds(h, PAGE, stride=N_HEADS), :].astype(bf16))
            return run

        return [f(part) for part in range(n_parts) for f in (scores_of, values_of)]

    def slot_end(t):
        @pl.when(t & (groups - 1) == groups - 1)
        def _():
            sq = t >> g_shift
            qbd = qbd_sc[...]
            pad = jnp.zeros((PAGE - ds, width), f32)
            kn = jnp.concatenate([kn_ref[sq], pad], axis=0).astype(bf16)
            vn = jnp.concatenate([vn_ref[sq], pad], axis=0).astype(bf16)
            s_new = lax.dot_general(qbd, kn, nt, preferred_element_type=f32)
            key = lax.broadcasted_iota(i32, s_new.shape, 1)
            qry = lax.broadcasted_iota(i32, s_new.shape, 0) & (ds - 1)
            sample_update(jnp.where(key <= qry, s_new, NEG), lambda i, h: vn[:, h * DV:(h + 1) * DV])
            out = sacc_sc[...] * (1.0 / sl_sc[...])
            for h in heads:
                o = out[h * rows_h:h * rows_h + ds] - lam * out[h * rows_h + ds:(h + 1) * rows_h]
                os_ref[sq, :, h * DV:(h + 1) * DV] = _rms(o, g_ref[h:h + 1, :] * (1.0 - lam_init)).astype(bf16)

    @pl.when(b == 0)
    def _():
        for g in range(n_sets - 1):
            for cp in page_copies(min(g, total_slots - 1), g):
                cp.start()

    def prep(j, c):
        sl = pl.ds(pl.multiple_of(j * tk, tk), tk)
        for h in heads:
            vt_sc[h, j, :DV] = v_ref[0, sl, h * DV:(h + 1) * DV].T
            vt_sc[h, j, DV:] = jnp.ones((ONES_ROWS, tk), bf16)
        return c

    lax.fori_loop(0, seq // tk, prep, 0)
    lane = lax.broadcasted_iota(i32, (tq, DV), 1)

    def q_block(qi, t, slot_in_diag):
        qsl = pl.ds(pl.multiple_of(qi * tq, tq), tq)
        for h in heads:
            qb = q_ref[0, qsl, h * DV:(h + 1) * DV]
            zero = jnp.zeros_like(qb)
            qq_sc[h, :tq] = jnp.where(lane < DK, qb, zero)
            qq_sc[h, tq:] = jnp.where(lane >= DK, qb, zero)
            m_sc[h] = jnp.full(m_sc.shape[1:], NEG, f32)
            acc_sc[h] = jnp.zeros(acc_sc.shape[1:], f32)

        def scores(j, h):
            ksl = pl.ds(pl.multiple_of(j * tk, tk), tk)
            return lax.dot_general(k_ref[0, ksl, h * DV:(h + 1) * DV], qq_sc[h], nt, preferred_element_type=f32)

        s_sc[...] = scores(0, 0)

        def kv_block(j, masked, hooks=()):
            s_next = None
            for h in heads:
                s = s_sc[...] if h == 0 else s_next
                if h + 1 < N_HEADS:
                    s_next = scores(j, h + 1)
                elif not masked:
                    s_sc[...] = scores(j + 1, 0)
                if h < len(hooks):
                    hooks[h]()
                if masked:
                    key_i = lax.broadcasted_iota(i32, s.shape, 0)
                    qry_i = lax.broadcasted_iota(i32, s.shape, 1) & (tq - 1)
                    s = jnp.where(key_i <= qry_i, s, NEG)
                m_prev = m_sc[h]
                m_new = jnp.maximum(m_prev, jnp.max(s, axis=0, keepdims=True))
                alpha = jnp.exp2(m_prev - m_new)
                p = jnp.exp2(s - m_new)
                pv = jnp.dot(vt_sc[h, j], p.astype(bf16), preferred_element_type=f32)
                acc_sc[h] = acc_sc[h] * alpha + pv
                m_sc[h] = m_new

        def full_block(j, cc):
            kv_block(j, False)
            return cc

        idle = lambda: None

        def full_pair(jj, tt):
            slot_begin(tt)
            slot_scores, slot_values = slot_hooks(tt, 1)
            kv_block(2 * jj, False, [idle, idle, slot_scores])
            kv_block(2 * jj + 1, False)
            slot_values()
            slot_end(tt)
            return tt + 1

        pairs = qi >> 1
        t = lax.fori_loop(0, pairs, full_pair, t)
        lax.fori_loop(2 * pairs, qi, full_block, 0)
        if slot_in_diag:
            slot_begin(t)
            slot_scores, slot_values = slot_hooks(t, 1)
            slot_scores()
            kv_block(qi, True, [idle, idle, slot_values])
        else:
            kv_block(qi, True)
        for h in heads:
            acc = acc_sc[h]
            inv_l = 1.0 / acc[DV:DV + 1, :]
            o_t = acc[:DV, :tq] * inv_l[:, :tq] - lam * (acc[:DV, tq:] * inv_l[:, tq:])
            o_ref[0, qsl, h * DV:(h + 1) * DV] = _rms(o_t.T, g_ref[h:h + 1, :] * (1.0 - lam_init)).astype(bf16)
        if slot_in_diag:
            slot_end(t)
            t = t + 1
        return t

    pair_slots = sum(qi >> 1 for qi in range(n_qb))
    diag_slots = slots - pair_slots
    assert 0 <= diag_slots <= n_qb, (slots, pair_slots)
    t = lax.fori_loop(0, diag_slots, lambda qi, tt: q_block(qi, tt, True), 0)
    lax.fori_loop(diag_slots, n_qb, lambda qi, tt: q_block(qi, tt, False), t)

    @pl.when(b == n_steps - 1)
    def _():
        for g in range(total_slots, total_slots + n_sets - 1):
            wait_pages(g % n_sets)


def _attention(page_table, lam_p, g_attn, q, k, v, q_s, k_new, v_new, cache_kt, cache_vr, lam_init):
    n, seq, w = q.shape
    nseq, ds, _ = q_s.shape
    assert nseq % n == 0 and ds & (ds - 1) == 0 and page_table.shape[1] % PAGES_PER_STEP == 0
    sps = nseq // n
    rows = N_HEADS * 2 * ds
    blk = pl.BlockSpec((1, seq, w), lambda b, pt: (b, 0, 0))
    blk_in = pl.BlockSpec((1, seq, w), lambda b, pt: (b, 0, 0), pipeline_mode=pl.Buffered(1))
    sblk = pl.BlockSpec((sps, ds, w), lambda b, pt: (b, 0, 0))
    full = lambda a: pl.BlockSpec(a.shape, lambda b, pt: (0,) * a.ndim)
    any_spec = pl.BlockSpec(memory_space=pl.ANY)
    grid_spec = pltpu.PrefetchScalarGridSpec(
        num_scalar_prefetch=1,
        grid=(n,),
        in_specs=[full(lam_p), full(g_attn), blk_in, blk_in, blk_in, sblk, sblk, sblk, any_spec, any_spec],
        out_specs=[blk, sblk],
        scratch_shapes=[pltpu.VMEM((N_HEADS, seq // ATTN_TK, DV + ONES_ROWS, ATTN_TK), bf16),
                        pltpu.VMEM((N_HEADS, 2 * ATTN_TQ, DV), bf16),
                        pltpu.VMEM((N_HEADS, DV + ONES_ROWS, 2 * ATTN_TQ), f32),
                        pltpu.VMEM((N_HEADS, 1, 2 * ATTN_TQ), f32),
                        pltpu.VMEM((ATTN_TK, 2 * ATTN_TQ), f32),
                        pltpu.VMEM((rows, w), bf16), pltpu.VMEM((rows, 1), f32), pltpu.VMEM((rows, 1), f32),
                        pltpu.VMEM((rows, DV), f32),
                        pltpu.VMEM((PAGE_SETS, PAGES_PER_STEP) + cache_kt.shape[1:], f32),
                        pltpu.VMEM((PAGE_SETS, PAGES_PER_STEP) + cache_vr.shape[1:], f32),
                        pltpu.SemaphoreType.DMA((PAGE_SETS,))],
    )
    return pl.pallas_call(
        functools.partial(_attn_kernel, lam_init=lam_init, n_steps=n),
        grid_spec=grid_spec,
        out_shape=[SDS((n, seq, w), bf16), SDS((nseq, ds, w), bf16)],
        compiler_params=pltpu.CompilerParams(dimension_semantics=("arbitrary",), vmem_limit_bytes=ATTN_VMEM_LIMIT),
        name="attention",
    )(page_table, lam_p, g_attn, q, k, v, q_s, k_new, v_new, cache_kt, cache_vr)


def _out_route_kernel(oa_ref, ob_ref, x_ref, g1_ref, sh_ref, sc_ref, wa_ref, wb_ref, g2_ref, wr_ref, br_ref,
                      x1_ref, h2_ref, ids_ref, gates_ref, cnt_ref):
    mix = (jnp.dot(oa_ref[0], wa_ref[...], preferred_element_type=f32)
           + jnp.dot(ob_ref[0], wb_ref[...], preferred_element_type=f32))
    x1 = x_ref[0] + g1_ref[0] * mix
    x1_ref[0] = x1
    h2 = _rms(x1, g2_ref[...]) * (1.0 + sc_ref[0]) + sh_ref[0]
    h2_ref[0] = h2.astype(bf16)
    tm = h2.shape[0]
    wr = wr_ref[...]
    wr_hi = wr.astype(bf16)
    wr_lo = (wr - wr_hi.astype(f32)).astype(bf16)
    h_hi = h2.astype(bf16)
    h_lo = (h2 - h_hi.astype(f32)).astype(bf16)
    nt = (((1,), (1,)), ((), ()))
    r1 = lax.dot_general(jnp.concatenate([wr_hi, wr_lo], axis=0), h_hi, nt, preferred_element_type=f32)
    r2 = lax.dot_general(wr_hi, h_lo, nt, preferred_element_type=f32)
    lg = r1[:ROUTER_ROWS] + r1[ROUTER_ROWS:] + r2 + br_ref[:, 0:1]
    row = lax.broadcasted_iota(i32, (SUBLANES, tm), 0).astype(f32)
    big = float(SUBLANES)
    gl = jnp.where(row < N_EXPERT_GROUPS, lg[0:SUBLANES], NEG)
    gmax = jnp.max(gl, axis=0, keepdims=True)
    g_p = 1.0 / jnp.sum(jnp.exp(gl - gmax), axis=0, keepdims=True)
    gidx = jnp.min(jnp.where(gl == gmax, row, big), axis=0, keepdims=True)
    esel = jnp.zeros((SUBLANES, tm), f32)
    for g in range(N_EXPERT_GROUPS):
        esel = jnp.where(gidx == float(g), lg[SUBLANES * (g + 1):SUBLANES * (g + 2)], esel)
    e1 = jnp.max(esel, axis=0, keepdims=True)
    i1 = jnp.min(jnp.where(esel == e1, row, big), axis=0, keepdims=True)
    esel2 = jnp.where(row == i1, -jnp.inf, esel)
    e2 = jnp.max(esel2, axis=0, keepdims=True)
    i2 = jnp.min(jnp.where(esel2 == e2, row, big), axis=0, keepdims=True)
    t = jnp.exp(e2 - e1)
    w1 = g_p / (1.0 + t)
    ids = jnp.concatenate([gidx * EXPERTS_PER_GROUP + i1, gidx * EXPERTS_PER_GROUP + i2], axis=0).astype(i32)
    ids_ref[0] = ids
    gates_ref[0] = jnp.concatenate([w1, w1 * t], axis=0)
    e_iota = lax.broadcasted_iota(i32, (N_EXPERTS, tm), 0)
    onehot = jnp.where((e_iota == ids[0:1]) | (e_iota == ids[1:2]), 1.0, 0.0).astype(bf16)
    cnt_ref[0] = jnp.dot(onehot, jnp.ones((tm, LANES), bf16), preferred_element_type=f32)


def _out_route(oa, ob, x, gate1, shift2, scale2, w_a, w_b, g2, wr_t, br_b, *, tm):
    n, t, d = x.shape
    w_half = oa.shape[2]
    mod_rows = gate1.shape[1]
    tpn = t // tm

    def mod_spec():
        if mod_rows == 1:
            return pl.BlockSpec((1, 1, d), lambda b, i: (b, 0, 0))
        return pl.BlockSpec((1, tm, d), lambda b, i: (b, i, 0))

    full = lambda shp: pl.BlockSpec(shp, lambda b, i: (0,) * len(shp))
    tok = lambda w: pl.BlockSpec((1, tm, w), lambda b, i: (b, i, 0))
    return pl.pallas_call(
        _out_route_kernel,
        grid=(n, tpn),
        in_specs=[tok(w_half), tok(w_half), tok(d), mod_spec(), mod_spec(), mod_spec(),
                  full((w_half, d)), full((w_half, d)), full((1, d)),
                  full((ROUTER_ROWS, d)), full((ROUTER_ROWS, LANES))],
        out_specs=[tok(d), tok(d),
                   pl.BlockSpec((1, 2, tm), lambda b, i: (b * tpn + i, 0, 0)),
                   pl.BlockSpec((1, 2, tm), lambda b, i: (b * tpn + i, 0, 0)),
                   pl.BlockSpec((1, N_EXPERTS, LANES), lambda b, i: (b * tpn + i, 0, 0))],
        out_shape=[SDS((n, t, d), f32), SDS((n, t, d), bf16), SDS((n * tpn, 2, tm), i32),
                   SDS((n * tpn, 2, tm), f32), SDS((n * tpn, N_EXPERTS, LANES), f32)],
        compiler_params=_cparams(("arbitrary", "arbitrary")),
        name="out_route",
    )(oa, ob, x, gate1, shift2, scale2, w_a, w_b, g2, wr_t, br_b)


def _chunk_list(counts, src0, dst0, rows, kmax):
    cin = jnp.cumsum(counts, axis=1)
    cex = cin - counts
    k = jnp.arange(kmax, dtype=i32)
    owner = jnp.sum((cin[:, None, :] <= k[None, :, None]).astype(i32), axis=2)
    owner = jnp.minimum(owner, N_EXPERTS - 1)
    pick = (owner[:, :, None] == jnp.arange(N_EXPERTS, dtype=i32)).astype(i32)
    at_owner = lambda v: jnp.sum(pick * v[:, None, :], axis=2)
    within = (k[None, :] - at_owner(cex)) * rows
    src = at_owner(src0) + within
    dst = at_owner(dst0) + within
    return src.reshape(-1).astype(i32), dst.reshape(-1).astype(i32), cin[:, -1].astype(i32)


def _route_plan(cnt, nb_max, sorted_rows):
    pc = _round_up(cnt, RUN_ALIGN)
    toff = jnp.cumsum(pc, axis=1) - pc
    tot = jnp.sum(pc, axis=0)
    seg = _round_up(tot, EXPERT_BLOCK)
    seg_end = jnp.cumsum(seg)
    seg_start = seg_end - seg
    run_start = seg_start[None, :] + jnp.cumsum(pc, axis=0) - pc
    nb = seg_end[-1] // EXPERT_BLOCK
    lists = []
    done = jnp.zeros_like(pc)
    for rows in CHUNK_ROWS:
        n = (pc - done) // rows
        kmax = sorted_rows // rows if rows == CHUNK_ROWS[0] else N_EXPERTS
        lists.extend(_chunk_list(n, toff + done, run_start + done, rows, kmax))
        done = done + n * rows
    return dict(
        lists=lists,
        tail_start=(seg_start + tot).astype(i32), tail_chunks=((seg - tot) // RUN_ALIGN).astype(i32),
        tail_total=(jnp.sum(seg - tot) // RUN_ALIGN).reshape(1).astype(i32),
        seg_start=seg_start.astype(i32), seg_blocks=(seg // EXPERT_BLOCK).astype(i32),
        nb=nb.reshape(1).astype(i32),
        toff_v=jnp.broadcast_to(toff.astype(f32)[:, :, None], toff.shape + (LANES,)))


def _sorted_positions(ids_ref, toff_ref, tm):
    ids = ids_ref[0]
    idc = jnp.concatenate([ids[0:1], ids[1:2]], axis=1)
    n_blk = 2 * tm // LANES
    e_iota = lax.broadcasted_iota(i32, (N_EXPERTS, 2 * tm), 0)
    onehot = jnp.where(e_iota == idc, 1.0, 0.0)
    stacked = jnp.concatenate([onehot[:, k * LANES:(k + 1) * LANES] for k in range(n_blk)], axis=0).astype(bf16)
    src = lax.broadcasted_iota(i32, (LANES, LANES), 0)
    dst = lax.broadcasted_iota(i32, (LANES, LANES), 1)
    within = jnp.dot(stacked, jnp.where(src < dst, 1.0, 0.0).astype(bf16), preferred_element_type=f32)
    totals = jnp.dot(stacked, jnp.ones((LANES, LANES), bf16), preferred_element_type=f32)
    r = lax.broadcasted_iota(i32, (n_blk * N_EXPERTS, n_blk * N_EXPERTS), 0)
    c = lax.broadcasted_iota(i32, (n_blk * N_EXPERTS, n_blk * N_EXPERTS), 1)
    same_expert_earlier = ((r & (N_EXPERTS - 1)) == (c & (N_EXPERTS - 1))) & (c < r)
    block_off = jnp.dot(jnp.where(same_expert_earlier, 1.0, 0.0).astype(bf16), totals.astype(bf16),
                        preferred_element_type=f32)
    before2 = within + block_off
    before = jnp.concatenate([before2[k * N_EXPERTS:(k + 1) * N_EXPERTS] for k in range(n_blk)], axis=1)
    pos = jnp.sum(onehot * (before + toff_ref[0][:, 0:1]), axis=0, keepdims=True)
    return jnp.where(idc >= 0, pos, -1.0)


def _aligned(row):
    return row if isinstance(row, int) else pl.multiple_of(row, RUN_ALIGN)


def _run_chunk_copy(vmem_buf, hbm_buf, sem, vrow, hrow, to_hbm, rows=RUN_ALIGN):
    v = vmem_buf.at[pl.ds(_aligned(vrow), rows)]
    h = hbm_buf.at[pl.ds(_aligned(hrow), rows)]
    return pltpu.make_async_copy(v, h, sem) if to_hbm else pltpu.make_async_copy(h, v, sem)


def _tile_chunks(i, list_refs, vmem_buf, hbm_buf, sems, to_hbm, wait):
    for c, rows in enumerate(CHUNK_ROWS):
        src_ref, dst_ref, n_ref = list_refs[3 * c:3 * c + 3]
        stride = src_ref.shape[0] // n_ref.shape[0]

        def one(k, carry, rows=rows, src_ref=src_ref, dst_ref=dst_ref, stride=stride, sem=sems.at[c]):
            if wait:
                _run_chunk_copy(vmem_buf, hbm_buf, sem, 0, 0, to_hbm, rows).wait()
            else:
                _run_chunk_copy(vmem_buf, hbm_buf, sem, src_ref[i * stride + k], dst_ref[i * stride + k],
                                to_hbm, rows).start()
            return carry

        lax.fori_loop(0, n_ref[i], one, 0)


def _dispatch_kernel(*refs):
    lists = refs[:N_LIST]
    tstart_ref, tail_chunks_ref, ttot_ref, nb_ref = refs[N_LIST:N_LIST + 4]
    hm_ref, ht_ref, ids_ref, toff_ref, xs_ref, xbuf_sc, zero_sc, sem = refs[N_LIST + 4:]
    i = pl.program_id(0)
    last = pl.num_programs(0) - 1
    tm = hm_ref.shape[0]
    rt = xbuf_sc.shape[1]
    slot = i & 1
    sem_tail, sem_dead = sem.at[len(CHUNK_ROWS)], sem.at[len(CHUNK_ROWS) + 1]

    @pl.when(i == 0)
    def _():
        zero_sc[...] = jnp.zeros(zero_sc.shape, ROW_DTYPE)

    h = jnp.where(i == last, ht_ref[...], hm_ref[...])
    pos = _sorted_positions(ids_ref, toff_ref, tm)
    r_iota = lax.broadcasted_iota(i32, (rt, tm), 0).astype(f32)
    perm = jnp.where((r_iota == pos[:, :tm]) | (r_iota == pos[:, tm:]), 1.0, 0.0).astype(bf16)
    xbuf_sc[slot] = jnp.dot(perm, h, preferred_element_type=f32).astype(ROW_DTYPE)

    @pl.when(i > 0)
    def _():
        _tile_chunks(i - 1, lists, xbuf_sc.at[1 - slot], xs_ref, sem, True, wait=True)

    _tile_chunks(i, lists, xbuf_sc.at[slot], xs_ref, sem, True, wait=False)

    @pl.when(i == last)
    def _():
        _tile_chunks(i, lists, xbuf_sc.at[slot], xs_ref, sem, True, wait=True)

        def per_expert(e, c):
            def per_chunk(k, cc):
                _run_chunk_copy(zero_sc, xs_ref, sem_tail, 0, tstart_ref[e] + k * RUN_ALIGN, True).start()
                return cc
            lax.fori_loop(0, tail_chunks_ref[e], per_chunk, 0)
            return c
        lax.fori_loop(0, N_EXPERTS, per_expert, 0)

        def wait_one(k, c):
            _run_chunk_copy(zero_sc, xs_ref, sem_tail, 0, 0, True).wait()
            return c
        lax.fori_loop(0, ttot_ref[0], wait_one, 0)

        def dead_block(row):
            return pltpu.make_async_copy(zero_sc, xs_ref.at[pl.ds(row, EXPERT_BLOCK)], sem_dead)

        n_dead = xs_ref.shape[0] // EXPERT_BLOCK - nb_ref[0]

        def start_dead(k, c):
            dead_block(pl.multiple_of((nb_ref[0] + k) * EXPERT_BLOCK, EXPERT_BLOCK)).start()
            return c
        lax.fori_loop(0, n_dead, start_dead, 0)

        def wait_dead(k, c):
            dead_block(0).wait()
            return c
        lax.fori_loop(0, n_dead, wait_dead, 0)


def _sorted_rows(tm):
    return _round_up(2 * tm + N_EXPERTS * (RUN_ALIGN - 1), LANES)


def _dispatch(plan, h_main, h_tail, ids, toff_v, rows_total):
    tiles, _, tm = ids.shape
    d = h_main.shape[1]
    tiles_main = h_main.shape[0] // tm
    assert tiles == tiles_main + 1 and h_tail.shape[0] == tm
    rt = _sorted_rows(tm)
    grid_spec = pltpu.PrefetchScalarGridSpec(
        num_scalar_prefetch=N_LIST + 4,
        grid=(tiles,),
        in_specs=[pl.BlockSpec((tm, d), lambda i, *_: (jnp.minimum(i, tiles_main - 1), 0)),
                  pl.BlockSpec((tm, d), lambda i, *_: (0, 0)),
                  pl.BlockSpec((1, 2, tm), lambda i, *_: (i, 0, 0)),
                  pl.BlockSpec((1, N_EXPERTS, LANES), lambda i, *_: (i, 0, 0))],
        out_specs=pl.BlockSpec(memory_space=pl.ANY),
        scratch_shapes=[pltpu.VMEM((2, rt, d), ROW_DTYPE),
                        pltpu.VMEM((EXPERT_BLOCK, d), ROW_DTYPE), pltpu.SemaphoreType.DMA((len(CHUNK_ROWS) + 2,))],
    )
    return pl.pallas_call(
        _dispatch_kernel,
        grid_spec=grid_spec,
        out_shape=SDS((rows_total, d), ROW_DTYPE),
        compiler_params=_cparams(("arbitrary",)),
        name="dispatch",
    )(*plan["lists"], plan["tail_start"], plan["tail_chunks"], plan["tail_total"], plan["nb"],
      h_main, h_tail, ids, toff_v)


def _experts_kernel(seg_ref, nblk_ref, nb_ref, x_ref, wg_ref, wu_ref, wd_ref, y_ref,
                    wg_sc, wu_sc, wd_sc, xbuf, ybuf, sem_in, sem_out):
    e = pl.program_id(0)
    eb = EXPERT_BLOCK
    nblk = nblk_ref[e]
    g0 = seg_ref[e] // eb
    n_all = nb_ref[0]

    def rows(g):
        return pl.ds(pl.multiple_of(g * eb, eb), eb)

    def x_copy(g):
        slot = lax.rem(g, X_AHEAD + 1)
        return pltpu.make_async_copy(x_ref.at[rows(g)], xbuf.at[slot], sem_in.at[slot])

    def y_copy(g):
        return pltpu.make_async_copy(ybuf.at[g & 1], y_ref.at[rows(g)], sem_out.at[g & 1])

    @pl.when(e == 0)
    def _():
        for g in range(X_AHEAD):
            @pl.when(g < n_all)
            def _():
                x_copy(g).start()

    @pl.when(nblk > 0)
    def _():
        wg_sc[...] = wg_ref[0].astype(bf16)
        wu_sc[...] = wu_ref[0].astype(bf16)
        wd_sc[...] = wd_ref[0].astype(bf16)

        def block(b, c):
            g = g0 + b
            x_copy(g).wait()

            @pl.when(g + X_AHEAD < n_all)
            def _():
                x_copy(g + X_AHEAD).start()

            xb = xbuf[lax.rem(g, X_AHEAD + 1)]
            a = jnp.dot(xb, wg_sc[...], preferred_element_type=f32)
            u = jnp.dot(xb, wu_sc[...], preferred_element_type=f32)
            hid = (jax.nn.silu(a) * u).astype(bf16)
            y = jnp.dot(hid, wd_sc[...], preferred_element_type=f32)

            @pl.when(g >= 2)
            def _():
                y_copy(g - 2).wait()

            ybuf[g & 1] = y.astype(ROW_DTYPE)
            y_copy(g).start()
            return c

        lax.fori_loop(0, nblk, block, 0)

    @pl.when(e == pl.num_programs(0) - 1)
    def _():
        @pl.when(n_all >= 2)
        def _():
            y_copy(n_all - 2).wait()

        @pl.when(n_all >= 1)
        def _():
            y_copy(n_all - 1).wait()

        def dead_block(b):
            return pltpu.make_async_copy(ybuf.at[0], y_ref.at[pl.ds(pl.multiple_of(b * eb, eb), eb)], sem_out.at[0])

        n_dead = y_ref.shape[0] // eb - nb_ref[0]
        ybuf[0] = jnp.zeros(ybuf.shape[1:], ROW_DTYPE)

        def start_dead(k, c):
            dead_block(nb_ref[0] + k).start()
            return c
        lax.fori_loop(0, n_dead, start_dead, 0)

        def wait_dead(k, c):
            dead_block(0).wait()
            return c
        lax.fori_loop(0, n_dead, wait_dead, 0)


def _experts(plan, xs, w_gate, w_up, w_down):
    rows_total, d = xs.shape
    n_exp, _, de = w_gate.shape
    eb = EXPERT_BLOCK
    grid_spec = pltpu.PrefetchScalarGridSpec(
        num_scalar_prefetch=3,
        grid=(n_exp,),
        in_specs=[pl.BlockSpec(memory_space=pl.ANY),
                  pl.BlockSpec((1, d, de), lambda e, *_: (e, 0, 0)),
                  pl.BlockSpec((1, d, de), lambda e, *_: (e, 0, 0)),
                  pl.BlockSpec((1, de, d), lambda e, *_: (e, 0, 0))],
        out_specs=pl.BlockSpec(memory_space=pl.ANY),
        scratch_shapes=[pltpu.VMEM((d, de), bf16), pltpu.VMEM((d, de), bf16), pltpu.VMEM((de, d), bf16),
                        pltpu.VMEM((X_AHEAD + 1, eb, d), ROW_DTYPE), pltpu.VMEM((2, eb, d), ROW_DTYPE),
                        pltpu.SemaphoreType.DMA((X_AHEAD + 1,)), pltpu.SemaphoreType.DMA((2,))],
    )
    return pl.pallas_call(
        _experts_kernel,
        grid_spec=grid_spec,
        out_shape=SDS((rows_total, d), ROW_DTYPE),
        compiler_params=_cparams(("arbitrary",)),
        name="experts",
    )(plan["seg_start"], plan["seg_blocks"], plan["nb"], xs, w_gate, w_up, w_down)


def _combine_kernel(*refs, final):
    lists = refs[:N_LIST]
    (xm_ref, xt_ref, gm_ref, gt_ref, gf_ref, ids_ref, gates_ref, toff_ref, y_ref,
     om_ref, ot_ref, ybuf_sc, sem) = refs[N_LIST:]
    i = pl.program_id(0)
    last = pl.num_programs(0) - 1
    tm = xm_ref.shape[0]
    rt = ybuf_sc.shape[1]
    slot = i & 1

    @pl.when(i == 0)
    def _():
        ybuf_sc[...] = jnp.zeros(ybuf_sc.shape, ROW_DTYPE)
        _tile_chunks(0, lists, ybuf_sc.at[0], y_ref, sem, False, wait=False)

    _tile_chunks(i, lists, ybuf_sc.at[slot], y_ref, sem, False, wait=True)

    @pl.when(i < last)
    def _():
        _tile_chunks(i + 1, lists, ybuf_sc.at[1 - slot], y_ref, sem, False, wait=False)

    pos = _sorted_positions(ids_ref, toff_ref, tm)
    gates = gates_ref[0]
    r_iota = lax.broadcasted_iota(i32, (rt, tm), 0).astype(f32)
    weights = (jnp.where(r_iota == pos[:, :tm], gates[0:1], 0.0)
               + jnp.where(r_iota == pos[:, tm:], gates[1:2], 0.0)).astype(bf16)
    ff = lax.dot_general(weights, ybuf_sc[slot], (((0,), (0,)), ((), ())), preferred_element_type=f32)

    def finish(x1, gate):
        x2 = x1 + gate * ff
        return _rms(x2, gf_ref[...]) if final else x2

    @pl.when(i < last)
    def _():
        om_ref[...] = finish(xm_ref[...], gm_ref[0])

    @pl.when(i == last)
    def _():
        ot_ref[...] = finish(xt_ref[...], gt_ref[...])


def _combine(plan, x_main, x_tail, gate_main, gate_tail, g_final, ids, gates, toff_v, y, *, final):
    tiles, _, tm = ids.shape
    d = x_main.shape[1]
    tiles_main = x_main.shape[0] // tm
    tpr = tiles_main // gate_main.shape[0]
    rt = _sorted_rows(tm)
    main_idx = lambda i, *_: (jnp.minimum(i, tiles_main - 1), 0)
    tail_spec = pl.BlockSpec((tm, d), lambda i, *_: (0, 0))
    slot_spec = pl.BlockSpec((1, 2, tm), lambda i, *_: (i, 0, 0))
    grid_spec = pltpu.PrefetchScalarGridSpec(
        num_scalar_prefetch=N_LIST,
        grid=(tiles,),
        in_specs=[pl.BlockSpec((tm, d), main_idx), tail_spec,
                  pl.BlockSpec((1, 1, d), lambda i, *_: (jnp.minimum(i, tiles_main - 1) // tpr, 0, 0)), tail_spec,
                  pl.BlockSpec((1, d), lambda i, *_: (0, 0)), slot_spec, slot_spec,
                  pl.BlockSpec((1, N_EXPERTS, LANES), lambda i, *_: (i, 0, 0)),
                  pl.BlockSpec(memory_space=pl.ANY)],
        out_specs=[pl.BlockSpec((tm, d), main_idx), tail_spec],
        scratch_shapes=[pltpu.VMEM((2, rt, d), ROW_DTYPE), pltpu.SemaphoreType.DMA((len(CHUNK_ROWS),))],
    )
    return pl.pallas_call(
        functools.partial(_combine_kernel, final=final),
        grid_spec=grid_spec,
        out_shape=[SDS(x_main.shape, f32), SDS((tm, d), f32)],
        compiler_params=_cparams(("arbitrary",)),
        name="combine",
    )(*plan["lists"], x_main, x_tail, gate_main, gate_tail, g_final, ids, gates, toff_v, y)


def _moe(main, tail, g_final, w_gate, w_up, w_down, *, tm, final):
    n, t, d = main["x1"].shape
    t_tail = tail["x1"].shape[1]
    pad = tm - t_tail
    tiles = n * t // tm + 1
    rows_max = 2 * (n * t + t_tail) + tiles * N_EXPERTS * (RUN_ALIGN - 1) + N_EXPERTS * (EXPERT_BLOCK - 1)
    nb_max = -(-rows_max // EXPERT_BLOCK)
    rows2 = lambda a: jnp.pad(a.reshape(t_tail, d), ((0, pad), (0, 0)))
    ids = jnp.concatenate([main["ids"], jnp.pad(tail["ids"], ((0, 0), (0, 0), (0, pad)), constant_values=-1)])
    gates = jnp.concatenate([main["gates"], jnp.pad(tail["gates"], ((0, 0), (0, 0), (0, pad)))])
    cnt = jnp.concatenate([main["cnt"], tail["cnt"]])[:, :, 0].astype(i32)
    plan = _route_plan(cnt, nb_max, _sorted_rows(tm))
    xs = _dispatch(plan, main["h2"].reshape(n * t, d), rows2(tail["h2"]), ids, plan["toff_v"],
                   nb_max * EXPERT_BLOCK)
    y = _experts(plan, xs, w_gate, w_up, w_down)
    out_m, out_t = _combine(plan, main["x1"].reshape(n * t, d), rows2(tail["x1"]), main["gate"],
                            rows2(tail["gate"]), g_final, ids, gates, plan["toff_v"], y, final=final)
    return out_m.reshape(n, t, d), out_t[:t_tail].reshape(1, t_tail, d)


def _before_attention(x, mods, tables, p, *, tm, mix, emit_vn):
    ws_t, mask, bs_b = mix
    return _in_proj(x, mods[0], mods[1], p["g1"], p["w_in"], tables, ws_t, mask, bs_b, p["g_v"], p["g_mlp"],
                    tm=tm, emit_vn=emit_vn)


def _after_attention(x, oa, ob, mods, p, *, tm):
    x1, h2, ids, gates, cnt = _out_route(oa, ob, x, mods[2], mods[3], mods[4], p["w_out_a"], p["w_out_b"],
                                         p["g2"], p["wr_t"], p["br_b"], tm=tm)
    return dict(x1=x1, h2=h2, ids=ids, gates=gates, cnt=cnt, gate=mods[5])


def kernel(x_prompt, x_sample, cache_k, cache_v, page_table, c_prompt, c_sample, w_ada, b_ada, g_norm1, w_in,
           lam_p, g_attn, g_v, w_s, b_s, g_mlp, w_out, g_norm2, w_router_g, b_router_g, w_router_e, b_router_e,
           w_exp_gate, w_exp_up, w_exp_down, g_final):
    nb, seq, d = x_prompt.shape
    ndec, dseq, _ = x_sample.shape
    depth = w_ada.shape[0]
    past_len = page_table.shape[1] * PAGE
    n_pool = cache_k.shape[1]
    a_w = N_HEADS * DV
    tm_s = ndec * dseq

    tables_p, tables_s8 = _rope_tables(seq, dseq, past_len)
    tables_s = tuple(jnp.tile(t, (ndec, 1)) for t in tables_s8)
    c_all = jnp.concatenate([c_prompt, c_sample], axis=0)
    c_pad = _round_up(c_all.shape[0], SUBLANES) - c_all.shape[0]
    c_all = jnp.pad(c_all, ((0, c_pad), (0, 0)))

    tri = jnp.tril(jnp.ones((CHUNK, CHUNK), f32))
    idx = jnp.arange(tm_s)
    mask_s = ((idx[:, None] // dseq == idx[None, :] // dseq) & (idx[None, :] % dseq <= idx[:, None] % dseq)).astype(f32)
    sel_s = (idx[:, None] % dseq == jnp.arange(dseq)[None, :]).astype(f32)

    xp, xs = x_prompt, x_sample.reshape(1, tm_s, d)
    kp_l, vp_l, ks_l, vs_l, cv_l = [], [], [], [], []
    for l in range(depth):
        lam_init = 0.8 - 0.6 * math.exp(-0.3 * l)
        mod = _adaln(c_all, w_ada[l], b_ada[l])
        mods_p = [mod[:nb, None, j * d:(j + 1) * d] for j in range(6)]
        mods_s = [jnp.repeat(mod[nb:nb + ndec, j * d:(j + 1) * d], dseq, axis=0)[None] for j in range(6)]
        wr_t = jnp.concatenate([w_router_g[l].T, jnp.zeros((SUBLANES - N_EXPERT_GROUPS, d), f32),
                                w_router_e[l].T], axis=0)
        br = jnp.concatenate([b_router_g[l], jnp.zeros((SUBLANES - N_EXPERT_GROUPS,), f32), b_router_e[l]])
        p = dict(
            g1=g_norm1[l].reshape(1, d), w_in=w_in[l].astype(bf16), g_v=g_v[l].reshape(1, -1),
            g_mlp=g_mlp[l].reshape(1, -1), w_out_a=w_out[l][:a_w].astype(bf16), w_out_b=w_out[l][a_w:].astype(bf16),
            g2=g_norm2[l].reshape(1, d), wr_t=wr_t, br_b=jnp.broadcast_to(br[:, None], (ROUTER_ROWS, LANES)),
            w_gate=w_exp_gate[l], w_up=w_exp_up[l], w_down=w_exp_down[l])
        gf = g_final.reshape(1, d)
        mix_p = (w_s[l], tri, jnp.broadcast_to(b_s[l][:, :, None], (N_GROUPS_B, CHUNK, C_B)))
        rep = lambda eq, *ops: jnp.einsum(eq, *ops, precision=lax.Precision.HIGHEST)
        mix_s = (rep("ia,gab,jb->gij", sel_s, w_s[l][:, :dseq, :dseq], sel_s), mask_s,
                 jnp.broadcast_to(rep("ia,ga->gi", sel_s, b_s[l][:, :dseq])[:, :, None], (N_GROUPS_B, tm_s, C_B)))

        ck = jnp.transpose(cache_k[l], (0, 2, 3, 4, 1)).reshape(n_pool, -1, PAGE)
        cv = cache_v[l].reshape(n_pool, PAGE * N_HEADS, DV)

        q_p, k_p, v_p, kb_p, vb_p, ob_p = _before_attention(xp, mods_p, tables_p, p, tm=TOKEN_TILE, mix=mix_p,
                                                            emit_vn=False)
        q_s, k_s, v_s, _, _, ob_s, cv_s = _before_attention(xs, mods_s, tables_s, p, tm=tm_s, mix=mix_s,
                                                            emit_vn=True)
        per_seq = lambda a: a.reshape(ndec, dseq, -1)
        oa_p, oa_s = _attention(page_table, lam_p[l], g_attn[l], q_p, kb_p, vb_p, per_seq(q_s), per_seq(k_s),
                                per_seq(v_s), ck, cv, lam_init)
        routed_p = _after_attention(xp, oa_p, ob_p, mods_p, p, tm=TOKEN_TILE)
        routed_s = _after_attention(xs, oa_s.reshape(1, tm_s, -1), ob_s, mods_s, p, tm=tm_s)
        xp, xs = _moe(routed_p, routed_s, gf, p["w_gate"], p["w_up"], p["w_down"], tm=TOKEN_TILE,
                      final=l == depth - 1)
        kp_l.append(k_p.reshape(nb, seq, N_HEADS, 2, DK))
        vp_l.append(v_p.reshape(nb, seq, N_HEADS, DV))
        ks_l.append(k_s.reshape(ndec, dseq, N_HEADS, 2, DK))
        vs_l.append(v_s.reshape(ndec, dseq, N_HEADS, DV))
        cv_l.append(cv_s.reshape(ndec, dseq, -1))
    return (xp, xs.reshape(ndec, dseq, d), jnp.stack(kp_l), jnp.stack(vp_l), jnp.stack(ks_l), jnp.stack(vs_l),
            jnp.stack(cv_l))
```

```python
import functools
import math

import jax
import jax.numpy as jnp
from jax import lax
from jax.experimental import pallas as pl
from jax.experimental.pallas import tpu as pltpu

f32 = jnp.float32
bf16 = jnp.bfloat16
i32 = jnp.int32
SDS = jax.ShapeDtypeStruct

N_HEADS = 4
DK = 64
DV = 2 * DK
ROT = DK // 4
ROPE_THETA = 500000.0
N_GROUPS_B = 4
C_B = 128
CHUNK = 128
N_EXPERT_GROUPS = 4
EXPERTS_PER_GROUP = 8
N_EXPERTS = N_EXPERT_GROUPS * EXPERTS_PER_GROUP
PAGE = 128
EPS = 1e-6
NEG = -1e30
LOG2E = math.log2(math.e)

LANES = 128
SUBLANES = 8
VMEM_LIMIT = 56 * 1024 * 1024
ATTN_VMEM_LIMIT = 60 * 1024 * 1024

TOKEN_TILE = 512
ATTN_TQ = 256
ATTN_TK = 256
ONES_ROWS = 16
PAGES_PER_STEP = 16
PAGE_SETS = 3
EXPERT_BLOCK = 256
X_AHEAD = 3
RUN_ALIGN = 16
CHUNK_ROWS = (32, 16)
ROW_DTYPE = bf16
N_LIST = 3 * len(CHUNK_ROWS)
ROUTER_ROWS = 40


def _cparams(sem=None):
    return pltpu.CompilerParams(dimension_semantics=sem, vmem_limit_bytes=VMEM_LIMIT)


def _round_up(x, m):
    return (x + m - 1) // m * m


def _adaln_kernel(c_ref, w_ref, b_ref, o_ref):
    a = jax.nn.silu(c_ref[...])
    w = w_ref[...]
    a_hi = a.astype(bf16)
    a_lo = (a - a_hi.astype(f32)).astype(bf16)
    w_hi = w.astype(bf16)
    w_lo = (w - w_hi.astype(f32)).astype(bf16)
    n = a.shape[0]
    r = jnp.dot(jnp.concatenate([a_hi, a_lo], axis=0), w_hi, preferred_element_type=f32)
    o_ref[...] = r[:n] + r[n:] + jnp.dot(a_hi, w_lo, preferred_element_type=f32) + b_ref[...]


def _adaln(c_all, w_ada, b_ada):
    n, d = c_all.shape
    m = w_ada.shape[1]
    tn = 1536
    return pl.pallas_call(
        _adaln_kernel,
        grid=(m // tn,),
        in_specs=[pl.BlockSpec((n, d), lambda j: (0, 0)),
                  pl.BlockSpec((d, tn), lambda j: (0, j)),
                  pl.BlockSpec((1, tn), lambda j: (0, j))],
        out_specs=pl.BlockSpec((n, tn), lambda j: (0, j)),
        out_shape=SDS((n, m), f32),
        compiler_params=_cparams(("arbitrary",)),
        name="adaln",
    )(c_all, w_ada, b_ada.reshape(1, m))


def _rope_kernel(inv_ref, cos_ref, sin_ref, *, rows_prompt, past_len):
    shape = cos_ref.shape
    r = lax.broadcasted_iota(i32, shape, 0)
    l = lax.broadcasted_iota(i32, shape, 1)
    base = jnp.where(r < rows_prompt, r * 16, past_len + (r - rows_prompt) * 16)
    pos = base + (l >> 3)
    ang = pos.astype(f32) * inv_ref[...]
    cos_ref[...] = jnp.cos(ang)
    sin_ref[...] = jnp.sin(ang)


def _rope_tables(seq, dec_seq, past_len):
    inv = ROPE_THETA ** (-jnp.arange(0, ROT, 2, dtype=f32) / ROT)
    inv_lane = jnp.tile(inv, LANES // (ROT // 2)).reshape(1, LANES)
    rp = seq // 16
    rt = rp + SUBLANES
    cos_c, sin_c = pl.pallas_call(
        functools.partial(_rope_kernel, rows_prompt=rp, past_len=past_len),
        out_shape=(SDS((rt, LANES), f32), SDS((rt, LANES), f32)),
        name="rope_table",
    )(inv_lane)

    def expand(c8, s8):
        n = c8.shape[0]
        z8 = jnp.zeros((n, ROT // 2), f32)
        rest = DK - ROT
        cos_t = jnp.concatenate([c8, c8, jnp.ones((n, rest), f32)], axis=1)
        sin_a = jnp.concatenate([-s8, z8, jnp.zeros((n, rest), f32)], axis=1)
        sin_b = jnp.concatenate([z8, s8, jnp.zeros((n, rest), f32)], axis=1)
        return tuple(jnp.tile(t, (1, LANES // DK)) for t in (cos_t, sin_a, sin_b))

    half = ROT // 2
    prompt = expand(cos_c[:rp].reshape(seq, half), sin_c[:rp].reshape(seq, half))
    sample = expand(cos_c[rp].reshape(16, half)[:dec_seq], sin_c[rp].reshape(16, half)[:dec_seq])
    return prompt, sample


def _rms(x, g):
    return x * lax.rsqrt(jnp.mean(x * x, axis=-1, keepdims=True) + EPS) * g


def _in_proj_kernel(x_ref, sh_ref, sc_ref, g1_ref, w_ref, cos_ref, sa_ref, sb_ref,
                    ws_ref, msk_ref, bs_ref, gv_ref, gm_ref,
                    q_ref, k_ref, v_ref, kb_ref, vb_ref, ob_ref, *vn_refs, mix_rows):
    x = x_ref[0]
    tm = x.shape[0]
    h = _rms(x, g1_ref[...]) * (1.0 + sc_ref[0]) + sh_ref[0]
    z = jnp.dot(h.astype(bf16), w_ref[...], preferred_element_type=f32)
    qk_w = N_HEADS * 2 * DK
    a_w = N_HEADS * DV
    b_w = N_GROUPS_B * C_B
    cos_t, sin_a, sin_b = cos_ref[...], sa_ref[...], sb_ref[...]
    for s in range(2 * qk_w // LANES):
        zs = z[:, s * LANES:(s + 1) * LANES]
        rot = zs * cos_t + pltpu.roll(zs, LANES - ROT // 2, 1) * sin_a + pltpu.roll(zs, ROT // 2, 1) * sin_b
        if s < qk_w // LANES:
            q_ref[0, :, s * LANES:(s + 1) * LANES] = (rot * (DK ** -0.5 * LOG2E)).astype(bf16)
        else:
            k_ref[0, :, s * LANES - qk_w:(s + 1) * LANES - qk_w] = rot
            kb_ref[0, :, s * LANES - qk_w:(s + 1) * LANES - qk_w] = rot.astype(bf16)
    v = z[:, 2 * qk_w:2 * qk_w + a_w]
    for h in range(N_HEADS):
        v_ref[0, pl.ds(h, tm, stride=N_HEADS), :] = v[:, h * DV:(h + 1) * DV]
    vb_ref[0] = v.astype(bf16)
    uv = jax.nn.gelu(z[:, 2 * qk_w + a_w:])
    nblk = tm // mix_rows
    for g in range(N_GROUPS_B):
        u = uv[:, g * C_B:(g + 1) * C_B]
        vn = _rms(uv[:, b_w + g * C_B:b_w + (g + 1) * C_B], gv_ref[:, g * C_B:(g + 1) * C_B])
        if vn_refs:
            vn_refs[0][0, :, g * C_B:(g + 1) * C_B] = vn
        wm = (ws_ref[g] * msk_ref[...]).astype(bf16)
        vcat = jnp.concatenate([vn[j * mix_rows:(j + 1) * mix_rows] for j in range(nblk)], axis=1)
        mixed = jnp.dot(wm, vcat.astype(bf16), preferred_element_type=f32)
        for j in range(nblk):
            t = u[j * mix_rows:(j + 1) * mix_rows] * (mixed[:, j * C_B:(j + 1) * C_B] + bs_ref[g])
            ob_ref[0, j * mix_rows:(j + 1) * mix_rows, g * C_B:(g + 1) * C_B] = _rms(
                t, gm_ref[:, g * C_B:(g + 1) * C_B]).astype(bf16)


def _in_proj(x, shift, scale, g1, w_in_bf, tables, ws_t, mask, bs_b, g_v, g_mlp, *, tm, emit_vn):
    n, t, d = x.shape
    in_w = w_in_bf.shape[1]
    r = ws_t.shape[1]
    mod_rows = shift.shape[1]
    tab_rows = tables[0].shape[0]
    w_half = N_HEADS * DV

    def mod_spec():
        if mod_rows == 1:
            return pl.BlockSpec((1, 1, d), lambda b, i: (b, 0, 0))
        return pl.BlockSpec((1, tm, d), lambda b, i: (b, i, 0))

    def tab_spec():
        if tab_rows == tm:
            return pl.BlockSpec((tm, LANES), lambda b, i: (0, 0))
        return pl.BlockSpec((tm, LANES), lambda b, i: (i, 0))

    full = lambda shp: pl.BlockSpec(shp, lambda b, i: (0,) * len(shp))
    tok = lambda w: pl.BlockSpec((1, tm, w), lambda b, i: (b, i, 0))
    out_shape = [SDS((n, t, w_half), bf16), SDS((n, t, w_half), f32), SDS((n, t * N_HEADS, DV), f32),
                 SDS((n, t, w_half), bf16), SDS((n, t, w_half), bf16), SDS((n, t, w_half), bf16)]
    out_specs = [tok(w_half)] * 6
    out_specs[2] = pl.BlockSpec((1, tm * N_HEADS, DV), lambda b, i: (b, i, 0))
    if emit_vn:
        out_shape.append(SDS((n, t, w_half), f32))
        out_specs.append(tok(w_half))
    return pl.pallas_call(
        functools.partial(_in_proj_kernel, mix_rows=r),
        grid=(n, t // tm),
        in_specs=[tok(d), mod_spec(), mod_spec(), full((1, d)), full((d, in_w)),
                  tab_spec(), tab_spec(), tab_spec(),
                  full((N_GROUPS_B, r, r)), full((r, r)), full((N_GROUPS_B, r, C_B)),
                  full((1, w_half)), full((1, w_half))],
        out_specs=out_specs,
        out_shape=out_shape,
        compiler_params=_cparams(("arbitrary", "arbitrary")),
        name="in_proj",
    )(x, shift, scale, g1, w_in_bf, *tables, ws_t, mask, bs_b, g_v, g_mlp)


def _diff_lambda(lam_ref, lam_init):
    lp = lam_ref[...]
    s1 = jnp.sum(lp[0:1] * lp[1:2], axis=1, keepdims=True)
    s2 = jnp.sum(lp[2:3] * lp[3:4], axis=1, keepdims=True)
    return jnp.exp(s1) - jnp.exp(s2) + lam_init


def _attn_kernel(pt_ref, lam_ref, g_ref, q_ref, k_ref, v_ref, qs_ref, kn_ref, vn_ref, ck_ref, cv_ref,
                 o_ref, os_ref, vt_sc, qq_sc, acc_sc, m_sc, s_sc, qbd_sc, sm_sc, sl_sc, sacc_sc, kbuf, vbuf, sem,
                 *, lam_init, n_steps):
    b = pl.program_id(0)
    seq = q_ref.shape[1]
    tq, tk = ATTN_TQ, ATTN_TK
    n_qb = seq // tq
    lam = _diff_lambda(lam_ref, lam_init)
    heads = range(N_HEADS)
    nt = (((1,), (1,)), ((), ()))

    seq_per_step, ds, width = qs_ref.shape
    pps = kbuf.shape[1]
    groups = pt_ref.shape[1] // pps
    slots = seq_per_step * groups
    total_slots = n_steps * slots
    rows_h = 2 * ds
    rows = N_HEADS * rows_h

    assert groups & (groups - 1) == 0
    g_shift = groups.bit_length() - 1
    n_sets = kbuf.shape[0]
    g_first = b * slots

    def set_of(g):
        return lax.rem(g, n_sets)

    def page_copies(g, buf):
        sq = g >> g_shift
        first_page = (g & (groups - 1)) * pps
        out = []
        for i in range(pps):
            page = pt_ref[sq, first_page + i]
            out.append(pltpu.make_async_copy(ck_ref.at[page], kbuf.at[buf, i], sem.at[buf]))
            out.append(pltpu.make_async_copy(cv_ref.at[page], vbuf.at[buf, i], sem.at[buf]))
        return out

    def wait_pages(buf):
        for i in range(pps):
            pltpu.make_async_copy(ck_ref.at[0], kbuf.at[buf, i], sem.at[buf]).wait()
            pltpu.make_async_copy(cv_ref.at[0], vbuf.at[buf, i], sem.at[buf]).wait()

    def sample_update(s, value):
        m_prev = sm_sc[...]
        m_new = jnp.maximum(m_prev, jnp.max(s, axis=1, keepdims=True))
        alpha = jnp.exp2(m_prev - m_new)
        p = jnp.exp2(s - m_new)
        sl_sc[...] = alpha * sl_sc[...] + jnp.sum(p, axis=1, keepdims=True)
        pb = p.astype(bf16)
        n_blk = s.shape[1] // PAGE
        parts = [jnp.dot(pb[h * rows_h:(h + 1) * rows_h],
                         jnp.concatenate([value(i, h) for i in range(n_blk)], axis=0), preferred_element_type=f32)
                 for h in heads]
        sacc_sc[...] = sacc_sc[...] * alpha + jnp.concatenate(parts, axis=0)
        sm_sc[...] = m_new

    def slot_begin(t):
        g = g_first + t
        wait_pages(set_of(g))
        g_ahead = g + n_sets - 1
        for cp in page_copies(jnp.minimum(g_ahead, total_slots - 1), set_of(g_ahead)):
            cp.start()

        @pl.when(t & (groups - 1) == 0)
        def _():
            row_i = lax.broadcasted_iota(i32, (rows, width), 0)
            col_i = lax.broadcasted_iota(i32, (rows, width), 1)
            qt = jnp.concatenate([qs_ref[t >> g_shift].astype(f32)] * (N_HEADS * 2), axis=0)
            same_map = (col_i >> (DK.bit_length() - 1)) == (row_i >> (ds.bit_length() - 1))
            qbd_sc[...] = jnp.where(same_map, qt, 0.0).astype(bf16)
            sm_sc[...] = jnp.full(sm_sc.shape, NEG, f32)
            sl_sc[...] = jnp.zeros(sl_sc.shape, f32)
            sacc_sc[...] = jnp.zeros(sacc_sc.shape, f32)

    def slot_hooks(t, n_parts):
        buf = set_of(g_first + t)
        ppp = pps // n_parts
        carried = {}

        def scores_of(part):
            def run():
                k_all = jnp.concatenate([kbuf[buf, part * ppp + i].astype(bf16) for i in range(ppp)], axis=1)
                carried[part] = jnp.dot(qbd_sc[...], k_all, preferred_element_type=f32)
            return run

        def values_of(part):
            def run():
                sample_update(carried.pop(part), lambda i, h: vbuf[
                    buf, part * ppp + i, pl.ds(h, PAGE, stride=N_HEADS), :].astype(bf16))
            return run

        return [f(part) for part in range(n_parts) for f in (scores_of, values_of)]

    def slot_end(t):
        @pl.when(t & (groups - 1) == groups - 1)
        def _():
            sq = t >> g_shift
            qbd = qbd_sc[...]
            pad = jnp.zeros((PAGE - ds, width), f32)
            kn = jnp.concatenate([kn_ref[sq], pad], axis=0).astype(bf16)
            vn = jnp.concatenate([vn_ref[sq], pad], axis=0).astype(bf16)
            s_new = lax.dot_general(qbd, kn, nt, preferred_element_type=f32)
            key = lax.broadcasted_iota(i32, s_new.shape, 1)
            qry = lax.broadcasted_iota(i32, s_new.shape, 0) & (ds - 1)
            sample_update(jnp.where(key <= qry, s_new, NEG), lambda i, h: vn[:, h * DV:(h + 1) * DV])
            out = sacc_sc[...] * (1.0 / sl_sc[...])
            for h in heads:
                o = out[h * rows_h:h * rows_h + ds] - lam * out[h * rows_h + ds:(h + 1) * rows_h]
                os_ref[sq, :, h * DV:(h + 1) * DV] = _rms(o, g_ref[h:h + 1, :] * (1.0 - lam_init)).astype(bf16)

    @pl.when(b == 0)
    def _():
        for g in range(n_sets - 1):
            for cp in page_copies(min(g, total_slots - 1), g):
                cp.start()

    def prep(j, c):
        sl = pl.ds(pl.multiple_of(j * tk, tk), tk)
        for h in heads:
            vt_sc[h, j, :DV] = v_ref[0, sl, h * DV:(h + 1) * DV].T
            vt_sc[h, j, DV:] = jnp.ones((ONES_ROWS, tk), bf16)
        return c

    lax.fori_loop(0, seq // tk, prep, 0)
    lane = lax.broadcasted_iota(i32, (tq, DV), 1)

    def q_block(qi, t, slot_in_diag):
        qsl = pl.ds(pl.multiple_of(qi * tq, tq), tq)
        for h in heads:
            qb = q_ref[0, qsl, h * DV:(h + 1) * DV]
            zero = jnp.zeros_like(qb)
            qq_sc[h, :tq] = jnp.where(lane < DK, qb, zero)
            qq_sc[h, tq:] = jnp.where(lane >= DK, qb, zero)
            m_sc[h] = jnp.full(m_sc.shape[1:], NEG, f32)
            acc_sc[h] = jnp.zeros(acc_sc.shape[1:], f32)

        def scores(j, h):
            ksl = pl.ds(pl.multiple_of(j * tk, tk), tk)
            return lax.dot_general(k_ref[0, ksl, h * DV:(h + 1) * DV], qq_sc[h], nt, preferred_element_type=f32)

        s_sc[...] = scores(0, 0)

        def kv_block(j, masked, hooks=()):
            s_next = None
            for h in heads:
                s = s_sc[...] if h == 0 else s_next
                if h + 1 < N_HEADS:
                    s_next = scores(j, h + 1)
                elif not masked:
                    s_sc[...] = scores(j + 1, 0)
                if h < len(hooks):
                    hooks[h]()
                if masked:
                    key_i = lax.broadcasted_iota(i32, s.shape, 0)
                    qry_i = lax.broadcasted_iota(i32, s.shape, 1) & (tq - 1)
                    s = jnp.where(key_i <= qry_i, s, NEG)
                m_prev = m_sc[h]
                m_new = jnp.maximum(m_prev, jnp.max(s, axis=0, keepdims=True))
                alpha = jnp.exp2(m_prev - m_new)
                p = jnp.exp2(s - m_new)
                pv = jnp.dot(vt_sc[h, j], p.astype(bf16), preferred_element_type=f32)
                acc_sc[h] = acc_sc[h] * alpha + pv
                m_sc[h] = m_new

        def full_block(j, cc):
            kv_block(j, False)
            return cc

        idle = lambda: None

        def full_pair(jj, tt):
            slot_begin(tt)
            slot_scores, slot_values = slot_hooks(tt, 1)
            kv_block(2 * jj, False, [idle, idle, slot_scores])
            kv_block(2 * jj + 1, False)
            slot_values()
            slot_end(tt)
            return tt + 1

        pairs = qi >> 1
        t = lax.fori_loop(0, pairs, full_pair, t)
        lax.fori_loop(2 * pairs, qi, full_block, 0)
        if slot_in_diag:
            slot_begin(t)
            slot_scores, slot_values = slot_hooks(t, 1)
            slot_scores()
            kv_block(qi, True, [idle, idle, slot_values])
        else:
            kv_block(qi, True)
        for h in heads:
            acc = acc_sc[h]
            inv_l = 1.0 / acc[DV:DV + 1, :]
            o_t = acc[:DV, :tq] * inv_l[:, :tq] - lam * (acc[:DV, tq:] * inv_l[:, tq:])
            o_ref[0, qsl, h * DV:(h + 1) * DV] = _rms(o_t.T, g_ref[h:h + 1, :] * (1.0 - lam_init)).astype(bf16)
        if slot_in_diag:
            slot_end(t)
            t = t + 1
        return t

    pair_slots = sum(qi >> 1 for qi in range(n_qb))
    diag_slots = slots - pair_slots
    assert 0 <= diag_slots <= n_qb, (slots, pair_slots)
    t = lax.fori_loop(0, diag_slots, lambda qi, tt: q_block(qi, tt, True), 0)
    lax.fori_loop(diag_slots, n_qb, lambda qi, tt: q_block(qi, tt, False), t)

    @pl.when(b == n_steps - 1)
    def _():
        for g in range(total_slots, total_slots + n_sets - 1):
            wait_pages(g % n_sets)


def _attention(page_table, lam_p, g_attn, q, k, v, q_s, k_new, v_new, cache_kt, cache_vr, lam_init):
    n, seq, w = q.shape
    nseq, ds, _ = q_s.shape
    assert nseq % n == 0 and ds & (ds - 1) == 0 and page_table.shape[1] % PAGES_PER_STEP == 0
    sps = nseq // n
    rows = N_HEADS * 2 * ds
    blk = pl.BlockSpec((1, seq, w), lambda b, pt: (b, 0, 0))
    blk_in = pl.BlockSpec((1, seq, w), lambda b, pt: (b, 0, 0), pipeline_mode=pl.Buffered(1))
    sblk = pl.BlockSpec((sps, ds, w), lambda b, pt: (b, 0, 0))
    full = lambda a: pl.BlockSpec(a.shape, lambda b, pt: (0,) * a.ndim)
    any_spec = pl.BlockSpec(memory_space=pl.ANY)
    grid_spec = pltpu.PrefetchScalarGridSpec(
        num_scalar_prefetch=1,
        grid=(n,),
        in_specs=[full(lam_p), full(g_attn), blk_in, blk_in, blk_in, sblk, sblk, sblk, any_spec, any_spec],
        out_specs=[blk, sblk],
        scratch_shapes=[pltpu.VMEM((N_HEADS, seq // ATTN_TK, DV + ONES_ROWS, ATTN_TK), bf16),
                        pltpu.VMEM((N_HEADS, 2 * ATTN_TQ, DV), bf16),
                        pltpu.VMEM((N_HEADS, DV + ONES_ROWS, 2 * ATTN_TQ), f32),
                        pltpu.VMEM((N_HEADS, 1, 2 * ATTN_TQ), f32),
                        pltpu.VMEM((ATTN_TK, 2 * ATTN_TQ), f32),
                        pltpu.VMEM((rows, w), bf16), pltpu.VMEM((rows, 1), f32), pltpu.VMEM((rows, 1), f32),
                        pltpu.VMEM((rows, DV), f32),
                        pltpu.VMEM((PAGE_SETS, PAGES_PER_STEP) + cache_kt.shape[1:], f32),
                        pltpu.VMEM((PAGE_SETS, PAGES_PER_STEP) + cache_vr.shape[1:], f32),
                        pltpu.SemaphoreType.DMA((PAGE_SETS,))],
    )
    return pl.pallas_call(
        functools.partial(_attn_kernel, lam_init=lam_init, n_steps=n),
        grid_spec=grid_spec,
        out_shape=[SDS((n, seq, w), bf16), SDS((nseq, ds, w), bf16)],
        compiler_params=pltpu.CompilerParams(dimension_semantics=("arbitrary",), vmem_limit_bytes=ATTN_VMEM_LIMIT),
        name="attention",
    )(page_table, lam_p, g_attn, q, k, v, q_s, k_new, v_new, cache_kt, cache_vr)


def _out_route_kernel(oa_ref, ob_ref, x_ref, g1_ref, sh_ref, sc_ref, wa_ref, wb_ref, g2_ref, wr_ref, br_ref,
                      x1_ref, h2_ref, ids_ref, gates_ref, cnt_ref):
    mix = (jnp.dot(oa_ref[0], wa_ref[...], preferred_element_type=f32)
           + jnp.dot(ob_ref[0], wb_ref[...], preferred_element_type=f32))
    x1 = x_ref[0] + g1_ref[0] * mix
    x1_ref[0] = x1
    h2 = _rms(x1, g2_ref[...]) * (1.0 + sc_ref[0]) + sh_ref[0]
    h2_ref[0] = h2.astype(bf16)
    tm = h2.shape[0]
    wr = wr_ref[...]
    wr_hi = wr.astype(bf16)
    wr_lo = (wr - wr_hi.astype(f32)).astype(bf16)
    h_hi = h2.astype(bf16)
    h_lo = (h2 - h_hi.astype(f32)).astype(bf16)
    nt = (((1,), (1,)), ((), ()))
    r1 = lax.dot_general(jnp.concatenate([wr_hi, wr_lo], axis=0), h_hi, nt, preferred_element_type=f32)
    r2 = lax.dot_general(wr_hi, h_lo, nt, preferred_element_type=f32)
    lg = r1[:ROUTER_ROWS] + r1[ROUTER_ROWS:] + r2 + br_ref[:, 0:1]
    row = lax.broadcasted_iota(i32, (SUBLANES, tm), 0).astype(f32)
    big = float(SUBLANES)
    gl = jnp.where(row < N_EXPERT_GROUPS, lg[0:SUBLANES], NEG)
    gmax = jnp.max(gl, axis=0, keepdims=True)
    g_p = 1.0 / jnp.sum(jnp.exp(gl - gmax), axis=0, keepdims=True)
    gidx = jnp.min(jnp.where(gl == gmax, row, big), axis=0, keepdims=True)
    esel = jnp.zeros((SUBLANES, tm), f32)
    for g in range(N_EXPERT_GROUPS):
        esel = jnp.where(gidx == float(g), lg[SUBLANES * (g + 1):SUBLANES * (g + 2)], esel)
    e1 = jnp.max(esel, axis=0, keepdims=True)
    i1 = jnp.min(jnp.where(esel == e1, row, big), axis=0, keepdims=True)
    esel2 = jnp.where(row == i1, -jnp.inf, esel)
    e2 = jnp.max(esel2, axis=0, keepdims=True)
    i2 = jnp.min(jnp.where(esel2 == e2, row, big), axis=0, keepdims=True)
    t = jnp.exp(e2 - e1)
    w1 = g_p / (1.0 + t)
    ids = jnp.concatenate([gidx * EXPERTS_PER_GROUP + i1, gidx * EXPERTS_PER_GROUP + i2], axis=0).astype(i32)
    ids_ref[0] = ids
    gates_ref[0] = jnp.concatenate([w1, w1 * t], axis=0)
    e_iota = lax.broadcasted_iota(i32, (N_EXPERTS, tm), 0)
    onehot = jnp.where((e_iota == ids[0:1]) | (e_iota == ids[1:2]), 1.0, 0.0).astype(bf16)
    cnt_ref[0] = jnp.dot(onehot, jnp.ones((tm, LANES), bf16), preferred_element_type=f32)


def _out_route(oa, ob, x, gate1, shift2, scale2, w_a, w_b, g2, wr_t, br_b, *, tm):
    n, t, d = x.shape
    w_half = oa.shape[2]
    mod_rows = gate1.shape[1]
    tpn = t // tm

    def mod_spec():
        if mod_rows == 1:
            return pl.BlockSpec((1, 1, d), lambda b, i: (b, 0, 0))
        return pl.BlockSpec((1, tm, d), lambda b, i: (b, i, 0))

    full = lambda shp: pl.BlockSpec(shp, lambda b, i: (0,) * len(shp))
    tok = lambda w: pl.BlockSpec((1, tm, w), lambda b, i: (b, i, 0))
    return pl.pallas_call(
        _out_route_kernel,
        grid=(n, tpn),
        in_specs=[tok(w_half), tok(w_half), tok(d), mod_spec(), mod_spec(), mod_spec(),
                  full((w_half, d)), full((w_half, d)), full((1, d)),
                  full((ROUTER_ROWS, d)), full((ROUTER_ROWS, LANES))],
        out_specs=[tok(d), tok(d),
                   pl.BlockSpec((1, 2, tm), lambda b, i: (b * tpn + i, 0, 0)),
                   pl.BlockSpec((1, 2, tm), lambda b, i: (b * tpn + i, 0, 0)),
                   pl.BlockSpec((1, N_EXPERTS, LANES), lambda b, i: (b * tpn + i, 0, 0))],
        out_shape=[SDS((n, t, d), f32), SDS((n, t, d), bf16), SDS((n * tpn, 2, tm), i32),
                   SDS((n * tpn, 2, tm), f32), SDS((n * tpn, N_EXPERTS, LANES), f32)],
        compiler_params=_cparams(("arbitrary", "arbitrary")),
        name="out_route",
    )(oa, ob, x, gate1, shift2, scale2, w_a, w_b, g2, wr_t, br_b)


def _chunk_list(counts, src0, dst0, rows, kmax):
    cin = jnp.cumsum(counts, axis=1)
    cex = cin - counts
    k = jnp.arange(kmax, dtype=i32)
    owner = jnp.sum((cin[:, None, :] <= k[None, :, None]).astype(i32), axis=2)
    owner = jnp.minimum(owner, N_EXPERTS - 1)
    pick = (owner[:, :, None] == jnp.arange(N_EXPERTS, dtype=i32)).astype(i32)
    at_owner = lambda v: jnp.sum(pick * v[:, None, :], axis=2)
    within = (k[None, :] - at_owner(cex)) * rows
    src = at_owner(src0) + within
    dst = at_owner(dst0) + within
    return src.reshape(-1).astype(i32), dst.reshape(-1).astype(i32), cin[:, -1].astype(i32)


def _route_plan(cnt, nb_max, sorted_rows):
    pc = _round_up(cnt, RUN_ALIGN)
    toff = jnp.cumsum(pc, axis=1) - pc
    tot = jnp.sum(pc, axis=0)
    seg = _round_up(tot, EXPERT_BLOCK)
    seg_end = jnp.cumsum(seg)
    seg_start = seg_end - seg
    run_start = seg_start[None, :] + jnp.cumsum(pc, axis=0) - pc
    nb = seg_end[-1] // EXPERT_BLOCK
    lists = []
    done = jnp.zeros_like(pc)
    for rows in CHUNK_ROWS:
        n = (pc - done) // rows
        kmax = sorted_rows // rows if rows == CHUNK_ROWS[0] else N_EXPERTS
        lists.extend(_chunk_list(n, toff + done, run_start + done, rows, kmax))
        done = done + n * rows
    return dict(
        lists=lists,
        tail_start=(seg_start + tot).astype(i32), tail_chunks=((seg - tot) // RUN_ALIGN).astype(i32),
        tail_total=(jnp.sum(seg - tot) // RUN_ALIGN).reshape(1).astype(i32),
        seg_start=seg_start.astype(i32), seg_blocks=(seg // EXPERT_BLOCK).astype(i32),
        nb=nb.reshape(1).astype(i32),
        toff_v=jnp.broadcast_to(toff.astype(f32)[:, :, None], toff.shape + (LANES,)))


def _sorted_positions(ids_ref, toff_ref, tm):
    ids = ids_ref[0]
    idc = jnp.concatenate([ids[0:1], ids[1:2]], axis=1)
    n_blk = 2 * tm // LANES
    e_iota = lax.broadcasted_iota(i32, (N_EXPERTS, 2 * tm), 0)
    onehot = jnp.where(e_iota == idc, 1.0, 0.0)
    stacked = jnp.concatenate([onehot[:, k * LANES:(k + 1) * LANES] for k in range(n_blk)], axis=0).astype(bf16)
    src = lax.broadcasted_iota(i32, (LANES, LANES), 0)
    dst = lax.broadcasted_iota(i32, (LANES, LANES), 1)
    within = jnp.dot(stacked, jnp.where(src < dst, 1.0, 0.0).astype(bf16), preferred_element_type=f32)
    totals = jnp.dot(stacked, jnp.ones((LANES, LANES), bf16), preferred_element_type=f32)
    r = lax.broadcasted_iota(i32, (n_blk * N_EXPERTS, n_blk * N_EXPERTS), 0)
    c = lax.broadcasted_iota(i32, (n_blk * N_EXPERTS, n_blk * N_EXPERTS), 1)
    same_expert_earlier = ((r & (N_EXPERTS - 1)) == (c & (N_EXPERTS - 1))) & (c < r)
    block_off = jnp.dot(jnp.where(same_expert_earlier, 1.0, 0.0).astype(bf16), totals.astype(bf16),
                        preferred_element_type=f32)
    before2 = within + block_off
    before = jnp.concatenate([before2[k * N_EXPERTS:(k + 1) * N_EXPERTS] for k in range(n_blk)], axis=1)
    pos = jnp.sum(onehot * (before + toff_ref[0][:, 0:1]), axis=0, keepdims=True)
    return jnp.where(idc >= 0, pos, -1.0)


def _aligned(row):
    return row if isinstance(row, int) else pl.multiple_of(row, RUN_ALIGN)


def _run_chunk_copy(vmem_buf, hbm_buf, sem, vrow, hrow, to_hbm, rows=RUN_ALIGN):
    v = vmem_buf.at[pl.ds(_aligned(vrow), rows)]
    h = hbm_buf.at[pl.ds(_aligned(hrow), rows)]
    return pltpu.make_async_copy(v, h, sem) if to_hbm else pltpu.make_async_copy(h, v, sem)


def _tile_chunks(i, list_refs, vmem_buf, hbm_buf, sems, to_hbm, wait):
    for c, rows in enumerate(CHUNK_ROWS):
        src_ref, dst_ref, n_ref = list_refs[3 * c:3 * c + 3]
        stride = src_ref.shape[0] // n_ref.shape[0]

        def one(k, carry, rows=rows, src_ref=src_ref, dst_ref=dst_ref, stride=stride, sem=sems.at[c]):
            if wait:
                _run_chunk_copy(vmem_buf, hbm_buf, sem, 0, 0, to_hbm, rows).wait()
            else:
                _run_chunk_copy(vmem_buf, hbm_buf, sem, src_ref[i * stride + k], dst_ref[i * stride + k],
                                to_hbm, rows).start()
            return carry

        lax.fori_loop(0, n_ref[i], one, 0)


def _dispatch_kernel(*refs):
    lists = refs[:N_LIST]
    tstart_ref, tail_chunks_ref, ttot_ref, nb_ref = refs[N_LIST:N_LIST + 4]
    hm_ref, ht_ref, ids_ref, toff_ref, xs_ref, xbuf_sc, zero_sc, sem = refs[N_LIST + 4:]
    i = pl.program_id(0)
    last = pl.num_programs(0) - 1
    tm = hm_ref.shape[0]
    rt = xbuf_sc.shape[1]
    slot = i & 1
    sem_tail, sem_dead = sem.at[len(CHUNK_ROWS)], sem.at[len(CHUNK_ROWS) + 1]

    @pl.when(i == 0)
    def _():
        zero_sc[...] = jnp.zeros(zero_sc.shape, ROW_DTYPE)

    h = jnp.where(i == last, ht_ref[...], hm_ref[...])
    pos = _sorted_positions(ids_ref, toff_ref, tm)
    r_iota = lax.broadcasted_iota(i32, (rt, tm), 0).astype(f32)
    perm = jnp.where((r_iota == pos[:, :tm]) | (r_iota == pos[:, tm:]), 1.0, 0.0).astype(bf16)
    xbuf_sc[slot] = jnp.dot(perm, h, preferred_element_type=f32).astype(ROW_DTYPE)

    @pl.when(i > 0)
    def _():
        _tile_chunks(i - 1, lists, xbuf_sc.at[1 - slot], xs_ref, sem, True, wait=True)

    _tile_chunks(i, lists, xbuf_sc.at[slot], xs_ref, sem, True, wait=False)

    @pl.when(i == last)
    def _():
        _tile_chunks(i, lists, xbuf_sc.at[slot], xs_ref, sem, True, wait=True)

        def per_expert(e, c):
            def per_chunk(k, cc):
                _run_chunk_copy(zero_sc, xs_ref, sem_tail, 0, tstart_ref[e] + k * RUN_ALIGN, True).start()
                return cc
            lax.fori_loop(0, tail_chunks_ref[e], per_chunk, 0)
            return c
        lax.fori_loop(0, N_EXPERTS, per_expert, 0)

        def wait_one(k, c):
            _run_chunk_copy(zero_sc, xs_ref, sem_tail, 0, 0, True).wait()
            return c
        lax.fori_loop(0, ttot_ref[0], wait_one, 0)

        def dead_block(row):
            return pltpu.make_async_copy(zero_sc, xs_ref.at[pl.ds(row, EXPERT_BLOCK)], sem_dead)

        n_dead = xs_ref.shape[0] // EXPERT_BLOCK - nb_ref[0]

        def start_dead(k, c):
            dead_block(pl.multiple_of((nb_ref[0] + k) * EXPERT_BLOCK, EXPERT_BLOCK)).start()
            return c
        lax.fori_loop(0, n_dead, start_dead, 0)

        def wait_dead(k, c):
            dead_block(0).wait()
            return c
        lax.fori_loop(0, n_dead, wait_dead, 0)


def _sorted_rows(tm):
    return _round_up(2 * tm + N_EXPERTS * (RUN_ALIGN - 1), LANES)


def _dispatch(plan, h_main, h_tail, ids, toff_v, rows_total):
    tiles, _, tm = ids.shape
    d = h_main.shape[1]
    tiles_main = h_main.shape[0] // tm
    assert tiles == tiles_main + 1 and h_tail.shape[0] == tm
    rt = _sorted_rows(tm)
    grid_spec = pltpu.PrefetchScalarGridSpec(
        num_scalar_prefetch=N_LIST + 4,
        grid=(tiles,),
        in_specs=[pl.BlockSpec((tm, d), lambda i, *_: (jnp.minimum(i, tiles_main - 1), 0)),
                  pl.BlockSpec((tm, d), lambda i, *_: (0, 0)),
                  pl.BlockSpec((1, 2, tm), lambda i, *_: (i, 0, 0)),
                  pl.BlockSpec((1, N_EXPERTS, LANES), lambda i, *_: (i, 0, 0))],
        out_specs=pl.BlockSpec(memory_space=pl.ANY),
        scratch_shapes=[pltpu.VMEM((2, rt, d), ROW_DTYPE),
                        pltpu.VMEM((EXPERT_BLOCK, d), ROW_DTYPE), pltpu.SemaphoreType.DMA((len(CHUNK_ROWS) + 2,))],
    )
    return pl.pallas_call(
        _dispatch_kernel,
        grid_spec=grid_spec,
        out_shape=SDS((rows_total, d), ROW_DTYPE),
        compiler_params=_cparams(("arbitrary",)),
        name="dispatch",
    )(*plan["lists"], plan["tail_start"], plan["tail_chunks"], plan["tail_total"], plan["nb"],
      h_main, h_tail, ids, toff_v)


def _experts_kernel(seg_ref, nblk_ref, nb_ref, x_ref, wg_ref, wu_ref, wd_ref, y_ref,
                    wg_sc, wu_sc, wd_sc, xbuf, ybuf, sem_in, sem_out):
    e = pl.program_id(0)
    eb = EXPERT_BLOCK
    nblk = nblk_ref[e]
    g0 = seg_ref[e] // eb
    n_all = nb_ref[0]

    def rows(g):
        return pl.ds(pl.multiple_of(g * eb, eb), eb)

    def x_copy(g):
        slot = lax.rem(g, X_AHEAD + 1)
        return pltpu.make_async_copy(x_ref.at[rows(g)], xbuf.at[slot], sem_in.at[slot])

    def y_copy(g):
        return pltpu.make_async_copy(ybuf.at[g & 1], y_ref.at[rows(g)], sem_out.at[g & 1])

    @pl.when(e == 0)
    def _():
        for g in range(X_AHEAD):
            @pl.when(g < n_all)
            def _():
                x_copy(g).start()

    @pl.when(nblk > 0)
    def _():
        wg_sc[...] = wg_ref[0].astype(bf16)
        wu_sc[...] = wu_ref[0].astype(bf16)
        wd_sc[...] = wd_ref[0].astype(bf16)

        def block(b, c):
            g = g0 + b
            x_copy(g).wait()

            @pl.when(g + X_AHEAD < n_all)
            def _():
                x_copy(g + X_AHEAD).start()

            xb = xbuf[lax.rem(g, X_AHEAD + 1)]
            a = jnp.dot(xb, wg_sc[...], preferred_element_type=f32)
            u = jnp.dot(xb, wu_sc[...], preferred_element_type=f32)
            hid = (jax.nn.silu(a) * u).astype(bf16)
            y = jnp.dot(hid, wd_sc[...], preferred_element_type=f32)

            @pl.when(g >= 2)
            def _():
                y_copy(g - 2).wait()

            ybuf[g & 1] = y.astype(ROW_DTYPE)
            y_copy(g).start()
            return c

        lax.fori_loop(0, nblk, block, 0)

    @pl.when(e == pl.num_programs(0) - 1)
    def _():
        @pl.when(n_all >= 2)
        def _():
            y_copy(n_all - 2).wait()

        @pl.when(n_all >= 1)
        def _():
            y_copy(n_all - 1).wait()

        def dead_block(b):
            return pltpu.make_async_copy(ybuf.at[0], y_ref.at[pl.ds(pl.multiple_of(b * eb, eb), eb)], sem_out.at[0])

        n_dead = y_ref.shape[0] // eb - nb_ref[0]
        ybuf[0] = jnp.zeros(ybuf.shape[1:], ROW_DTYPE)

        def start_dead(k, c):
            dead_block(nb_ref[0] + k).start()
            return c
        lax.fori_loop(0, n_dead, start_dead, 0)

        def wait_dead(k, c):
            dead_block(0).wait()
            return c
        lax.fori_loop(0, n_dead, wait_dead, 0)


def _experts(plan, xs, w_gate, w_up, w_down):
    rows_total, d = xs.shape
    n_exp, _, de = w_gate.shape
    eb = EXPERT_BLOCK
    grid_spec = pltpu.PrefetchScalarGridSpec(
        num_scalar_prefetch=3,
        grid=(n_exp,),
        in_specs=[pl.BlockSpec(memory_space=pl.ANY),
                  pl.BlockSpec((1, d, de), lambda e, *_: (e, 0, 0)),
                  pl.BlockSpec((1, d, de), lambda e, *_: (e, 0, 0)),
                  pl.BlockSpec((1, de, d), lambda e, *_: (e, 0, 0))],
        out_specs=pl.BlockSpec(memory_space=pl.ANY),
        scratch_shapes=[pltpu.VMEM((d, de), bf16), pltpu.VMEM((d, de), bf16), pltpu.VMEM((de, d), bf16),
                        pltpu.VMEM((X_AHEAD + 1, eb, d), ROW_DTYPE), pltpu.VMEM((2, eb, d), ROW_DTYPE),
                        pltpu.SemaphoreType.DMA((X_AHEAD + 1,)), pltpu.SemaphoreType.DMA((2,))],
    )
    return pl.pallas_call(
        _experts_kernel,
        grid_spec=grid_spec,
        out_shape=SDS((rows_total, d), ROW_DTYPE),
        compiler_params=_cparams(("arbitrary",)),
        name="experts",
    )(plan["seg_start"], plan["seg_blocks"], plan["nb"], xs, w_gate, w_up, w_down)


def _combine_kernel(*refs, final):
    lists = refs[:N_LIST]
    (xm_ref, xt_ref, gm_ref, gt_ref, gf_ref, ids_ref, gates_ref, toff_ref, y_ref,
     om_ref, ot_ref, ybuf_sc, sem) = refs[N_LIST:]
    i = pl.program_id(0)
    last = pl.num_programs(0) - 1
    tm = xm_ref.shape[0]
    rt = ybuf_sc.shape[1]
    slot = i & 1

    @pl.when(i == 0)
    def _():
        ybuf_sc[...] = jnp.zeros(ybuf_sc.shape, ROW_DTYPE)
        _tile_chunks(0, lists, ybuf_sc.at[0], y_ref, sem, False, wait=False)

    _tile_chunks(i, lists, ybuf_sc.at[slot], y_ref, sem, False, wait=True)

    @pl.when(i < last)
    def _():
        _tile_chunks(i + 1, lists, ybuf_sc.at[1 - slot], y_ref, sem, False, wait=False)

    pos = _sorted_positions(ids_ref, toff_ref, tm)
    gates = gates_ref[0]
    r_iota = lax.broadcasted_iota(i32, (rt, tm), 0).astype(f32)
    weights = (jnp.where(r_iota == pos[:, :tm], gates[0:1], 0.0)
               + jnp.where(r_iota == pos[:, tm:], gates[1:2], 0.0)).astype(bf16)
    ff = lax.dot_general(weights, ybuf_sc[slot], (((0,), (0,)), ((), ())), preferred_element_type=f32)

    def finish(x1, gate):
        x2 = x1 + gate * ff
        return _rms(x2, gf_ref[...]) if final else x2

    @pl.when(i < last)
    def _():
        om_ref[...] = finish(xm_ref[...], gm_ref[0])

    @pl.when(i == last)
    def _():
        ot_ref[...] = finish(xt_ref[...], gt_ref[...])


def _combine(plan, x_main, x_tail, gate_main, gate_tail, g_final, ids, gates, toff_v, y, *, final):
    tiles, _, tm = ids.shape
    d = x_main.shape[1]
    tiles_main = x_main.shape[0] // tm
    tpr = tiles_main // gate_main.shape[0]
    rt = _sorted_rows(tm)
    main_idx = lambda i, *_: (jnp.minimum(i, tiles_main - 1), 0)
    tail_spec = pl.BlockSpec((tm, d), lambda i, *_: (0, 0))
    slot_spec = pl.BlockSpec((1, 2, tm), lambda i, *_: (i, 0, 0))
    grid_spec = pltpu.PrefetchScalarGridSpec(
        num_scalar_prefetch=N_LIST,
        grid=(tiles,),
        in_specs=[pl.BlockSpec((tm, d), main_idx), tail_spec,
                  pl.BlockSpec((1, 1, d), lambda i, *_: (jnp.minimum(i, tiles_main - 1) // tpr, 0, 0)), tail_spec,
                  pl.BlockSpec((1, d), lambda i, *_: (0, 0)), slot_spec, slot_spec,
                  pl.BlockSpec((1, N_EXPERTS, LANES), lambda i, *_: (i, 0, 0)),
                  pl.BlockSpec(memory_space=pl.ANY)],
        out_specs=[pl.BlockSpec((tm, d), main_idx), tail_spec],
        scratch_shapes=[pltpu.VMEM((2, rt, d), ROW_DTYPE), pltpu.SemaphoreType.DMA((len(CHUNK_ROWS),))],
    )
    return pl.pallas_call(
        functools.partial(_combine_kernel, final=final),
        grid_spec=grid_spec,
        out_shape=[SDS(x_main.shape, f32), SDS((tm, d), f32)],
        compiler_params=_cparams(("arbitrary",)),
        name="combine",
    )(*plan["lists"], x_main, x_tail, gate_main, gate_tail, g_final, ids, gates, toff_v, y)


def _moe(main, tail, g_final, w_gate, w_up, w_down, *, tm, final):
    n, t, d = main["x1"].shape
    t_tail = tail["x1"].shape[1]
    pad = tm - t_tail
    tiles = n * t // tm + 1
    rows_max = 2 * (n * t + t_tail) + tiles * N_EXPERTS * (RUN_ALIGN - 1) + N_EXPERTS * (EXPERT_BLOCK - 1)
    nb_max = -(-rows_max // EXPERT_BLOCK)
    rows2 = lambda a: jnp.pad(a.reshape(t_tail, d), ((0, pad), (0, 0)))
    ids = jnp.concatenate([main["ids"], jnp.pad(tail["ids"], ((0, 0), (0, 0), (0, pad)), constant_values=-1)])
    gates = jnp.concatenate([main["gates"], jnp.pad(tail["gates"], ((0, 0), (0, 0), (0, pad)))])
    cnt = jnp.concatenate([main["cnt"], tail["cnt"]])[:, :, 0].astype(i32)
    plan = _route_plan(cnt, nb_max, _sorted_rows(tm))
    xs = _dispatch(plan, main["h2"].reshape(n * t, d), rows2(tail["h2"]), ids, plan["toff_v"],
                   nb_max * EXPERT_BLOCK)
    y = _experts(plan, xs, w_gate, w_up, w_down)
    out_m, out_t = _combine(plan, main["x1"].reshape(n * t, d), rows2(tail["x1"]), main["gate"],
                            rows2(tail["gate"]), g_final, ids, gates, plan["toff_v"], y, final=final)
    return out_m.reshape(n, t, d), out_t[:t_tail].reshape(1, t_tail, d)


def _before_attention(x, mods, tables, p, *, tm, mix, emit_vn):
    ws_t, mask, bs_b = mix
    return _in_proj(x, mods[0], mods[1], p["g1"], p["w_in"], tables, ws_t, mask, bs_b, p["g_v"], p["g_mlp"],
                    tm=tm, emit_vn=emit_vn)


def _after_attention(x, oa, ob, mods, p, *, tm):
    x1, h2, ids, gates, cnt = _out_route(oa, ob, x, mods[2], mods[3], mods[4], p["w_out_a"], p["w_out_b"],
                                         p["g2"], p["wr_t"], p["br_b"], tm=tm)
    return dict(x1=x1, h2=h2, ids=ids, gates=gates, cnt=cnt, gate=mods[5])


def kernel(x_prompt, x_sample, cache_k, cache_v, page_table, c_prompt, c_sample, w_ada, b_ada, g_norm1, w_in,
           lam_p, g_attn, g_v, w_s, b_s, g_mlp, w_out, g_norm2, w_router_g, b_router_g, w_router_e, b_router_e,
           w_exp_gate, w_exp_up, w_exp_down, g_final):
    nb, seq, d = x_prompt.shape
    ndec, dseq, _ = x_sample.shape
    depth = w_ada.shape[0]
    past_len = page_table.shape[1] * PAGE
    n_pool = cache_k.shape[1]
    a_w = N_HEADS * DV
    tm_s = ndec * dseq

    tables_p, tables_s8 = _rope_tables(seq, dseq, past_len)
    tables_s = tuple(jnp.tile(t, (ndec, 1)) for t in tables_s8)
    c_all = jnp.concatenate([c_prompt, c_sample], axis=0)
    c_pad = _round_up(c_all.shape[0], SUBLANES) - c_all.shape[0]
    c_all = jnp.pad(c_all, ((0, c_pad), (0, 0)))

    tri = jnp.tril(jnp.ones((CHUNK, CHUNK), f32))
    idx = jnp.arange(tm_s)
    mask_s = ((idx[:, None] // dseq == idx[None, :] // dseq) & (idx[None, :] % dseq <= idx[:, None] % dseq)).astype(f32)
    sel_s = (idx[:, None] % dseq == jnp.arange(dseq)[None, :]).astype(f32)

    xp, xs = x_prompt, x_sample.reshape(1, tm_s, d)
    kp_l, vp_l, ks_l, vs_l, cv_l = [], [], [], [], []
    for l in range(depth):
        lam_init = 0.8 - 0.6 * math.exp(-0.3 * l)
        mod = _adaln(c_all, w_ada[l], b_ada[l])
        mods_p = [mod[:nb, None, j * d:(j + 1) * d] for j in range(6)]
        mods_s = [jnp.repeat(mod[nb:nb + ndec, j * d:(j + 1) * d], dseq, axis=0)[None] for j in range(6)]
        wr_t = jnp.concatenate([w_router_g[l].T, jnp.zeros((SUBLANES - N_EXPERT_GROUPS, d), f32),
                                w_router_e[l].T], axis=0)
        br = jnp.concatenate([b_router_g[l], jnp.zeros((SUBLANES - N_EXPERT_GROUPS,), f32), b_router_e[l]])
        p = dict(
            g1=g_norm1[l].reshape(1, d), w_in=w_in[l].astype(bf16), g_v=g_v[l].reshape(1, -1),
            g_mlp=g_mlp[l].reshape(1, -1), w_out_a=w_out[l][:a_w].astype(bf16), w_out_b=w_out[l][a_w:].astype(bf16),
            g2=g_norm2[l].reshape(1, d), wr_t=wr_t, br_b=jnp.broadcast_to(br[:, None], (ROUTER_ROWS, LANES)),
            w_gate=w_exp_gate[l], w_up=w_exp_up[l], w_down=w_exp_down[l])
        gf = g_final.reshape(1, d)
        mix_p = (w_s[l], tri, jnp.broadcast_to(b_s[l][:, :, None], (N_GROUPS_B, CHUNK, C_B)))
        rep = lambda eq, *ops: jnp.einsum(eq, *ops, precision=lax.Precision.HIGHEST)
        mix_s = (rep("ia,gab,jb->gij", sel_s, w_s[l][:, :dseq, :dseq], sel_s), mask_s,
                 jnp.broadcast_to(rep("ia,ga->gi", sel_s, b_s[l][:, :dseq])[:, :, None], (N_GROUPS_B, tm_s, C_B)))

        ck = jnp.transpose(cache_k[l], (0, 2, 3, 4, 1)).reshape(n_pool, -1, PAGE)
        cv = cache_v[l].reshape(n_pool, PAGE * N_HEADS, DV)

        q_p, k_p, v_p, kb_p, vb_p, ob_p = _before_attention(xp, mods_p, tables_p, p, tm=TOKEN_TILE, mix=mix_p,
                                                            emit_vn=False)
        q_s, k_s, v_s, _, _, ob_s, cv_s = _before_attention(xs, mods_s, tables_s, p, tm=tm_s, mix=mix_s,
                                                            emit_vn=True)
        per_seq = lambda a: a.reshape(ndec, dseq, -1)
        oa_p, oa_s = _attention(page_table, lam_p[l], g_attn[l], q_p, kb_p, vb_p, per_seq(q_s), per_seq(k_s),
                                per_seq(v_s), ck, cv, lam_init)
        routed_p = _after_attention(xp, oa_p, ob_p, mods_p, p, tm=TOKEN_TILE)
        routed_s = _after_attention(xs, oa_s.reshape(1, tm_s, -1), ob_s, mods_s, p, tm=tm_s)
        xp, xs = _moe(routed_p, routed_s, gf, p["w_gate"], p["w_up"], p["w_down"], tm=TOKEN_TILE,
                      final=l == depth - 1)
        kp_l.append(k_p.reshape(nb, seq, N_HEADS, 2, DK))
        vp_l.append(v_p.reshape(nb, seq, N_HEADS, DV))
        ks_l.append(k_s.reshape(ndec, dseq, N_HEADS, 2, DK))
        vs_l.append(v_s.reshape(ndec, dseq, N_HEADS, DV))
        cv_l.append(cv_s.reshape(ndec, dseq, -1))
    return (xp, xs.reshape(ndec, dseq, d), jnp.stack(kp_l), jnp.stack(vp_l), jnp.stack(ks_l), jnp.stack(vs_l),
            jnp.stack(cv_l))
```

```python
import functools
import math

import jax
import jax.numpy as jnp
from jax import lax
from jax.experimental import pallas as pl
from jax.experimental.pallas import tpu as pltpu

f32 = jnp.float32
bf16 = jnp.bfloat16
i32 = jnp.int32
SDS = jax.ShapeDtypeStruct

N_HEADS = 4
DK = 64
DV = 2 * DK
ROT = DK // 4
ROPE_THETA = 500000.0
N_GROUPS_B = 4
C_B = 128
CHUNK = 128
N_EXPERT_GROUPS = 4
EXPERTS_PER_GROUP = 8
N_EXPERTS = N_EXPERT_GROUPS * EXPERTS_PER_GROUP
PAGE = 128
EPS = 1e-6
NEG = -1e30
LOG2E = math.log2(math.e)

LANES = 128
SUBLANES = 8
VMEM_LIMIT = 56 * 1024 * 1024
ATTN_VMEM_LIMIT = 60 * 1024 * 1024

TOKEN_TILE = 512
ATTN_TQ = 256
ATTN_TK = 256
ONES_ROWS = 16
PAGES_PER_STEP = 16
PAGE_SETS = 3
EXPERT_BLOCK = 256
X_AHEAD = 5
RUN_ALIGN = 16
CHUNK_ROWS = (32, 16)
ROW_DTYPE = bf16
N_LIST = 3 * len(CHUNK_ROWS)
ROUTER_ROWS = 40


def _cparams(sem=None):
    return pltpu.CompilerParams(dimension_semantics=sem, vmem_limit_bytes=VMEM_LIMIT)


def _round_up(x, m):
    return (x + m - 1) // m * m


def _adaln_kernel(c_ref, w_ref, b_ref, o_ref):
    a = jax.nn.silu(c_ref[...])
    w = w_ref[...]
    a_hi = a.astype(bf16)
    a_lo = (a - a_hi.astype(f32)).astype(bf16)
    w_hi = w.astype(bf16)
    w_lo = (w - w_hi.astype(f32)).astype(bf16)
    n = a.shape[0]
    r = jnp.dot(jnp.concatenate([a_hi, a_lo], axis=0), w_hi, preferred_element_type=f32)
    o_ref[...] = r[:n] + r[n:] + jnp.dot(a_hi, w_lo, preferred_element_type=f32) + b_ref[...]


def _adaln(c_all, w_ada, b_ada):
    n, d = c_all.shape
    m = w_ada.shape[1]
    tn = 1536
    return pl.pallas_call(
        _adaln_kernel,
        grid=(m // tn,),
        in_specs=[pl.BlockSpec((n, d), lambda j: (0, 0)),
                  pl.BlockSpec((d, tn), lambda j: (0, j)),
                  pl.BlockSpec((1, tn), lambda j: (0, j))],
        out_specs=pl.BlockSpec((n, tn), lambda j: (0, j)),
        out_shape=SDS((n, m), f32),
        compiler_params=_cparams(("arbitrary",)),
        name="adaln",
    )(c_all, w_ada, b_ada.reshape(1, m))


def _rope_kernel(inv_ref, cos_ref, sin_ref, *, rows_prompt, past_len):
    shape = cos_ref.shape
    r = lax.broadcasted_iota(i32, shape, 0)
    l = lax.broadcasted_iota(i32, shape, 1)
    base = jnp.where(r < rows_prompt, r * 16, past_len + (r - rows_prompt) * 16)
    pos = base + (l >> 3)
    ang = pos.astype(f32) * inv_ref[...]
    cos_ref[...] = jnp.cos(ang)
    sin_ref[...] = jnp.sin(ang)


def _rope_tables(seq, dec_seq, past_len):
    inv = ROPE_THETA ** (-jnp.arange(0, ROT, 2, dtype=f32) / ROT)
    inv_lane = jnp.tile(inv, LANES // (ROT // 2)).reshape(1, LANES)
    rp = seq // 16
    rt = rp + SUBLANES
    cos_c, sin_c = pl.pallas_call(
        functools.partial(_rope_kernel, rows_prompt=rp, past_len=past_len),
        out_shape=(SDS((rt, LANES), f32), SDS((rt, LANES), f32)),
        name="rope_table",
    )(inv_lane)

    def expand(c8, s8):
        n = c8.shape[0]
        z8 = jnp.zeros((n, ROT // 2), f32)
        rest = DK - ROT
        cos_t = jnp.concatenate([c8, c8, jnp.ones((n, rest), f32)], axis=1)
        sin_a = jnp.concatenate([-s8, z8, jnp.zeros((n, rest), f32)], axis=1)
        sin_b = jnp.concatenate([z8, s8, jnp.zeros((n, rest), f32)], axis=1)
        return tuple(jnp.tile(t, (1, LANES // DK)) for t in (cos_t, sin_a, sin_b))

    half = ROT // 2
    prompt = expand(cos_c[:rp].reshape(seq, half), sin_c[:rp].reshape(seq, half))
    sample = expand(cos_c[rp].reshape(16, half)[:dec_seq], sin_c[rp].reshape(16, half)[:dec_seq])
    return prompt, sample


def _rms(x, g):
    return x * lax.rsqrt(jnp.mean(x * x, axis=-1, keepdims=True) + EPS) * g


def _in_proj_kernel(x_ref, sh_ref, sc_ref, g1_ref, w_ref, cos_ref, sa_ref, sb_ref,
                    ws_ref, msk_ref, bs_ref, gv_ref, gm_ref,
                    q_ref, k_ref, v_ref, kb_ref, vb_ref, ob_ref, *vn_refs, mix_rows):
    x = x_ref[0]
    tm = x.shape[0]
    h = _rms(x, g1_ref[...]) * (1.0 + sc_ref[0]) + sh_ref[0]
    z = jnp.dot(h.astype(bf16), w_ref[...], preferred_element_type=f32)
    qk_w = N_HEADS * 2 * DK
    a_w = N_HEADS * DV
    b_w = N_GROUPS_B * C_B
    cos_t, sin_a, sin_b = cos_ref[...], sa_ref[...], sb_ref[...]
    for s in range(2 * qk_w // LANES):
        zs = z[:, s * LANES:(s + 1) * LANES]
        rot = zs * cos_t + pltpu.roll(zs, LANES - ROT // 2, 1) * sin_a + pltpu.roll(zs, ROT // 2, 1) * sin_b
        if s < qk_w // LANES:
            q_ref[0, :, s * LANES:(s + 1) * LANES] = (rot * (DK ** -0.5 * LOG2E)).astype(bf16)
        else:
            k_ref[0, :, s * LANES - qk_w:(s + 1) * LANES - qk_w] = rot
            kb_ref[0, :, s * LANES - qk_w:(s + 1) * LANES - qk_w] = rot.astype(bf16)
    v = z[:, 2 * qk_w:2 * qk_w + a_w]
    for h in range(N_HEADS):
        v_ref[0, pl.ds(h, tm, stride=N_HEADS), :] = v[:, h * DV:(h + 1) * DV]
    vb_ref[0] = v.astype(bf16)
    uv = jax.nn.gelu(z[:, 2 * qk_w + a_w:])
    nblk = tm // mix_rows
    for g in range(N_GROUPS_B):
        u = uv[:, g * C_B:(g + 1) * C_B]
        vn = _rms(uv[:, b_w + g * C_B:b_w + (g + 1) * C_B], gv_ref[:, g * C_B:(g + 1) * C_B])
        if vn_refs:
            vn_refs[0][0, :, g * C_B:(g + 1) * C_B] = vn
        wm = (ws_ref[g] * msk_ref[...]).astype(bf16)
        vcat = jnp.concatenate([vn[j * mix_rows:(j + 1) * mix_rows] for j in range(nblk)], axis=1)
        mixed = jnp.dot(wm, vcat.astype(bf16), preferred_element_type=f32)
        for j in range(nblk):
            t = u[j * mix_rows:(j + 1) * mix_rows] * (mixed[:, j * C_B:(j + 1) * C_B] + bs_ref[g])
            ob_ref[0, j * mix_rows:(j + 1) * mix_rows, g * C_B:(g + 1) * C_B] = _rms(
                t, gm_ref[:, g * C_B:(g + 1) * C_B]).astype(bf16)


def _in_proj(x, shift, scale, g1, w_in_bf, tables, ws_t, mask, bs_b, g_v, g_mlp, *, tm, emit_vn):
    n, t, d = x.shape
    in_w = w_in_bf.shape[1]
    r = ws_t.shape[1]
    mod_rows = shift.shape[1]
    tab_rows = tables[0].shape[0]
    w_half = N_HEADS * DV

    def mod_spec():
        if mod_rows == 1:
            return pl.BlockSpec((1, 1, d), lambda b, i: (b, 0, 0))
        return pl.BlockSpec((1, tm, d), lambda b, i: (b, i, 0))

    def tab_spec():
        if tab_rows == tm:
            return pl.BlockSpec((tm, LANES), lambda b, i: (0, 0))
        return pl.BlockSpec((tm, LANES), lambda b, i: (i, 0))

    full = lambda shp: pl.BlockSpec(shp, lambda b, i: (0,) * len(shp))
    tok = lambda w: pl.BlockSpec((1, tm, w), lambda b, i: (b, i, 0))
    out_shape = [SDS((n, t, w_half), bf16), SDS((n, t, w_half), f32), SDS((n, t * N_HEADS, DV), f32),
                 SDS((n, t, w_half), bf16), SDS((n, t, w_half), bf16), SDS((n, t, w_half), bf16)]
    out_specs = [tok(w_half)] * 6
    out_specs[2] = pl.BlockSpec((1, tm * N_HEADS, DV), lambda b, i: (b, i, 0))
    if emit_vn:
        out_shape.append(SDS((n, t, w_half), f32))
        out_specs.append(tok(w_half))
    return pl.pallas_call(
        functools.partial(_in_proj_kernel, mix_rows=r),
        grid=(n, t // tm),
        in_specs=[tok(d), mod_spec(), mod_spec(), full((1, d)), full((d, in_w)),
                  tab_spec(), tab_spec(), tab_spec(),
                  full((N_GROUPS_B, r, r)), full((r, r)), full((N_GROUPS_B, r, C_B)),
                  full((1, w_half)), full((1, w_half))],
        out_specs=out_specs,
        out_shape=out_shape,
        compiler_params=_cparams(("arbitrary", "arbitrary")),
        name="in_proj",
    )(x, shift, scale, g1, w_in_bf, *tables, ws_t, mask, bs_b, g_v, g_mlp)


def _diff_lambda(lam_ref, lam_init):
    lp = lam_ref[...]
    s1 = jnp.sum(lp[0:1] * lp[1:2], axis=1, keepdims=True)
    s2 = jnp.sum(lp[2:3] * lp[3:4], axis=1, keepdims=True)
    return jnp.exp(s1) - jnp.exp(s2) + lam_init


def _attn_kernel(pt_ref, lam_ref, g_ref, q_ref, k_ref, v_ref, qs_ref, kn_ref, vn_ref, ck_ref, cv_ref,
                 o_ref, os_ref, vt_sc, qq_sc, acc_sc, m_sc, s_sc, qbd_sc, sm_sc, sl_sc, sacc_sc, kbuf, vbuf, sem,
                 *, lam_init, n_steps):
    b = pl.program_id(0)
    seq = q_ref.shape[1]
    tq, tk = ATTN_TQ, ATTN_TK
    n_qb = seq // tq
    lam = _diff_lambda(lam_ref, lam_init)
    heads = range(N_HEADS)
    nt = (((1,), (1,)), ((), ()))

    seq_per_step, ds, width = qs_ref.shape
    pps = kbuf.shape[1]
    groups = pt_ref.shape[1] // pps
    slots = seq_per_step * groups
    total_slots = n_steps * slots
    rows_h = 2 * ds
    rows = N_HEADS * rows_h

    assert groups & (groups - 1) == 0
    g_shift = groups.bit_length() - 1
    n_sets = kbuf.shape[0]
    g_first = b * slots

    def set_of(g):
        return lax.rem(g, n_sets)

    def page_copies(g, buf):
        sq = g >> g_shift
        first_page = (g & (groups - 1)) * pps
        out = []
        for i in range(pps):
            page = pt_ref[sq, first_page + i]
            out.append(pltpu.make_async_copy(ck_ref.at[page], kbuf.at[buf, i], sem.at[buf]))
            out.append(pltpu.make_async_copy(cv_ref.at[page], vbuf.at[buf, i], sem.at[buf]))
        return out

    def wait_pages(buf):
        for i in range(pps):
            pltpu.make_async_copy(ck_ref.at[0], kbuf.at[buf, i], sem.at[buf]).wait()
            pltpu.make_async_copy(cv_ref.at[0], vbuf.at[buf, i], sem.at[buf]).wait()

    def sample_update(s, value):
        m_prev = sm_sc[...]
        m_new = jnp.maximum(m_prev, jnp.max(s, axis=1, keepdims=True))
        alpha = jnp.exp2(m_prev - m_new)
        p = jnp.exp2(s - m_new)
        sl_sc[...] = alpha * sl_sc[...] + jnp.sum(p, axis=1, keepdims=True)
        pb = p.astype(bf16)
        n_blk = s.shape[1] // PAGE
        parts = [jnp.dot(pb[h * rows_h:(h + 1) * rows_h],
                         jnp.concatenate([value(i, h) for i in range(n_blk)], axis=0), preferred_element_type=f32)
                 for h in heads]
        sacc_sc[...] = sacc_sc[...] * alpha + jnp.concatenate(parts, axis=0)
        sm_sc[...] = m_new

    def slot_begin(t):
        g = g_first + t
        wait_pages(set_of(g))
        g_ahead = g + n_sets - 1
        for cp in page_copies(jnp.minimum(g_ahead, total_slots - 1), set_of(g_ahead)):
            cp.start()

        @pl.when(t & (groups - 1) == 0)
        def _():
            row_i = lax.broadcasted_iota(i32, (rows, width), 0)
            col_i = lax.broadcasted_iota(i32, (rows, width), 1)
            qt = jnp.concatenate([qs_ref[t >> g_shift].astype(f32)] * (N_HEADS * 2), axis=0)
            same_map = (col_i >> (DK.bit_length() - 1)) == (row_i >> (ds.bit_length() - 1))
            qbd_sc[...] = jnp.where(same_map, qt, 0.0).astype(bf16)
            sm_sc[...] = jnp.full(sm_sc.shape, NEG, f32)
            sl_sc[...] = jnp.zeros(sl_sc.shape, f32)
            sacc_sc[...] = jnp.zeros(sacc_sc.shape, f32)

    def slot_hooks(t, n_parts):
        buf = set_of(g_first + t)
        ppp = pps // n_parts
        carried = {}

        def scores_of(part):
            def run():
                k_all = jnp.concatenate([kbuf[buf, part * ppp + i].astype(bf16) for i in range(ppp)], axis=1)
                carried[part] = jnp.dot(qbd_sc[...], k_all, preferred_element_type=f32)
            return run

        def values_of(part):
            def run():
                sample_update(carried.pop(part), lambda i, h: vbuf[
                    buf, part * ppp + i, pl.ds(h, PAGE, stride=N_HEADS), :].astype(bf16))
            return run

        return [f(part) for part in range(n_parts) for f in (scores_of, values_of)]

    def slot_end(t):
        @pl.when(t & (groups - 1) == groups - 1)
        def _():
            sq = t >> g_shift
            qbd = qbd_sc[...]
            pad = jnp.zeros((PAGE - ds, width), f32)
            kn = jnp.concatenate([kn_ref[sq], pad], axis=0).astype(bf16)
            vn = jnp.concatenate([vn_ref[sq], pad], axis=0).astype(bf16)
            s_new = lax.dot_general(qbd, kn, nt, preferred_element_type=f32)
            key = lax.broadcasted_iota(i32, s_new.shape, 1)
            qry = lax.broadcasted_iota(i32, s_new.shape, 0) & (ds - 1)
            sample_update(jnp.where(key <= qry, s_new, NEG), lambda i, h: vn[:, h * DV:(h + 1) * DV])
            out = sacc_sc[...] * (1.0 / sl_sc[...])
            for h in heads:
                o = out[h * rows_h:h * rows_h + ds] - lam * out[h * rows_h + ds:(h + 1) * rows_h]
                os_ref[sq, :, h * DV:(h + 1) * DV] = _rms(o, g_ref[h:h + 1, :] * (1.0 - lam_init)).astype(bf16)

    @pl.when(b == 0)
    def _():
        for g in range(n_sets - 1):
            for cp in page_copies(min(g, total_slots - 1), g):
                cp.start()

    def prep(j, c):
        sl = pl.ds(pl.multiple_of(j * tk, tk), tk)
        for h in heads:
            vt_sc[h, j, :DV] = v_ref[0, sl, h * DV:(h + 1) * DV].T
            vt_sc[h, j, DV:] = jnp.ones((ONES_ROWS, tk), bf16)
        return c

    lax.fori_loop(0, seq // tk, prep, 0)
    lane = lax.broadcasted_iota(i32, (tq, DV), 1)

    def q_block(qi, t, slot_in_diag):
        qsl = pl.ds(pl.multiple_of(qi * tq, tq), tq)
        for h in heads:
            qb = q_ref[0, qsl, h * DV:(h + 1) * DV]
            zero = jnp.zeros_like(qb)
            qq_sc[h, :tq] = jnp.where(lane < DK, qb, zero)
            qq_sc[h, tq:] = jnp.where(lane >= DK, qb, zero)
            m_sc[h] = jnp.full(m_sc.shape[1:], NEG, f32)
            acc_sc[h] = jnp.zeros(acc_sc.shape[1:], f32)

        def scores(j, h):
            ksl = pl.ds(pl.multiple_of(j * tk, tk), tk)
            return lax.dot_general(k_ref[0, ksl, h * DV:(h + 1) * DV], qq_sc[h], nt, preferred_element_type=f32)

        s_sc[...] = scores(0, 0)

        def kv_block(j, masked, hooks=()):
            s_next = None
            for h in heads:
                s = s_sc[...] if h == 0 else s_next
                if h + 1 < N_HEADS:
                    s_next = scores(j, h + 1)
                elif not masked:
                    s_sc[...] = scores(j + 1, 0)
                if h < len(hooks):
                    hooks[h]()
                if masked:
                    key_i = lax.broadcasted_iota(i32, s.shape, 0)
                    qry_i = lax.broadcasted_iota(i32, s.shape, 1) & (tq - 1)
                    s = jnp.where(key_i <= qry_i, s, NEG)
                m_prev = m_sc[h]
                m_new = jnp.maximum(m_prev, jnp.max(s, axis=0, keepdims=True))
                alpha = jnp.exp2(m_prev - m_new)
                p = jnp.exp2(s - m_new)
                pv = jnp.dot(vt_sc[h, j], p.astype(bf16), preferred_element_type=f32)
                acc_sc[h] = acc_sc[h] * alpha + pv
                m_sc[h] = m_new

        def full_block(j, cc):
            kv_block(j, False)
            return cc

        idle = lambda: None

        def full_pair(jj, tt):
            slot_begin(tt)
            slot_scores, slot_values = slot_hooks(tt, 1)
            kv_block(2 * jj, False, [idle, idle, slot_scores])
            kv_block(2 * jj + 1, False)
            slot_values()
            slot_end(tt)
            return tt + 1

        pairs = qi >> 1
        t = lax.fori_loop(0, pairs, full_pair, t)
        lax.fori_loop(2 * pairs, qi, full_block, 0)
        if slot_in_diag:
            slot_begin(t)
            slot_scores, slot_values = slot_hooks(t, 1)
            slot_scores()
            kv_block(qi, True, [idle, idle, slot_values])
        else:
            kv_block(qi, True)
        for h in heads:
            acc = acc_sc[h]
            inv_l = 1.0 / acc[DV:DV + 1, :]
            o_t = acc[:DV, :tq] * inv_l[:, :tq] - lam * (acc[:DV, tq:] * inv_l[:, tq:])
            o_ref[0, qsl, h * DV:(h + 1) * DV] = _rms(o_t.T, g_ref[h:h + 1, :] * (1.0 - lam_init)).astype(bf16)
        if slot_in_diag:
            slot_end(t)
            t = t + 1
        return t

    pair_slots = sum(qi >> 1 for qi in range(n_qb))
    diag_slots = slots - pair_slots
    assert 0 <= diag_slots <= n_qb, (slots, pair_slots)
    t = lax.fori_loop(0, diag_slots, lambda qi, tt: q_block(qi, tt, True), 0)
    lax.fori_loop(diag_slots, n_qb, lambda qi, tt: q_block(qi, tt, False), t)

    @pl.when(b == n_steps - 1)
    def _():
        for g in range(total_slots, total_slots + n_sets - 1):
            wait_pages(g % n_sets)


def _attention(page_table, lam_p, g_attn, q, k, v, q_s, k_new, v_new, cache_kt, cache_vr, lam_init):
    n, seq, w = q.shape
    nseq, ds, _ = q_s.shape
    assert nseq % n == 0 and ds & (ds - 1) == 0 and page_table.shape[1] % PAGES_PER_STEP == 0
    sps = nseq // n
    rows = N_HEADS * 2 * ds
    blk = pl.BlockSpec((1, seq, w), lambda b, pt: (b, 0, 0))
    blk_in = pl.BlockSpec((1, seq, w), lambda b, pt: (b, 0, 0), pipeline_mode=pl.Buffered(1))
    sblk = pl.BlockSpec((sps, ds, w), lambda b, pt: (b, 0, 0))
    full = lambda a: pl.BlockSpec(a.shape, lambda b, pt: (0,) * a.ndim)
    any_spec = pl.BlockSpec(memory_space=pl.ANY)
    grid_spec = pltpu.PrefetchScalarGridSpec(
        num_scalar_prefetch=1,
        grid=(n,),
        in_specs=[full(lam_p), full(g_attn), blk_in, blk_in, blk_in, sblk, sblk, sblk, any_spec, any_spec],
        out_specs=[blk, sblk],
        scratch_shapes=[pltpu.VMEM((N_HEADS, seq // ATTN_TK, DV + ONES_ROWS, ATTN_TK), bf16),
                        pltpu.VMEM((N_HEADS, 2 * ATTN_TQ, DV), bf16),
                        pltpu.VMEM((N_HEADS, DV + ONES_ROWS, 2 * ATTN_TQ), f32),
                        pltpu.VMEM((N_HEADS, 1, 2 * ATTN_TQ), f32),
                        pltpu.VMEM((ATTN_TK, 2 * ATTN_TQ), f32),
                        pltpu.VMEM((rows, w), bf16), pltpu.VMEM((rows, 1), f32), pltpu.VMEM((rows, 1), f32),
                        pltpu.VMEM((rows, DV), f32),
                        pltpu.VMEM((PAGE_SETS, PAGES_PER_STEP) + cache_kt.shape[1:], f32),
                        pltpu.VMEM((PAGE_SETS, PAGES_PER_STEP) + cache_vr.shape[1:], f32),
                        pltpu.SemaphoreType.DMA((PAGE_SETS,))],
    )
    return pl.pallas_call(
        functools.partial(_attn_kernel, lam_init=lam_init, n_steps=n),
        grid_spec=grid_spec,
        out_shape=[SDS((n, seq, w), bf16), SDS((nseq, ds, w), bf16)],
        compiler_params=pltpu.CompilerParams(dimension_semantics=("arbitrary",), vmem_limit_bytes=ATTN_VMEM_LIMIT),
        name="attention",
    )(page_table, lam_p, g_attn, q, k, v, q_s, k_new, v_new, cache_kt, cache_vr)


def _out_route_kernel(oa_ref, ob_ref, x_ref, g1_ref, sh_ref, sc_ref, wa_ref, wb_ref, g2_ref, wr_ref, br_ref,
                      x1_ref, h2_ref, ids_ref, gates_ref, cnt_ref):
    mix = (jnp.dot(oa_ref[0], wa_ref[...], preferred_element_type=f32)
           + jnp.dot(ob_ref[0], wb_ref[...], preferred_element_type=f32))
    x1 = x_ref[0] + g1_ref[0] * mix
    x1_ref[0] = x1
    h2 = _rms(x1, g2_ref[...]) * (1.0 + sc_ref[0]) + sh_ref[0]
    h2_ref[0] = h2.astype(bf16)
    tm = h2.shape[0]
    wr = wr_ref[...]
    wr_hi = wr.astype(bf16)
    wr_lo = (wr - wr_hi.astype(f32)).astype(bf16)
    h_hi = h2.astype(bf16)
    h_lo = (h2 - h_hi.astype(f32)).astype(bf16)
    nt = (((1,), (1,)), ((), ()))
    r1 = lax.dot_general(jnp.concatenate([wr_hi, wr_lo], axis=0), h_hi, nt, preferred_element_type=f32)
    r2 = lax.dot_general(wr_hi, h_lo, nt, preferred_element_type=f32)
    lg = r1[:ROUTER_ROWS] + r1[ROUTER_ROWS:] + r2 + br_ref[:, 0:1]
    row = lax.broadcasted_iota(i32, (SUBLANES, tm), 0).astype(f32)
    big = float(SUBLANES)
    gl = jnp.where(row < N_EXPERT_GROUPS, lg[0:SUBLANES], NEG)
    gmax = jnp.max(gl, axis=0, keepdims=True)
    g_p = 1.0 / jnp.sum(jnp.exp(gl - gmax), axis=0, keepdims=True)
    gidx = jnp.min(jnp.where(gl == gmax, row, big), axis=0, keepdims=True)
    esel = jnp.zeros((SUBLANES, tm), f32)
    for g in range(N_EXPERT_GROUPS):
        esel = jnp.where(gidx == float(g), lg[SUBLANES * (g + 1):SUBLANES * (g + 2)], esel)
    e1 = jnp.max(esel, axis=0, keepdims=True)
    i1 = jnp.min(jnp.where(esel == e1, row, big), axis=0, keepdims=True)
    esel2 = jnp.where(row == i1, -jnp.inf, esel)
    e2 = jnp.max(esel2, axis=0, keepdims=True)
    i2 = jnp.min(jnp.where(esel2 == e2, row, big), axis=0, keepdims=True)
    t = jnp.exp(e2 - e1)
    w1 = g_p / (1.0 + t)
    ids = jnp.concatenate([gidx * EXPERTS_PER_GROUP + i1, gidx * EXPERTS_PER_GROUP + i2], axis=0).astype(i32)
    ids_ref[0] = ids
    gates_ref[0] = jnp.concatenate([w1, w1 * t], axis=0)
    e_iota = lax.broadcasted_iota(i32, (N_EXPERTS, tm), 0)
    onehot = jnp.where((e_iota == ids[0:1]) | (e_iota == ids[1:2]), 1.0, 0.0).astype(bf16)
    cnt_ref[0] = jnp.dot(onehot, jnp.ones((tm, LANES), bf16), preferred_element_type=f32)


def _out_route(oa, ob, x, gate1, shift2, scale2, w_a, w_b, g2, wr_t, br_b, *, tm):
    n, t, d = x.shape
    w_half = oa.shape[2]
    mod_rows = gate1.shape[1]
    tpn = t // tm

    def mod_spec():
        if mod_rows == 1:
            return pl.BlockSpec((1, 1, d), lambda b, i: (b, 0, 0))
        return pl.BlockSpec((1, tm, d), lambda b, i: (b, i, 0))

    full = lambda shp: pl.BlockSpec(shp, lambda b, i: (0,) * len(shp))
    tok = lambda w: pl.BlockSpec((1, tm, w), lambda b, i: (b, i, 0))
    return pl.pallas_call(
        _out_route_kernel,
        grid=(n, tpn),
        in_specs=[tok(w_half), tok(w_half), tok(d), mod_spec(), mod_spec(), mod_spec(),
                  full((w_half, d)), full((w_half, d)), full((1, d)),
                  full((ROUTER_ROWS, d)), full((ROUTER_ROWS, LANES))],
        out_specs=[tok(d), tok(d),
                   pl.BlockSpec((1, 2, tm), lambda b, i: (b * tpn + i, 0, 0)),
                   pl.BlockSpec((1, 2, tm), lambda b, i: (b * tpn + i, 0, 0)),
                   pl.BlockSpec((1, N_EXPERTS, LANES), lambda b, i: (b * tpn + i, 0, 0))],
        out_shape=[SDS((n, t, d), f32), SDS((n, t, d), bf16), SDS((n * tpn, 2, tm), i32),
                   SDS((n * tpn, 2, tm), f32), SDS((n * tpn, N_EXPERTS, LANES), f32)],
        compiler_params=_cparams(("arbitrary", "arbitrary")),
        name="out_route",
    )(oa, ob, x, gate1, shift2, scale2, w_a, w_b, g2, wr_t, br_b)


def _chunk_list(counts, src0, dst0, rows, kmax):
    cin = jnp.cumsum(counts, axis=1)
    cex = cin - counts
    k = jnp.arange(kmax, dtype=i32)
    owner = jnp.sum((cin[:, None, :] <= k[None, :, None]).astype(i32), axis=2)
    owner = jnp.minimum(owner, N_EXPERTS - 1)
    pick = (owner[:, :, None] == jnp.arange(N_EXPERTS, dtype=i32)).astype(i32)
    at_owner = lambda v: jnp.sum(pick * v[:, None, :], axis=2)
    within = (k[None, :] - at_owner(cex)) * rows
    src = at_owner(src0) + within
    dst = at_owner(dst0) + within
    return src.reshape(-1).astype(i32), dst.reshape(-1).astype(i32), cin[:, -1].astype(i32)


def _route_plan(cnt, nb_max, sorted_rows):
    pc = _round_up(cnt, RUN_ALIGN)
    toff = jnp.cumsum(pc, axis=1) - pc
    tot = jnp.sum(pc, axis=0)
    seg = _round_up(tot, EXPERT_BLOCK)
    seg_end = jnp.cumsum(seg)
    seg_start = seg_end - seg
    run_start = seg_start[None, :] + jnp.cumsum(pc, axis=0) - pc
    nb = seg_end[-1] // EXPERT_BLOCK
    lists = []
    done = jnp.zeros_like(pc)
    for rows in CHUNK_ROWS:
        n = (pc - done) // rows
        kmax = sorted_rows // rows if rows == CHUNK_ROWS[0] else N_EXPERTS
        lists.extend(_chunk_list(n, toff + done, run_start + done, rows, kmax))
        done = done + n * rows
    return dict(
        lists=lists,
        tail_start=(seg_start + tot).astype(i32), tail_chunks=((seg - tot) // RUN_ALIGN).astype(i32),
        tail_total=(jnp.sum(seg - tot) // RUN_ALIGN).reshape(1).astype(i32),
        seg_start=seg_start.astype(i32), seg_blocks=(seg // EXPERT_BLOCK).astype(i32),
        nb=nb.reshape(1).astype(i32),
        toff_v=jnp.broadcast_to(toff.astype(f32)[:, :, None], toff.shape + (LANES,)))


def _sorted_positions(ids_ref, toff_ref, tm):
    ids = ids_ref[0]
    idc = jnp.concatenate([ids[0:1], ids[1:2]], axis=1)
    n_blk = 2 * tm // LANES
    e_iota = lax.broadcasted_iota(i32, (N_EXPERTS, 2 * tm), 0)
    onehot = jnp.where(e_iota == idc, 1.0, 0.0)
    stacked = jnp.concatenate([onehot[:, k * LANES:(k + 1) * LANES] for k in range(n_blk)], axis=0).astype(bf16)
    src = lax.broadcasted_iota(i32, (LANES, LANES), 0)
    dst = lax.broadcasted_iota(i32, (LANES, LANES), 1)
    within = jnp.dot(stacked, jnp.where(src < dst, 1.0, 0.0).astype(bf16), preferred_element_type=f32)
    totals = jnp.dot(stacked, jnp.ones((LANES, LANES), bf16), preferred_element_type=f32)
    r = lax.broadcasted_iota(i32, (n_blk * N_EXPERTS, n_blk * N_EXPERTS), 0)
    c = lax.broadcasted_iota(i32, (n_blk * N_EXPERTS, n_blk * N_EXPERTS), 1)
    same_expert_earlier = ((r & (N_EXPERTS - 1)) == (c & (N_EXPERTS - 1))) & (c < r)
    block_off = jnp.dot(jnp.where(same_expert_earlier, 1.0, 0.0).astype(bf16), totals.astype(bf16),
                        preferred_element_type=f32)
    before2 = within + block_off
    before = jnp.concatenate([before2[k * N_EXPERTS:(k + 1) * N_EXPERTS] for k in range(n_blk)], axis=1)
    pos = jnp.sum(onehot * (before + toff_ref[0][:, 0:1]), axis=0, keepdims=True)
    return jnp.where(idc >= 0, pos, -1.0)


def _aligned(row):
    return row if isinstance(row, int) else pl.multiple_of(row, RUN_ALIGN)


def _run_chunk_copy(vmem_buf, hbm_buf, sem, vrow, hrow, to_hbm, rows=RUN_ALIGN):
    v = vmem_buf.at[pl.ds(_aligned(vrow), rows)]
    h = hbm_buf.at[pl.ds(_aligned(hrow), rows)]
    return pltpu.make_async_copy(v, h, sem) if to_hbm else pltpu.make_async_copy(h, v, sem)


def _tile_chunks(i, list_refs, vmem_buf, hbm_buf, sems, to_hbm, wait):
    for c, rows in enumerate(CHUNK_ROWS):
        src_ref, dst_ref, n_ref = list_refs[3 * c:3 * c + 3]
        stride = src_ref.shape[0] // n_ref.shape[0]

        def one(k, carry, rows=rows, src_ref=src_ref, dst_ref=dst_ref, stride=stride, sem=sems.at[c]):
            if wait:
                _run_chunk_copy(vmem_buf, hbm_buf, sem, 0, 0, to_hbm, rows).wait()
            else:
                _run_chunk_copy(vmem_buf, hbm_buf, sem, src_ref[i * stride + k], dst_ref[i * stride + k],
                                to_hbm, rows).start()
            return carry

        lax.fori_loop(0, n_ref[i], one, 0)


def _dispatch_kernel(*refs):
    lists = refs[:N_LIST]
    tstart_ref, tail_chunks_ref, ttot_ref, nb_ref = refs[N_LIST:N_LIST + 4]
    hm_ref, ht_ref, ids_ref, toff_ref, xs_ref, xbuf_sc, zero_sc, sem = refs[N_LIST + 4:]
    i = pl.program_id(0)
    last = pl.num_programs(0) - 1
    tm = hm_ref.shape[0]
    rt = xbuf_sc.shape[1]
    slot = i & 1
    sem_tail, sem_dead = sem.at[len(CHUNK_ROWS)], sem.at[len(CHUNK_ROWS) + 1]

    @pl.when(i == 0)
    def _():
        zero_sc[...] = jnp.zeros(zero_sc.shape, ROW_DTYPE)

    h = jnp.where(i == last, ht_ref[...], hm_ref[...])
    pos = _sorted_positions(ids_ref, toff_ref, tm)
    r_iota = lax.broadcasted_iota(i32, (rt, tm), 0).astype(f32)
    perm = jnp.where((r_iota == pos[:, :tm]) | (r_iota == pos[:, tm:]), 1.0, 0.0).astype(bf16)
    xbuf_sc[slot] = jnp.dot(perm, h, preferred_element_type=f32).astype(ROW_DTYPE)

    @pl.when(i > 0)
    def _():
        _tile_chunks(i - 1, lists, xbuf_sc.at[1 - slot], xs_ref, sem, True, wait=True)

    _tile_chunks(i, lists, xbuf_sc.at[slot], xs_ref, sem, True, wait=False)

    @pl.when(i == last)
    def _():
        _tile_chunks(i, lists, xbuf_sc.at[slot], xs_ref, sem, True, wait=True)

        def per_expert(e, c):
            def per_chunk(k, cc):
                _run_chunk_copy(zero_sc, xs_ref, sem_tail, 0, tstart_ref[e] + k * RUN_ALIGN, True).start()
                return cc
            lax.fori_loop(0, tail_chunks_ref[e], per_chunk, 0)
            return c
        lax.fori_loop(0, N_EXPERTS, per_expert, 0)

        def wait_one(k, c):
            _run_chunk_copy(zero_sc, xs_ref, sem_tail, 0, 0, True).wait()
            return c
        lax.fori_loop(0, ttot_ref[0], wait_one, 0)

        def dead_block(row):
            return pltpu.make_async_copy(zero_sc, xs_ref.at[pl.ds(row, EXPERT_BLOCK)], sem_dead)

        n_dead = xs_ref.shape[0] // EXPERT_BLOCK - nb_ref[0]

        def start_dead(k, c):
            dead_block(pl.multiple_of((nb_ref[0] + k) * EXPERT_BLOCK, EXPERT_BLOCK)).start()
            return c
        lax.fori_loop(0, n_dead, start_dead, 0)

        def wait_dead(k, c):
            dead_block(0).wait()
            return c
        lax.fori_loop(0, n_dead, wait_dead, 0)


def _sorted_rows(tm):
    return _round_up(2 * tm + N_EXPERTS * (RUN_ALIGN - 1), LANES)


def _dispatch(plan, h_main, h_tail, ids, toff_v, rows_total):
    tiles, _, tm = ids.shape
    d = h_main.shape[1]
    tiles_main = h_main.shape[0] // tm
    assert tiles == tiles_main + 1 and h_tail.shape[0] == tm
    rt = _sorted_rows(tm)
    grid_spec = pltpu.PrefetchScalarGridSpec(
        num_scalar_prefetch=N_LIST + 4,
        grid=(tiles,),
        in_specs=[pl.BlockSpec((tm, d), lambda i, *_: (jnp.minimum(i, tiles_main - 1), 0)),
                  pl.BlockSpec((tm, d), lambda i, *_: (0, 0)),
                  pl.BlockSpec((1, 2, tm), lambda i, *_: (i, 0, 0)),
                  pl.BlockSpec((1, N_EXPERTS, LANES), lambda i, *_: (i, 0, 0))],
        out_specs=pl.BlockSpec(memory_space=pl.ANY),
        scratch_shapes=[pltpu.VMEM((2, rt, d), ROW_DTYPE),
                        pltpu.VMEM((EXPERT_BLOCK, d), ROW_DTYPE), pltpu.SemaphoreType.DMA((len(CHUNK_ROWS) + 2,))],
    )
    return pl.pallas_call(
        _dispatch_kernel,
        grid_spec=grid_spec,
        out_shape=SDS((rows_total, d), ROW_DTYPE),
        compiler_params=_cparams(("arbitrary",)),
        name="dispatch",
    )(*plan["lists"], plan["tail_start"], plan["tail_chunks"], plan["tail_total"], plan["nb"],
      h_main, h_tail, ids, toff_v)


def _experts_kernel(seg_ref, nblk_ref, nb_ref, x_ref, wg_ref, wu_ref, wd_ref, y_ref,
                    wg_sc, wu_sc, wd_sc, xbuf, ybuf, sem_in, sem_out):
    e = pl.program_id(0)
    eb = EXPERT_BLOCK
    nblk = nblk_ref[e]
    g0 = seg_ref[e] // eb
    n_all = nb_ref[0]

    def rows(g):
        return pl.ds(pl.multiple_of(g * eb, eb), eb)

    def x_copy(g):
        slot = lax.rem(g, X_AHEAD + 1)
        return pltpu.make_async_copy(x_ref.at[rows(g)], xbuf.at[slot], sem_in.at[slot])

    def y_copy(g):
        return pltpu.make_async_copy(ybuf.at[g & 1], y_ref.at[rows(g)], sem_out.at[g & 1])

    @pl.when(e == 0)
    def _():
        for g in range(X_AHEAD):
            @pl.when(g < n_all)
            def _():
                x_copy(g).start()

    @pl.when(nblk > 0)
    def _():
        wg_sc[...] = wg_ref[0].astype(bf16)
        wu_sc[...] = wu_ref[0].astype(bf16)
        wd_sc[...] = wd_ref[0].astype(bf16)

        def block(b, c):
            g = g0 + b
            x_copy(g).wait()

            @pl.when(g + X_AHEAD < n_all)
            def _():
                x_copy(g + X_AHEAD).start()

            xb = xbuf[lax.rem(g, X_AHEAD + 1)]
            a = jnp.dot(xb, wg_sc[...], preferred_element_type=f32)
            u = jnp.dot(xb, wu_sc[...], preferred_element_type=f32)
            hid = (jax.nn.silu(a) * u).astype(bf16)
            y = jnp.dot(hid, wd_sc[...], preferred_element_type=f32)

            @pl.when(g >= 2)
            def _():
                y_copy(g - 2).wait()

            ybuf[g & 1] = y.astype(ROW_DTYPE)
            y_copy(g).start()
            return c

        lax.fori_loop(0, nblk, block, 0)

    @pl.when(e == pl.num_programs(0) - 1)
    def _():
        @pl.when(n_all >= 2)
        def _():
            y_copy(n_all - 2).wait()

        @pl.when(n_all >= 1)
        def _():
            y_copy(n_all - 1).wait()

        def dead_block(b):
            return pltpu.make_async_copy(ybuf.at[0], y_ref.at[pl.ds(pl.multiple_of(b * eb, eb), eb)], sem_out.at[0])

        n_dead = y_ref.shape[0] // eb - nb_ref[0]
        ybuf[0] = jnp.zeros(ybuf.shape[1:], ROW_DTYPE)

        def start_dead(k, c):
            dead_block(nb_ref[0] + k).start()
            return c
        lax.fori_loop(0, n_dead, start_dead, 0)

        def wait_dead(k, c):
            dead_block(0).wait()
            return c
        lax.fori_loop(0, n_dead, wait_dead, 0)


def _experts(plan, xs, w_gate, w_up, w_down):
    rows_total, d = xs.shape
    n_exp, _, de = w_gate.shape
    eb = EXPERT_BLOCK
    grid_spec = pltpu.PrefetchScalarGridSpec(
        num_scalar_prefetch=3,
        grid=(n_exp,),
        in_specs=[pl.BlockSpec(memory_space=pl.ANY),
                  pl.BlockSpec((1, d, de), lambda e, *_: (e, 0, 0)),
                  pl.BlockSpec((1, d, de), lambda e, *_: (e, 0, 0)),
                  pl.BlockSpec((1, de, d), lambda e, *_: (e, 0, 0))],
        out_specs=pl.BlockSpec(memory_space=pl.ANY),
        scratch_shapes=[pltpu.VMEM((d, de), bf16), pltpu.VMEM((d, de), bf16), pltpu.VMEM((de, d), bf16),
                        pltpu.VMEM((X_AHEAD + 1, eb, d), ROW_DTYPE), pltpu.VMEM((2, eb, d), ROW_DTYPE),
                        pltpu.SemaphoreType.DMA((X_AHEAD + 1,)), pltpu.SemaphoreType.DMA((2,))],
    )
    return pl.pallas_call(
        _experts_kernel,
        grid_spec=grid_spec,
        out_shape=SDS((rows_total, d), ROW_DTYPE),
        compiler_params=_cparams(("arbitrary",)),
        name="experts",
    )(plan["seg_start"], plan["seg_blocks"], plan["nb"], xs, w_gate, w_up, w_down)


def _combine_kernel(*refs, final):
    lists = refs[:N_LIST]
    (xm_ref, xt_ref, gm_ref, gt_ref, gf_ref, ids_ref, gates_ref, toff_ref, y_ref,
     om_ref, ot_ref, ybuf_sc, sem) = refs[N_LIST:]
    i = pl.program_id(0)
    last = pl.num_programs(0) - 1
    tm = xm_ref.shape[0]
    rt = ybuf_sc.shape[1]
    slot = i & 1

    @pl.when(i == 0)
    def _():
        ybuf_sc[...] = jnp.zeros(ybuf_sc.shape, ROW_DTYPE)
        _tile_chunks(0, lists, ybuf_sc.at[0], y_ref, sem, False, wait=False)

    _tile_chunks(i, lists, ybuf_sc.at[slot], y_ref, sem, False, wait=True)

    @pl.when(i < last)
    def _():
        _tile_chunks(i + 1, lists, ybuf_sc.at[1 - slot], y_ref, sem, False, wait=False)

    pos = _sorted_positions(ids_ref, toff_ref, tm)
    gates = gates_ref[0]
    r_iota = lax.broadcasted_iota(i32, (rt, tm), 0).astype(f32)
    weights = (jnp.where(r_iota == pos[:, :tm], gates[0:1], 0.0)
               + jnp.where(r_iota == pos[:, tm:], gates[1:2], 0.0)).astype(bf16)
    ff = lax.dot_general(weights, ybuf_sc[slot], (((0,), (0,)), ((), ())), preferred_element_type=f32)

    def finish(x1, gate):
        x2 = x1 + gate * ff
        return _rms(x2, gf_ref[...]) if final else x2

    @pl.when(i < last)
    def _():
        om_ref[...] = finish(xm_ref[...], gm_ref[0])

    @pl.when(i == last)
    def _():
        ot_ref[...] = finish(xt_ref[...], gt_ref[...])


def _combine(plan, x_main, x_tail, gate_main, gate_tail, g_final, ids, gates, toff_v, y, *, final):
    tiles, _, tm = ids.shape
    d = x_main.shape[1]
    tiles_main = x_main.shape[0] // tm
    tpr = tiles_main // gate_main.shape[0]
    rt = _sorted_rows(tm)
    main_idx = lambda i, *_: (jnp.minimum(i, tiles_main - 1), 0)
    tail_spec = pl.BlockSpec((tm, d), lambda i, *_: (0, 0))
    slot_spec = pl.BlockSpec((1, 2, tm), lambda i, *_: (i, 0, 0))
    grid_spec = pltpu.PrefetchScalarGridSpec(
        num_scalar_prefetch=N_LIST,
        grid=(tiles,),
        in_specs=[pl.BlockSpec((tm, d), main_idx), tail_spec,
                  pl.BlockSpec((1, 1, d), lambda i, *_: (jnp.minimum(i, tiles_main - 1) // tpr, 0, 0)), tail_spec,
                  pl.BlockSpec((1, d), lambda i, *_: (0, 0)), slot_spec, slot_spec,
                  pl.BlockSpec((1, N_EXPERTS, LANES), lambda i, *_: (i, 0, 0)),
                  pl.BlockSpec(memory_space=pl.ANY)],
        out_specs=[pl.BlockSpec((tm, d), main_idx), tail_spec],
        scratch_shapes=[pltpu.VMEM((2, rt, d), ROW_DTYPE), pltpu.SemaphoreType.DMA((len(CHUNK_ROWS),))],
    )
    return pl.pallas_call(
        functools.partial(_combine_kernel, final=final),
        grid_spec=grid_spec,
        out_shape=[SDS(x_main.shape, f32), SDS((tm, d), f32)],
        compiler_params=_cparams(("arbitrary",)),
        name="combine",
    )(*plan["lists"], x_main, x_tail, gate_main, gate_tail, g_final, ids, gates, toff_v, y)


def _moe(main, tail, g_final, w_gate, w_up, w_down, *, tm, final):
    n, t, d = main["x1"].shape
    t_tail = tail["x1"].shape[1]
    pad = tm - t_tail
    tiles = n * t // tm + 1
    rows_max = 2 * (n * t + t_tail) + tiles * N_EXPERTS * (RUN_ALIGN - 1) + N_EXPERTS * (EXPERT_BLOCK - 1)
    nb_max = -(-rows_max // EXPERT_BLOCK)
    rows2 = lambda a: jnp.pad(a.reshape(t_tail, d), ((0, pad), (0, 0)))
    ids = jnp.concatenate([main["ids"], jnp.pad(tail["ids"], ((0, 0), (0, 0), (0, pad)), constant_values=-1)])
    gates = jnp.concatenate([main["gates"], jnp.pad(tail["gates"], ((0, 0), (0, 0), (0, pad)))])
    cnt = jnp.concatenate([main["cnt"], tail["cnt"]])[:, :, 0].astype(i32)
    plan = _route_plan(cnt, nb_max, _sorted_rows(tm))
    xs = _dispatch(plan, main["h2"].reshape(n * t, d), rows2(tail["h2"]), ids, plan["toff_v"],
                   nb_max * EXPERT_BLOCK)
    y = _experts(plan, xs, w_gate, w_up, w_down)
    out_m, out_t = _combine(plan, main["x1"].reshape(n * t, d), rows2(tail["x1"]), main["gate"],
                            rows2(tail["gate"]), g_final, ids, gates, plan["toff_v"], y, final=final)
    return out_m.reshape(n, t, d), out_t[:t_tail].reshape(1, t_tail, d)


def _before_attention(x, mods, tables, p, *, tm, mix, emit_vn):
    ws_t, mask, bs_b = mix
    return _in_proj(x, mods[0], mods[1], p["g1"], p["w_in"], tables, ws_t, mask, bs_b, p["g_v"], p["g_mlp"],
                    tm=tm, emit_vn=emit_vn)


def _after_attention(x, oa, ob, mods, p, *, tm):
    x1, h2, ids, gates, cnt = _out_route(oa, ob, x, mods[2], mods[3], mods[4], p["w_out_a"], p["w_out_b"],
                                         p["g2"], p["wr_t"], p["br_b"], tm=tm)
    return dict(x1=x1, h2=h2, ids=ids, gates=gates, cnt=cnt, gate=mods[5])


def kernel(x_prompt, x_sample, cache_k, cache_v, page_table, c_prompt, c_sample, w_ada, b_ada, g_norm1, w_in,
           lam_p, g_attn, g_v, w_s, b_s, g_mlp, w_out, g_norm2, w_router_g, b_router_g, w_router_e, b_router_e,
           w_exp_gate, w_exp_up, w_exp_down, g_final):
    nb, seq, d = x_prompt.shape
    ndec, dseq, _ = x_sample.shape
    depth = w_ada.shape[0]
    past_len = page_table.shape[1] * PAGE
    n_pool = cache_k.shape[1]
    a_w = N_HEADS * DV
    tm_s = ndec * dseq

    tables_p, tables_s8 = _rope_tables(seq, dseq, past_len)
    tables_s = tuple(jnp.tile(t, (ndec, 1)) for t in tables_s8)
    c_all = jnp.concatenate([c_prompt, c_sample], axis=0)
    c_pad = _round_up(c_all.shape[0], SUBLANES) - c_all.shape[0]
    c_all = jnp.pad(c_all, ((0, c_pad), (0, 0)))

    tri = jnp.tril(jnp.ones((CHUNK, CHUNK), f32))
    idx = jnp.arange(tm_s)
    mask_s = ((idx[:, None] // dseq == idx[None, :] // dseq) & (idx[None, :] % dseq <= idx[:, None] % dseq)).astype(f32)
    sel_s = (idx[:, None] % dseq == jnp.arange(dseq)[None, :]).astype(f32)

    xp, xs = x_prompt, x_sample.reshape(1, tm_s, d)
    kp_l, vp_l, ks_l, vs_l, cv_l = [], [], [], [], []
    for l in range(depth):
        lam_init = 0.8 - 0.6 * math.exp(-0.3 * l)
        mod = _adaln(c_all, w_ada[l], b_ada[l])
        mods_p = [mod[:nb, None, j * d:(j + 1) * d] for j in range(6)]
        mods_s = [jnp.repeat(mod[nb:nb + ndec, j * d:(j + 1) * d], dseq, axis=0)[None] for j in range(6)]
        wr_t = jnp.concatenate([w_router_g[l].T, jnp.zeros((SUBLANES - N_EXPERT_GROUPS, d), f32),
                                w_router_e[l].T], axis=0)
        br = jnp.concatenate([b_router_g[l], jnp.zeros((SUBLANES - N_EXPERT_GROUPS,), f32), b_router_e[l]])
        p = dict(
            g1=g_norm1[l].reshape(1, d), w_in=w_in[l].astype(bf16), g_v=g_v[l].reshape(1, -1),
            g_mlp=g_mlp[l].reshape(1, -1), w_out_a=w_out[l][:a_w].astype(bf16), w_out_b=w_out[l][a_w:].astype(bf16),
            g2=g_norm2[l].reshape(1, d), wr_t=wr_t, br_b=jnp.broadcast_to(br[:, None], (ROUTER_ROWS, LANES)),
            w_gate=w_exp_gate[l], w_up=w_exp_up[l], w_down=w_exp_down[l])
        gf = g_final.reshape(1, d)
        mix_p = (w_s[l], tri, jnp.broadcast_to(b_s[l][:, :, None], (N_GROUPS_B, CHUNK, C_B)))
        rep = lambda eq, *ops: jnp.einsum(eq, *ops, precision=lax.Precision.HIGHEST)
        mix_s = (rep("ia,gab,jb->gij", sel_s, w_s[l][:, :dseq, :dseq], sel_s), mask_s,
                 jnp.broadcast_to(rep("ia,ga->gi", sel_s, b_s[l][:, :dseq])[:, :, None], (N_GROUPS_B, tm_s, C_B)))

        ck = jnp.transpose(cache_k[l], (0, 2, 3, 4, 1)).reshape(n_pool, -1, PAGE)
        cv = cache_v[l].reshape(n_pool, PAGE * N_HEADS, DV)

        q_p, k_p, v_p, kb_p, vb_p, ob_p = _before_attention(xp, mods_p, tables_p, p, tm=TOKEN_TILE, mix=mix_p,
                                                            emit_vn=False)
        q_s, k_s, v_s, _, _, ob_s, cv_s = _before_attention(xs, mods_s, tables_s, p, tm=tm_s, mix=mix_s,
                                                            emit_vn=True)
        per_seq = lambda a: a.reshape(ndec, dseq, -1)
        oa_p, oa_s = _attention(page_table, lam_p[l], g_attn[l], q_p, kb_p, vb_p, per_seq(q_s), per_seq(k_s),
                                per_seq(v_s), ck, cv, lam_init)
        routed_p = _after_attention(xp, oa_p, ob_p, mods_p, p, tm=TOKEN_TILE)
        routed_s = _after_attention(xs, oa_s.reshape(1, tm_s, -1), ob_s, mods_s, p, tm=tm_s)
        xp, xs = _moe(routed_p, routed_s, gf, p["w_gate"], p["w_up"], p["w_down"], tm=TOKEN_TILE,
                      final=l == depth - 1)
        kp_l.append(k_p.reshape(nb, seq, N_HEADS, 2, DK))
        vp_l.append(v_p.reshape(nb, seq, N_HEADS, DV))
        ks_l.append(k_s.reshape(ndec, dseq, N_HEADS, 2, DK))
        vs_l.append(v_s.reshape(ndec, dseq, N_HEADS, DV))
        cv_l.append(cv_s.reshape(ndec, dseq, -1))
    return (xp, xs.reshape(ndec, dseq, d), jnp.stack(kp_l), jnp.stack(vp_l), jnp.stack(ks_l), jnp.stack(vs_l),
            jnp.stack(cv_l))
```
